```python
import jax
import jax.numpy as jnp
from jax import lax
import numpy as np

D_MODEL = 1024
BATCH = 4
SEQ = 8192
DEPTH = 1

GRID_W = 64
CTX_LEN = 256
N_MOD = 6
EPS = 1e-6
ROPE_BASE = 10000.0
Q_BLOCK = 128
RET_CHUNK = 128

MLA_HEADS = 8
MLA_NOPE = 64
MLA_ROPE = 32
MLA_QK = MLA_NOPE + MLA_ROPE
MLA_V = 64
Q_LORA = 256
KV_LORA = 128
RET_HEADS = 4
RET_DK = 64
RET_DV = 128
MIX_WIDTH = MLA_HEADS * MLA_V + RET_HEADS * RET_DV

OFF_Q = 0
OFF_KV = OFF_Q + Q_LORA
OFF_PE = OFF_KV + KV_LORA
OFF_RQ = OFF_PE + MLA_ROPE
OFF_RK = OFF_RQ + RET_HEADS * RET_DK
OFF_RV = OFF_RK + RET_HEADS * RET_DK
OFF_RG = OFF_RV + RET_HEADS * RET_DV
D_IN = OFF_RG + RET_HEADS * RET_DV

N_EXPERTS = 32
TOP_K = 4
D_FF = D_MODEL
SWIGLU_LIMIT = 7.0
SWIGLU_ALPHA = 1.702
MOE_BLOCK = 128

kernel_name = 'hybrid_mla_retention_moe_prefix_dit_layer'


def rms_norm(x, w):
    xf = x.astype(jnp.float32)
    y = xf * lax.rsqrt(jnp.mean(xf * xf, axis=-1, keepdims=True) + EPS)
    return (y * w.astype(jnp.float32)).astype(x.dtype)


def modulate(h, shift, scale):
    return h * (1 + scale) + shift


def axial_rope_tables(L, dim):
    rows = L // GRID_W
    nf = dim // 4
    inv = jnp.power(ROPE_BASE, -jnp.arange(nf, dtype=jnp.float32) / nf)
    row = jnp.repeat(jnp.arange(rows, dtype=jnp.float32), GRID_W)
    col = jnp.tile(jnp.arange(GRID_W, dtype=jnp.float32), rows)
    pos = jnp.stack([row, col], axis=-1)
    ang = pos[:, :, None] * inv
    ang = jnp.broadcast_to(ang[:, :, None, :], (L, 2, 2, nf)).reshape(L, dim)
    return jnp.cos(ang), jnp.sin(ang)


def apply_axial_rope(x, cos, sin):
    d = x.shape[-1]
    xa = x.reshape(x.shape[:-1] + (2, 2, d // 4))
    rot = jnp.stack([-xa[..., 1, :], xa[..., 0, :]], axis=-2).reshape(x.shape)
    return (x.astype(jnp.float32) * cos + rot.astype(jnp.float32) * sin).astype(x.dtype)


def merge_heads(o):
    B, H, L, dv = o.shape
    return o.transpose(0, 2, 1, 3).reshape(B, L, H * dv)


def mla_q(p, g_q_lora, w_q_up, g_q_head, rope):
    B, L, _ = p.shape
    c_q = rms_norm(p[..., OFF_Q:OFF_Q + Q_LORA], g_q_lora)
    q = (c_q @ w_q_up).reshape(B, L, MLA_HEADS, MLA_QK)
    q = rms_norm(q, g_q_head).transpose(0, 2, 1, 3)
    if rope is not None:
        q = jnp.concatenate([q[..., :MLA_NOPE], apply_axial_rope(q[..., MLA_NOPE:], *rope)], axis=-1)
    return q


def mla_kv(p, g_kv_lora, w_kv_up, g_k_head, rope):
    B, L, _ = p.shape
    c_kv = rms_norm(p[..., OFF_KV:OFF_KV + KV_LORA], g_kv_lora)
    k_pe = p[..., OFF_PE:OFF_PE + MLA_ROPE]
    kv = (c_kv @ w_kv_up).reshape(B, L, MLA_HEADS, MLA_NOPE + MLA_V)
    k_nope, v = kv[..., :MLA_NOPE], kv[..., MLA_NOPE:]
    k_pe = jnp.broadcast_to(k_pe[:, :, None, :], (B, L, MLA_HEADS, MLA_ROPE))
    k = rms_norm(jnp.concatenate([k_nope, k_pe], axis=-1), g_k_head).transpose(0, 2, 1, 3)
    if rope is not None:
        k = jnp.concatenate([k[..., :MLA_NOPE], apply_axial_rope(k[..., MLA_NOPE:], *rope)], axis=-1)
    return k, v.transpose(0, 2, 1, 3)


def attend_blocks(q, k, v):
    B, H, Lq, dq = q.shape
    nb = Lq // Q_BLOCK
    scale = dq ** -0.5
    qb = q.reshape(B, H, nb, Q_BLOCK, dq).transpose(2, 0, 1, 3, 4)

    def one_block(qi):
        s = jnp.einsum('bhqd,bhkd->bhqk', qi, k).astype(jnp.float32) * scale
        pr = jax.nn.softmax(s, axis=-1).astype(v.dtype)
        return jnp.einsum('bhqk,bhkd->bhqd', pr, v)

    o = lax.map(one_block, qb)
    return o.transpose(1, 2, 0, 3, 4).reshape(B, H, Lq, v.shape[-1])


def ret_q(p, rope):
    B, L, _ = p.shape
    q = p[..., OFF_RQ:OFF_RK].reshape(B, L, RET_HEADS, RET_DK).transpose(0, 2, 1, 3)
    return q if rope is None else apply_axial_rope(q, *rope)


def ret_kv(p, rope):
    B, L, _ = p.shape
    k = p[..., OFF_RK:OFF_RV].reshape(B, L, RET_HEADS, RET_DK).transpose(0, 2, 1, 3) * (RET_DK ** -0.5)
    v = p[..., OFF_RV:OFF_RG].reshape(B, L, RET_HEADS, RET_DV).transpose(0, 2, 1, 3)
    if rope is not None:
        k = apply_axial_rope(k, *rope)
    return k, v


def retention_state(k, v, log_gamma, reverse):
    L = k.shape[2]
    j = jnp.arange(L, dtype=jnp.float32)
    expo = j if reverse else (L - 1) - j
    w = jnp.exp(expo[None, :] * log_gamma[:, None])
    return jnp.einsum('bhld,bhle->bhde', k.astype(jnp.float32) * w[None, :, :, None],
                      v.astype(jnp.float32))


def retention_scan(q, k, v, log_gamma, s0):
    B, H, L, DK = q.shape
    DV = v.shape[-1]
    C = RET_CHUNK
    n = L // C
    to_chunks = lambda t: t.astype(jnp.float32).reshape(B, H, n, C, t.shape[-1]).transpose(2, 0, 1, 3, 4)
    i = jnp.arange(C, dtype=jnp.float32)
    diff = i[:, None] - i[None, :]
    causal = diff >= 0
    decay_mask = jnp.where(causal, jnp.exp(jnp.where(causal, diff, 0.0)[None] * log_gamma[:, None, None]), 0.0)
    q_decay = jnp.exp((i + 1.0)[None, :] * log_gamma[:, None])[..., None]
    k_decay = jnp.exp(((C - 1.0) - i)[None, :] * log_gamma[:, None])[..., None]
    chunk_decay = jnp.exp(C * log_gamma)[:, None, None]

    def step(s, inp):
        qc, kc, vc = inp
        inner = jnp.einsum('bhid,bhjd->bhij', qc, kc) * decay_mask
        o = jnp.einsum('bhij,bhje->bhie', inner, vc) + jnp.einsum('bhid,bhde->bhie', qc * q_decay, s)
        s_new = s * chunk_decay + jnp.einsum('bhjd,bhje->bhde', kc * k_decay, vc)
        return s_new, o

    _, o = lax.scan(step, s0.astype(jnp.float32), (to_chunks(q), to_chunks(k), to_chunks(v)))
    return o.transpose(1, 2, 0, 3, 4).reshape(B, H, L, DV)


def retention_bidir(q, k, v, log_gamma, s0_f, s0_b):
    flip = lambda t: jnp.flip(t, axis=2)
    o_f = retention_scan(q, k, v, log_gamma[0], s0_f)
    o_b = flip(retention_scan(flip(q), flip(k), flip(v), log_gamma[1], s0_b))
    return o_f + o_b


def ret_output(o, p, g_ret_out):
    B, H, L, DV = o.shape
    o = rms_norm(o.transpose(0, 2, 1, 3), g_ret_out.reshape(H, DV)).reshape(B, L, H * DV)
    gate = p[..., OFF_RG:D_IN].astype(jnp.float32)
    return (o * jax.nn.silu(gate)).astype(p.dtype)


def moe_ffn(h, w_router, b_router, w1, b1, w2, b2):
    T, D = h.shape
    N = T * TOP_K
    logits = (h @ w_router).astype(jnp.float32) + b_router.astype(jnp.float32)
    top_val, top_idx = lax.top_k(logits, TOP_K)
    gates = jax.nn.softmax(top_val, axis=-1)
    flat_e = top_idx.reshape(N).astype(jnp.int32)
    flat_g = gates.reshape(N)
    ar = jnp.arange(N, dtype=jnp.int32)
    flat_tok = ar // TOP_K
    order = jnp.argsort(flat_e)
    sorted_e = flat_e[order]
    counts = jax.ops.segment_sum(jnp.ones((N,), jnp.int32), flat_e, num_segments=N_EXPERTS)
    padded = ((counts + MOE_BLOCK - 1) // MOE_BLOCK) * MOE_BLOCK
    pad_end = jnp.cumsum(padded)
    pad_start = pad_end - padded
    start = jnp.cumsum(counts) - counts
    dest = pad_start[sorted_e] + (ar - start[sorted_e])
    n_blocks = -(-N // MOE_BLOCK) + N_EXPERTS
    n_pad = n_blocks * MOE_BLOCK
    buf_tok = jnp.full((n_pad,), T, jnp.int32).at[dest].set(flat_tok[order])
    buf_g = jnp.zeros((n_pad,), jnp.float32).at[dest].set(flat_g[order])
    block_e = jnp.minimum(jnp.searchsorted(pad_end, jnp.arange(n_blocks, dtype=jnp.int32) * MOE_BLOCK,
                                           side='right'), N_EXPERTS - 1).astype(jnp.int32)
    h_pad = jnp.concatenate([h, jnp.zeros((1, D), h.dtype)], axis=0)

    def expert_block(args):
        tok, g, e = args
        xb = h_pad[tok]
        u = (xb @ w1[e] + b1[e]).astype(jnp.float32)
        glu = jnp.minimum(u[:, 0::2], SWIGLU_LIMIT)
        lin = jnp.clip(u[:, 1::2], -SWIGLU_LIMIT, SWIGLU_LIMIT)
        act = glu * jax.nn.sigmoid(SWIGLU_ALPHA * glu) * (lin + 1.0)
        y = act.astype(xb.dtype) @ w2[e] + b2[e]
        return y.astype(jnp.float32) * g[:, None]

    ys = lax.map(expert_block, (buf_tok.reshape(n_blocks, MOE_BLOCK),
                                buf_g.reshape(n_blocks, MOE_BLOCK), block_e))
    out = jnp.zeros((T + 1, D), jnp.float32).at[buf_tok].add(ys.reshape(n_pad, D))[:T]
    return out.astype(h.dtype)


def setup_inputs(seed: int = 0) -> dict:
    key = jax.random.key(seed)
    ks = jax.random.split(key, 24)
    f32 = jnp.float32

    def nrm(k, shape, scale):
        return jax.random.normal(k, shape, f32) * scale

    def gain(k, shape):
        return 1.0 + 0.1 * jax.random.normal(k, shape, f32)

    dec0 = jnp.log(jnp.power(2.0, 5.0 + jnp.arange(RET_HEADS, dtype=f32)) - 1.0)
    return {
        'x': nrm(ks[0], (BATCH, SEQ, D_MODEL), 1.0),
        'c': nrm(ks[1], (BATCH, D_MODEL), 1.0),
        'ctx': nrm(ks[2], (BATCH, CTX_LEN, D_MODEL), 1.0),
        'c_ctx': nrm(ks[3], (D_MODEL,), 1.0),
        'g_attn': gain(ks[4], (DEPTH, D_MODEL)),
        'g_ffn': gain(ks[5], (DEPTH, D_MODEL)),
        'w_ada': nrm(ks[6], (DEPTH, D_MODEL, N_MOD * D_MODEL), 0.5 * D_MODEL ** -0.5),
        'b_ada': nrm(ks[7], (DEPTH, N_MOD * D_MODEL), 0.02),
        'w_in': nrm(ks[8], (DEPTH, D_MODEL, D_IN), D_MODEL ** -0.5),
        'g_q_lora': gain(ks[9], (DEPTH, Q_LORA)),
        'w_q_up': nrm(ks[10], (DEPTH, Q_LORA, MLA_HEADS * MLA_QK), Q_LORA ** -0.5),
        'g_q_head': gain(ks[11], (DEPTH, MLA_QK)),
        'g_kv_lora': gain(ks[12], (DEPTH, KV_LORA)),
        'w_kv_up': nrm(ks[13], (DEPTH, KV_LORA, MLA_HEADS * (MLA_NOPE + MLA_V)), KV_LORA ** -0.5),
        'g_k_head': gain(ks[14], (DEPTH, MLA_QK)),
        'ret_decay_logit': dec0 + nrm(ks[15], (DEPTH, 2, RET_HEADS), 0.1),
        'g_ret_out': gain(ks[16], (DEPTH, RET_HEADS * RET_DV)),
        'w_out': nrm(ks[17], (DEPTH, MIX_WIDTH, D_MODEL), MIX_WIDTH ** -0.5),
        'w_router': nrm(ks[18], (DEPTH, D_MODEL, N_EXPERTS), D_MODEL ** -0.5),
        'b_router': nrm(ks[19], (DEPTH, N_EXPERTS), 0.01),
        'w_mlp1': nrm(ks[20], (DEPTH, N_EXPERTS, D_MODEL, 2 * D_FF), D_MODEL ** -0.5),
        'b_mlp1': nrm(ks[21], (DEPTH, N_EXPERTS, 2 * D_FF), 0.02),
        'w_mlp2': nrm(ks[22], (DEPTH, N_EXPERTS, D_FF, D_MODEL), D_FF ** -0.5),
        'b_mlp2': nrm(ks[23], (DEPTH, N_EXPERTS, D_MODEL), 0.02),
    }


def reference(x, c, ctx, c_ctx, g_attn, g_ffn, w_ada, b_ada, w_in, g_q_lora, w_q_up, g_q_head,
              g_kv_lora, w_kv_up, g_k_head, ret_decay_logit, g_ret_out, w_out,
              w_router, b_router, w_mlp1, b_mlp1, w_mlp2, b_mlp2):
    B, L, D = x.shape
    Lc = ctx.shape[1]
    rope_mla = axial_rope_tables(L, MLA_ROPE)
    rope_ret = axial_rope_tables(L, RET_DK)
    for l in range(DEPTH):
        last = l == DEPTH - 1
        mod = (jax.nn.silu(c) @ w_ada[l] + b_ada[l]).reshape(B, N_MOD, 1, D)
        mod_c = (jax.nn.silu(c_ctx) @ w_ada[l] + b_ada[l]).reshape(N_MOD, 1, D)

        h = modulate(rms_norm(x, g_attn[l]), mod[:, 0], mod[:, 1])
        hc = modulate(rms_norm(ctx, g_attn[l]), mod_c[0], mod_c[1])
        p = h @ w_in[l]
        pc = hc @ w_in[l]

        k_c, v_c = mla_kv(pc, g_kv_lora[l], w_kv_up[l], g_k_head[l], None)
        k_x, v_x = mla_kv(p, g_kv_lora[l], w_kv_up[l], g_k_head[l], rope_mla)
        q_x = mla_q(p, g_q_lora[l], w_q_up[l], g_q_head[l], rope_mla)
        o_mla = attend_blocks(q_x, jnp.concatenate([k_c, k_x], axis=2), jnp.concatenate([v_c, v_x], axis=2))

        log_gamma = jax.nn.log_sigmoid(ret_decay_logit[l].astype(jnp.float32))
        kr_c, vr_c = ret_kv(pc, None)
        s0_f = retention_state(kr_c, vr_c, log_gamma[0], reverse=False)
        s0_b = retention_state(kr_c, vr_c, log_gamma[1], reverse=True)
        kr_x, vr_x = ret_kv(p, rope_ret)
        o_ret = retention_bidir(ret_q(p, rope_ret), kr_x, vr_x, log_gamma, s0_f, s0_b)

        mix = jnp.concatenate([merge_heads(o_mla), ret_output(o_ret, p, g_ret_out[l])], axis=-1)
        x = x + mod[:, 2] * (mix @ w_out[l])
        if not last:
            o_mla_c = attend_blocks(mla_q(pc, g_q_lora[l], w_q_up[l], g_q_head[l], None), k_c, v_c)
            s_zero = jnp.zeros((B, RET_HEADS, RET_DK, RET_DV), jnp.float32)
            o_ret_c = retention_bidir(ret_q(pc, None), kr_c, vr_c, log_gamma, s_zero, s_zero)
            mix_c = jnp.concatenate([merge_heads(o_mla_c), ret_output(o_ret_c, pc, g_ret_out[l])], axis=-1)
            ctx = ctx + mod_c[2] * (mix_c @ w_out[l])

        hf = modulate(rms_norm(x, g_ffn[l]), mod[:, 3], mod[:, 4]).reshape(B * L, D)
        moe_w = (w_router[l], b_router[l], w_mlp1[l], b_mlp1[l], w_mlp2[l], b_mlp2[l])
        if last:
            x = x + mod[:, 5] * moe_ffn(hf, *moe_w).reshape(B, L, D)
        else:
            hcf = modulate(rms_norm(ctx, g_ffn[l]), mod_c[3], mod_c[4]).reshape(B * Lc, D)
            y = moe_ffn(jnp.concatenate([hf, hcf], axis=0), *moe_w)
            x = x + mod[:, 5] * y[:B * L].reshape(B, L, D)
            ctx = ctx + mod_c[5] * y[B * L:].reshape(B, Lc, D)
    return x
```

```python
import functools
import math

import jax
import jax.numpy as jnp
from jax import lax
from jax.experimental import pallas as pl
from jax.experimental.pallas import tpu as pltpu

F32 = jnp.float32
BF16 = jnp.bfloat16

LANES = 128
SUBLANES = 8
VMEM_LIMIT_BYTES = 56 * 1024 * 1024

EPS = 1e-6
ROPE_BASE = 10000.0
GRID_W = 64
N_MOD = 6
MLA_HEADS = 8
MLA_NOPE = 64
MLA_ROPE = 32
MLA_QK = MLA_NOPE + MLA_ROPE
MLA_V = 64
Q_LORA = 256
KV_LORA = 128
RET_HEADS = 4
RET_DK = 64
RET_DV = 128
N_EXPERTS = 32
TOP_K = 4
SWIGLU_LIMIT = 7.0
SWIGLU_ALPHA = 1.702
LOG2E = 1.4426950408889634

PROJ_TM = 512
ATT_TQ = 512
ATT_TK = 512
RET_C = 256
OUT_TM = 512
MOE_BLK = 512
DISP_T = 256
COMB_T = 256


def _cparams(sem):
    return pltpu.CompilerParams(dimension_semantics=sem, vmem_limit_bytes=VMEM_LIMIT_BYTES)


def _sigmoid(x):
    return 1.0 / (1.0 + jnp.exp(-x))


def _adaln_kernel(c_ref, w_ref, b_ref, o_ref):
    c = c_ref[...]
    s = (c * _sigmoid(c)).astype(BF16)
    o_ref[...] = jnp.dot(s, w_ref[...].astype(BF16), preferred_element_type=F32) + b_ref[...]


def _adaln(cc, w_ada, b_ada):
    rows, d = cc.shape
    n = w_ada.shape[1]
    tn = 1536
    return pl.pallas_call(
        _adaln_kernel,
        grid=(n // tn,),
        in_specs=[pl.BlockSpec((rows, d), lambda j: (0, 0)),
                  pl.BlockSpec((d, tn), lambda j: (0, j)),
                  pl.BlockSpec((1, tn), lambda j: (0, j))],
        out_specs=pl.BlockSpec((rows, tn), lambda j: (0, j)),
        out_shape=jax.ShapeDtypeStruct((rows, n), F32),
        compiler_params=_cparams(("arbitrary",)),
        name="adaln",
    )(cc, w_ada, b_ada)


def _rope(x, cos, sin, lo_mask, back, fwd):
    rot = jnp.where(lo_mask, pltpu.roll(x, back, 1), pltpu.roll(x, fwd, 1))
    return x * cos + rot * sin


def _in_proj_kernel(x_ref, mod_ref, gattn_ref, win_ref, gql_ref, wq_ref, gqh_ref,
                    gkvl_ref, wkv_ref, gkh_ref, cm_ref, sm_ref, cr_ref, sr_ref,
                    q_ref, k_ref, v_ref, rq_ref, rk_ref, rv_ref, rg_ref, *, q_scale):
    x = x_ref[0]
    shift = mod_ref[0, 0:1, :]
    scale = mod_ref[0, 1:2, :]
    ms = jnp.mean(x * x, axis=-1, keepdims=True)
    h = x * lax.rsqrt(ms + EPS) * gattn_ref[...]
    h = h * (1.0 + scale) + shift
    p = jnp.dot(h.astype(BF16), win_ref[...], preferred_element_type=F32)

    lane = lax.broadcasted_iota(jnp.int32, (1, LANES), 1)
    mla_lo = (lane % (MLA_ROPE // 2)) < (MLA_ROPE // 4)
    ret_lo = (lane % (RET_DK // 2)) < (RET_DK // 4)
    cm, sm = cm_ref[...], sm_ref[...]
    cr, sr = cr_ref[...], sr_ref[...]
    mla_back, mla_fwd = LANES - MLA_ROPE // 4, MLA_ROPE // 4
    ret_back, ret_fwd = LANES - RET_DK // 4, RET_DK // 4

    cq = p[:, 0:Q_LORA]
    cq = cq * lax.rsqrt(jnp.mean(cq * cq, axis=-1, keepdims=True) + EPS) * gql_ref[...]
    qf = jnp.dot(cq.astype(BF16), wq_ref[...], preferred_element_type=F32)
    for hd in range(MLA_HEADS):
        blk = qf[:, hd * LANES:(hd + 1) * LANES]
        r = lax.rsqrt(jnp.sum(blk * blk, axis=-1, keepdims=True) * (1.0 / MLA_QK) + EPS)
        blk = _rope(blk * r * gqh_ref[...], cm, sm, mla_lo, mla_back, mla_fwd)
        q_ref[0, hd] = (blk * q_scale).astype(BF16)

    o_kv = Q_LORA
    ckv = p[:, o_kv:o_kv + KV_LORA]
    ckv = ckv * lax.rsqrt(jnp.mean(ckv * ckv, axis=-1, keepdims=True) + EPS) * gkvl_ref[...]
    kvf = jnp.dot(ckv.astype(BF16), wkv_ref[...], preferred_element_type=F32)
    o_pe = o_kv + KV_LORA
    pe = p[:, o_pe:o_pe + LANES]
    ones_col = (lane == MLA_V).astype(F32)
    for hd in range(MLA_HEADS):
        kk = kvf[:, hd * LANES:(hd + 1) * LANES] + pe
        r = lax.rsqrt(jnp.sum(kk * kk, axis=-1, keepdims=True) * (1.0 / MLA_QK) + EPS)
        kk = _rope(kk * r * gkh_ref[...], cm, sm, mla_lo, mla_back, mla_fwd)
        k_ref[0, hd] = kk.astype(BF16)
        vv = kvf[:, (MLA_HEADS + hd) * LANES:(MLA_HEADS + hd + 1) * LANES] + ones_col
        v_ref[0, hd] = vv.astype(BF16)

    o_rq = o_pe + LANES
    n_qk = RET_HEADS * RET_DK
    for j in range(n_qk // LANES):
        blk = p[:, o_rq + j * LANES:o_rq + (j + 1) * LANES]
        rq_ref[0, :, j * LANES:(j + 1) * LANES] = _rope(
            blk, cr, sr, ret_lo, ret_back, ret_fwd).astype(BF16)
    o_rk = o_rq + n_qk
    for j in range(n_qk // LANES):
        blk = p[:, o_rk + j * LANES:o_rk + (j + 1) * LANES] * (RET_DK ** -0.5)
        rk_ref[0, :, j * LANES:(j + 1) * LANES] = _rope(
            blk, cr, sr, ret_lo, ret_back, ret_fwd).astype(BF16)
    o_rv = o_rk + n_qk
    n_v = RET_HEADS * RET_DV
    rv_ref[0] = p[:, o_rv:o_rv + n_v].astype(BF16)
    rg_ref[0] = p[:, o_rv + n_v:o_rv + 2 * n_v]


def _in_proj(x, mod3, mod_row_of_batch, g_attn, w_in_r, g_q_lora, w_q_r, gqh, g_kv_lora,
             w_kv_r, gkh, tabs, tm):
    B, L, D = x.shape
    cm, sm, cr, sr = tabs
    n_in = w_in_r.shape[1]
    const = lambda b, i: (0, 0)
    tab_spec = pl.BlockSpec((tm, LANES), lambda b, i: (i, 0))
    head_spec = pl.BlockSpec((1, MLA_HEADS, tm, LANES), lambda b, i: (b, 0, i, 0))
    n_qk = RET_HEADS * RET_DK
    n_v = RET_HEADS * RET_DV
    seq_spec = lambda w: pl.BlockSpec((1, tm, w), lambda b, i: (b, i, 0))
    head_shape = jax.ShapeDtypeStruct((B, MLA_HEADS, L, LANES), BF16)
    q_scale = MLA_QK ** -0.5 * LOG2E
    return pl.pallas_call(
        functools.partial(_in_proj_kernel, q_scale=q_scale),
        grid=(B, L // tm),
        in_specs=[
            pl.BlockSpec((1, tm, D), lambda b, i: (b, i, 0)),
            pl.BlockSpec((1, N_MOD, D), lambda b, i: (mod_row_of_batch(b), 0, 0)),
            pl.BlockSpec((1, D), const),
            pl.BlockSpec((D, n_in), const),
            pl.BlockSpec((1, Q_LORA), const),
            pl.BlockSpec(w_q_r.shape, const),
            pl.BlockSpec((1, LANES), const),
            pl.BlockSpec((1, KV_LORA), const),
            pl.BlockSpec(w_kv_r.shape, const),
            pl.BlockSpec((1, LANES), const),
            tab_spec, tab_spec, tab_spec, tab_spec,
        ],
        out_specs=[head_spec, head_spec, head_spec,
                   seq_spec(n_qk), seq_spec(n_qk), seq_spec(n_v), seq_spec(n_v)],
        out_shape=[head_shape, head_shape, head_shape,
                   jax.ShapeDtypeStruct((B, L, n_qk), BF16),
                   jax.ShapeDtypeStruct((B, L, n_qk), BF16),
                   jax.ShapeDtypeStruct((B, L, n_v), BF16),
                   jax.ShapeDtypeStruct((B, L, n_v), F32)],
        compiler_params=_cparams(("arbitrary", "arbitrary")),
        name="in_proj",
    )(x, mod3, g_attn, w_in_r, g_q_lora, w_q_r, gqh, g_kv_lora, w_kv_r, gkh, cm, sm, cr, sr)


def _attn_kernel(q_ref, kx_ref, vx_ref, kc_ref, vc_ref, o_ref, m_ref, acc_ref, *, tk):
    n_kv = kx_ref.shape[2] // tk
    dn = (((1,), (1,)), ((), ()))
    outs = []
    for hh in range(2):
        q = q_ref[0, hh]
        s = lax.dot_general(q, kc_ref[0, hh], dn, preferred_element_type=F32)
        m0 = jnp.max(s, axis=-1, keepdims=True)
        pr = jnp.exp2(s - m0)
        m_ref[...] = m0
        acc_ref[...] = jnp.dot(pr.astype(BF16), vc_ref[0, hh], preferred_element_type=F32)

        def body(j, carry, hh=hh, q=q):
            start = pl.multiple_of(j * tk, tk)
            kb = kx_ref[0, hh, pl.ds(start, tk), :]
            vb = vx_ref[0, hh, pl.ds(start, tk), :]
            s = lax.dot_general(q, kb, dn, preferred_element_type=F32)
            m_old = m_ref[...]
            m_new = jnp.maximum(m_old, jnp.max(s, axis=-1, keepdims=True))
            alpha = jnp.exp2(m_old - m_new)
            pr = jnp.exp2(s - m_new)
            acc_ref[...] = alpha * acc_ref[...] + jnp.dot(
                pr.astype(BF16), vb, preferred_element_type=F32)
            m_ref[...] = m_new
            return carry

        lax.fori_loop(0, n_kv, body, 0)
        acc = acc_ref[...]
        outs.append(acc[:, :MLA_V] / acc[:, MLA_V:MLA_V + 1])
    o_ref[0] = jnp.concatenate(outs, axis=-1).astype(BF16)


def _attention(q, kx, vx, kc, vc, tq, tk):
    B, H, L, _ = q.shape
    Lc = kc.shape[2]
    return pl.pallas_call(
        functools.partial(_attn_kernel, tk=tk),
        grid=(B, H // 2, L // tq),
        in_specs=[
            pl.BlockSpec((1, 2, tq, LANES), lambda b, h, i: (b, h, i, 0)),
            pl.BlockSpec((1, 2, L, LANES), lambda b, h, i: (b, h, 0, 0)),
            pl.BlockSpec((1, 2, L, LANES), lambda b, h, i: (b, h, 0, 0)),
            pl.BlockSpec((1, 2, Lc, LANES), lambda b, h, i: (b, h, 0, 0)),
            pl.BlockSpec((1, 2, Lc, LANES), lambda b, h, i: (b, h, 0, 0)),
        ],
        out_specs=pl.BlockSpec((1, tq, LANES), lambda b, h, i: (b, i, h)),
        out_shape=jax.ShapeDtypeStruct((B, L, H * MLA_V), BF16),
        scratch_shapes=[pltpu.VMEM((tq, 1), F32), pltpu.VMEM((tq, LANES), F32)],
        compiler_params=_cparams(("arbitrary", "arbitrary", "arbitrary")),
        name="attention",
    )(q, kx, vx, kc, vc)


def _ret_kernel(lg_ref, lgf_ref, lgb_ref, lgvf_ref, lgvb_ref, gout_ref, rq_ref, rk_ref, rv_ref, rg_ref,
                kc_ref, vc_ref, o_ref, dm_ref, qdf_ref, qdb_ref, kdf_ref, kdb_ref,
                f_ref, r_ref, rs_ref, *, n_chunks):
    C = rq_ref.shape[1]
    Lc = kc_ref.shape[1]
    ps = pl.program_id(1)
    i = pl.program_id(2)
    n_pairs = RET_HEADS // 2
    pw = 2 * RET_DK
    vw = 2 * RET_DV
    tdn = (((0,), (0,)), ((), ()))
    ndn = (((1,), (1,)), ((), ()))
    lgf = lgf_ref[...]
    lgb = lgb_ref[...]

    @pl.when((pl.program_id(0) == 0) & (ps == 0) & (i == 0))
    def _tables():
        a = lax.broadcasted_iota(jnp.int32, (C, C), 0)
        b = lax.broadcasted_iota(jnp.int32, (C, C), 1)
        dab = (a - b).astype(F32)
        for hd in range(RET_HEADS):
            fwd = jnp.where(a >= b, jnp.exp(jnp.where(a >= b, dab, 0.0) * lg_ref[0, hd]), 0.0)
            bwd = jnp.where(b >= a, jnp.exp(jnp.where(b >= a, -dab, 0.0) * lg_ref[1, hd]), 0.0)
            dm_ref[hd] = fwd + bwd
        row = lax.broadcasted_iota(jnp.int32, (C, 1), 0).astype(F32)
        qdf_ref[...] = jnp.exp((row + 1.0) * lgf)
        qdb_ref[...] = jnp.exp((C - row) * lgb)
        kdf_ref[...] = jnp.exp((C - 1.0 - row) * lgf)
        kdb_ref[...] = jnp.exp(row * lgb)

    @pl.when((ps == 0) & (i == 0))
    def _init_states():
        rowc = lax.broadcasted_iota(jnp.int32, (Lc, 1), 0).astype(F32)
        wf = jnp.exp((Lc - 1.0 - rowc) * lgf)
        wb = jnp.exp(rowc * lgb)
        kc = kc_ref[0].astype(F32)
        for pr in range(n_pairs):
            kp = kc[:, pr * pw:(pr + 1) * pw]
            vp = vc_ref[0, :, pr * vw:(pr + 1) * vw]
            f_ref[pr] = lax.dot_general((kp * wf[:, pr * pw:(pr + 1) * pw]).astype(BF16), vp, tdn,
                                        preferred_element_type=F32)
            r_ref[pr] = lax.dot_general((kp * wb[:, pr * pw:(pr + 1) * pw]).astype(BF16), vp, tdn,
                                        preferred_element_type=F32)

    @pl.when(ps == 0)
    def _backward_states():
        c = n_chunks - 1 - i
        k = rk_ref[0].astype(F32)
        cdb = jnp.exp(C * lgvb_ref[...])
        for pr in range(n_pairs):
            r_old = r_ref[pr]
            rs_ref[c, pr] = r_old.astype(BF16)
            kp = (k[:, pr * pw:(pr + 1) * pw] * kdb_ref[:, pr * pw:(pr + 1) * pw]).astype(BF16)
            vp = rv_ref[0, :, pr * vw:(pr + 1) * vw]
            upd = lax.dot_general(kp, vp, tdn, preferred_element_type=F32)
            r_ref[pr] = r_old * cdb[:, pr * vw:(pr + 1) * vw] + upd

    @pl.when(ps == 1)
    def _forward_outputs():
        q = rq_ref[0].astype(F32)
        k = rk_ref[0].astype(F32)
        cdf = jnp.exp(C * lgvf_ref[...])
        lane = lax.broadcasted_iota(jnp.int32, (1, pw), 1)
        for pr in range(n_pairs):
            sl = slice(pr * pw, (pr + 1) * pw)
            qp = q[:, sl]
            kpb = rk_ref[0, :, sl]
            qf = qp * qdf_ref[:, sl]
            qb = qp * qdb_ref[:, sl]
            fb = f_ref[pr].astype(BF16)
            rb = rs_ref[i, pr]
            for hh in range(2):
                hd = 2 * pr + hh
                hm = (lane // RET_DK) == hh
                vs = slice(hd * RET_DV, (hd + 1) * RET_DV)
                fs = slice(hh * RET_DV, (hh + 1) * RET_DV)
                a = lax.dot_general(jnp.where(hm, qp, 0.0).astype(BF16), kpb, ndn,
                                    preferred_element_type=F32)
                a = (a * dm_ref[hd]).astype(BF16)
                o = jnp.dot(a, rv_ref[0, :, vs], preferred_element_type=F32)
                o = o + jnp.dot(jnp.where(hm, qf, 0.0).astype(BF16), fb[:, fs],
                                preferred_element_type=F32)
                o = o + jnp.dot(jnp.where(hm, qb, 0.0).astype(BF16), rb[:, fs],
                                preferred_element_type=F32)
                o = o * lax.rsqrt(jnp.mean(o * o, axis=-1, keepdims=True) + EPS) * gout_ref[:, vs]
                g = rg_ref[0, :, vs]
                o_ref[0, :, vs] = (o * (g * _sigmoid(g))).astype(BF16)
            kp = (k[:, sl] * kdf_ref[:, sl]).astype(BF16)
            vp = rv_ref[0, :, pr * vw:(pr + 1) * vw]
            upd = lax.dot_general(kp, vp, tdn, preferred_element_type=F32)
            f_ref[pr] = f_ref[pr] * cdf[:, pr * vw:(pr + 1) * vw] + upd


def _retention(log_gamma, g_ret_out, rq, rk, rv, rg, rk_c, rv_c, C):
    B, L, n_qk = rq.shape
    n_v = rv.shape[2]
    Lc = rk_c.shape[1]
    n = L // C
    lgf = jnp.repeat(log_gamma[0], RET_DK)[None, :]
    lgb = jnp.repeat(log_gamma[1], RET_DK)[None, :]
    lgvf = jnp.repeat(log_gamma[0], RET_DV)[None, :]
    lgvb = jnp.repeat(log_gamma[1], RET_DV)[None, :]
    chunk = lambda p, i: jnp.where(p == 0, n - 1 - i, i)
    fwd_only = lambda p, i: jnp.where(p == 0, 0, i)
    const2 = lambda b, p, i: (0, 0)
    n_pairs = RET_HEADS // 2
    return pl.pallas_call(
        functools.partial(_ret_kernel, n_chunks=n),
        grid=(B, 2, n),
        in_specs=[
            pl.BlockSpec(memory_space=pltpu.SMEM),
            pl.BlockSpec((1, n_qk), const2),
            pl.BlockSpec((1, n_qk), const2),
            pl.BlockSpec((1, n_v), const2),
            pl.BlockSpec((1, n_v), const2),
            pl.BlockSpec((1, n_v), const2),
            pl.BlockSpec((1, C, n_qk), lambda b, p, i: (b, fwd_only(p, i), 0)),
            pl.BlockSpec((1, C, n_qk), lambda b, p, i: (b, chunk(p, i), 0)),
            pl.BlockSpec((1, C, n_v), lambda b, p, i: (b, chunk(p, i), 0)),
            pl.BlockSpec((1, C, n_v), lambda b, p, i: (b, fwd_only(p, i), 0)),
            pl.BlockSpec((1, Lc, n_qk), lambda b, p, i: (b, 0, 0)),
            pl.BlockSpec((1, Lc, n_v), lambda b, p, i: (b, 0, 0)),
        ],
        out_specs=pl.BlockSpec((1, C, n_v), lambda b, p, i: (b, fwd_only(p, i), 0)),
        out_shape=jax.ShapeDtypeStruct((B, L, n_v), BF16),
        scratch_shapes=[
            pltpu.VMEM((RET_HEADS, C, C), F32),
            pltpu.VMEM((C, n_qk), F32), pltpu.VMEM((C, n_qk), F32),
            pltpu.VMEM((C, n_qk), F32), pltpu.VMEM((C, n_qk), F32),
            pltpu.VMEM((n_pairs, 2 * RET_DK, 2 * RET_DV), F32),
            pltpu.VMEM((n_pairs, 2 * RET_DK, 2 * RET_DV), F32),
            pltpu.VMEM((n, n_pairs, 2 * RET_DK, 2 * RET_DV), BF16),
        ],
        compiler_params=_cparams(("arbitrary", "arbitrary", "arbitrary")),
        name="retention",
    )(log_gamma, lgf, lgb, lgvf, lgvb, g_ret_out, rq, rk, rv, rg, rk_c, rv_c)


def _out_router_kernel(mla_ref, ret_ref, x_ref, mod_ref, wo_ref, gffn_ref, wr_ref, br_ref,
                       x1_ref, hf_ref, idx_ref, rank_ref, gate_ref, cnt_ref):
    tm = x_ref.shape[0]
    n_mla = mla_ref.shape[1]
    y = jnp.dot(mla_ref[...], wo_ref[0:n_mla, :], preferred_element_type=F32)
    y = y + jnp.dot(ret_ref[...], wo_ref[n_mla:, :], preferred_element_type=F32)
    x1 = x_ref[...] + mod_ref[0, 2:3, :] * y
    x1_ref[...] = x1
    ms = jnp.mean(x1 * x1, axis=-1, keepdims=True)
    hf = x1 * lax.rsqrt(ms + EPS) * gffn_ref[...]
    hf = hf * (1.0 + mod_ref[0, 4:5, :]) + mod_ref[0, 3:4, :]
    for c in range(hf.shape[1] // LANES):
        hf_ref[pl.ds(c, tm, stride=SUBLANES), :] = hf[:, c * LANES:(c + 1) * LANES]

    logits = jnp.dot(hf, wr_ref[...], preferred_element_type=F32,
                     precision=lax.Precision.HIGHEST) + br_ref[...]
    lt = logits.T[0:N_EXPERTS, :]
    e_iota = lax.broadcasted_iota(jnp.int32, (N_EXPERTS, tm), 0).astype(F32)
    s_iota = lax.broadcasted_iota(jnp.int32, (tm, tm), 0)
    t_iota = lax.broadcasted_iota(jnp.int32, (tm, tm), 1)
    upper = (s_iota <= t_iota).astype(BF16)
    vals, idxs, ranks = [], [], []
    seen = jnp.zeros((N_EXPERTS, 1), F32)
    for _ in range(TOP_K):
        mx = jnp.max(lt, axis=0, keepdims=True)
        ix = jnp.min(jnp.where(lt == mx, e_iota, float(N_EXPERTS)), axis=0, keepdims=True)
        hit = e_iota == ix
        lt = jnp.where(hit, -jnp.inf, lt)
        onehot = hit.astype(F32)
        prefix = jnp.dot(hit.astype(BF16), upper, preferred_element_type=F32)
        rank = jnp.sum(onehot * (prefix - 1.0 + seen), axis=0, keepdims=True)
        seen = seen + jnp.sum(onehot, axis=1, keepdims=True)
        vals.append(mx)
        idxs.append(ix)
        ranks.append(rank.astype(jnp.int32))
    ex = [jnp.exp(v - vals[0]) for v in vals]
    den = ex[0] + ex[1] + ex[2] + ex[3]
    idx_ref[...] = jnp.concatenate(idxs, axis=0).astype(jnp.int32)
    rank_ref[...] = jnp.concatenate(ranks, axis=0)
    gate_ref[...] = jnp.concatenate([e / den for e in ex], axis=0)
    cnt_ref[0] = jnp.broadcast_to(seen, (N_EXPERTS, LANES)).astype(jnp.int32)


def _out_router(mla, ret, x2, mod3, w_out_b, g_ffn, w_r, b_r, L, tm):
    T, D = x2.shape
    n_tiles = T // tm
    per_b = L // tm
    const = lambda i: (0, 0)
    return pl.pallas_call(
        _out_router_kernel,
        grid=(n_tiles,),
        in_specs=[
            pl.BlockSpec((tm, mla.shape[1]), lambda i: (i, 0)),
            pl.BlockSpec((tm, ret.shape[1]), lambda i: (i, 0)),
            pl.BlockSpec((tm, D), lambda i: (i, 0)),
            pl.BlockSpec((1, N_MOD, D), lambda i: (i // per_b, 0, 0)),
            pl.BlockSpec(w_out_b.shape, const),
            pl.BlockSpec((1, D), const),
            pl.BlockSpec(w_r.shape, const),
            pl.BlockSpec((1, LANES), const),
        ],
        out_specs=[
            pl.BlockSpec((tm, D), lambda i: (i, 0)),
            pl.BlockSpec((tm * SUBLANES, LANES), lambda i: (i, 0)),
            pl.BlockSpec((TOP_K, tm), lambda i: (0, i)),
            pl.BlockSpec((TOP_K, tm), lambda i: (0, i)),
            pl.BlockSpec((TOP_K, tm), lambda i: (0, i)),
            pl.BlockSpec((1, N_EXPERTS, LANES), lambda i: (i, 0, 0)),
        ],
        out_shape=[
            jax.ShapeDtypeStruct((T, D), F32),
            jax.ShapeDtypeStruct((T * SUBLANES, LANES), F32),
            jax.ShapeDtypeStruct((TOP_K, T), jnp.int32),
            jax.ShapeDtypeStruct((TOP_K, T), jnp.int32),
            jax.ShapeDtypeStruct((TOP_K, T), F32),
            jax.ShapeDtypeStruct((n_tiles, N_EXPERTS, LANES), jnp.int32),
        ],
        compiler_params=_cparams(("arbitrary",)),
        name="out_router",
    )(mla, ret, x2, mod3, w_out_b, g_ffn, w_r, b_r)


def _row_copy(src, dst, sem):
    return pltpu.make_async_copy(src, dst, sem)


def _dispatch_kernel(dest_ref, hf_ref, xs_ref, sem, *, td):
    def issue(t, carry):
        src = hf_ref.at[pl.ds(pl.multiple_of(t * SUBLANES, SUBLANES), SUBLANES), :]
        for k in range(TOP_K):
            d = dest_ref[t * TOP_K + k]
            dst = xs_ref.at[pl.ds(pl.multiple_of(d * SUBLANES, SUBLANES), SUBLANES), :]
            _row_copy(src, dst, sem).start()
        return carry

    lax.fori_loop(0, td, issue, 0)
    for k in range(TOP_K):
        _row_copy(hf_ref, xs_ref.at[pl.ds(0, td * SUBLANES), :], sem).wait()


def _dispatch(dest_flat, hf8, n_pad, td):
    T8, _ = hf8.shape
    T = T8 // SUBLANES
    return pl.pallas_call(
        functools.partial(_dispatch_kernel, td=td),
        grid=(T // td,),
        in_specs=[
            pl.BlockSpec((td * TOP_K,), lambda i: (i,), memory_space=pltpu.SMEM),
            pl.BlockSpec((td * SUBLANES, LANES), lambda i: (i, 0)),
        ],
        out_specs=pl.BlockSpec(memory_space=pl.ANY),
        out_shape=jax.ShapeDtypeStruct((n_pad * SUBLANES, LANES), F32),
        scratch_shapes=[pltpu.SemaphoreType.DMA],
        compiler_params=_cparams(("arbitrary",)),
        name="dispatch",
    )(dest_flat, hf8)


def _experts_kernel(ie_ref, ib_ref, lo_ref, hi_ref, first_ref, ni_ref, xs_ref, w1g_ref, w1l_ref,
                    w2_ref, b1g_ref, b1l_ref, b2_ref, ys_ref):
    i = pl.program_id(0)
    blk = xs_ref.shape[0] // SUBLANES
    d = w1g_ref.shape[1]

    @pl.when(i < ni_ref[0])
    def _():
        cols = [xs_ref[pl.ds(c, blk, stride=SUBLANES), :] for c in range(d // LANES)]
        x = jnp.concatenate(cols, axis=-1)
        row = lax.broadcasted_iota(jnp.int32, (blk, 1), 0)
        mine = (row >= lo_ref[i]) & (row < hi_ref[i])
        x = jnp.where(mine, x, 0.0).astype(BF16)
        hg = jnp.dot(x, w1g_ref[0], preferred_element_type=F32) + b1g_ref[0]
        hl = jnp.dot(x, w1l_ref[0], preferred_element_type=F32) + b1l_ref[0]
        glu = jnp.minimum(hg, SWIGLU_LIMIT)
        lin = jnp.clip(hl, -SWIGLU_LIMIT, SWIGLU_LIMIT)
        act = glu * _sigmoid(SWIGLU_ALPHA * glu) * (lin + 1.0)
        y = jnp.dot(act.astype(BF16), w2_ref[0], preferred_element_type=F32) + b2_ref[0]
        y = jnp.where(mine, y, 0.0)

        @pl.when(first_ref[i] == 1)
        def _():
            for c in range(d // LANES):
                ys_ref[pl.ds(c, blk, stride=SUBLANES), :] = y[:, c * LANES:(c + 1) * LANES]

        @pl.when(first_ref[i] == 0)
        def _():
            for c in range(d // LANES):
                ys_ref[pl.ds(c, blk, stride=SUBLANES), :] += y[:, c * LANES:(c + 1) * LANES]


def _experts(items, xs, w1g, w1l, w2, b1g, b1l, b2, blk):
    item_e, item_blk, item_lo, item_hi, item_first, n_items = items
    d = w1g.shape[1]
    f = w1g.shape[2]
    row_map = lambda i, ie, ib, lo, hi, fi, ni: (ib[i], 0)
    exp_map = lambda i, ie, ib, lo, hi, fi, ni: (ie[i], 0, 0)
    grid_spec = pltpu.PrefetchScalarGridSpec(
        num_scalar_prefetch=6,
        grid=(item_e.shape[0],),
        in_specs=[
            pl.BlockSpec((blk * SUBLANES, LANES), row_map),
            pl.BlockSpec((1, d, f), exp_map),
            pl.BlockSpec((1, d, f), exp_map),
            pl.BlockSpec((1, f, d), exp_map),
            pl.BlockSpec((1, 1, f), exp_map),
            pl.BlockSpec((1, 1, f), exp_map),
            pl.BlockSpec((1, 1, d), exp_map),
        ],
        out_specs=pl.BlockSpec((blk * SUBLANES, LANES), row_map),
    )
    return pl.pallas_call(
        _experts_kernel,
        grid_spec=grid_spec,
        out_shape=jax.ShapeDtypeStruct(xs.shape, F32),
        compiler_params=_cparams(("arbitrary",)),
        name="experts",
    )(item_e, item_blk, item_lo, item_hi, item_first, n_items, xs, w1g, w1l, w2, b1g, b1l, b2)


def _combine_kernel(dest_ref, x1_ref, gate_ref, mod_ref, ys_ref, o_ref, buf_ref, sem, *, tc):
    def issue(t, carry):
        for k in range(TOP_K):
            d = dest_ref[t * TOP_K + k]
            src = ys_ref.at[pl.ds(pl.multiple_of(d * SUBLANES, SUBLANES), SUBLANES), :]
            dst = buf_ref.at[k, pl.ds(pl.multiple_of(t * SUBLANES, SUBLANES), SUBLANES), :]
            _row_copy(src, dst, sem).start()
        return carry

    lax.fori_loop(0, tc, issue, 0)
    for k in range(TOP_K):
        _row_copy(ys_ref.at[pl.ds(0, tc * SUBLANES), :], buf_ref.at[k], sem).wait()

    g = gate_ref[...]
    for c in range(o_ref.shape[1] // LANES):
        cs = slice(c * LANES, (c + 1) * LANES)
        acc = g[:, 0:1] * buf_ref[0, pl.ds(c, tc, stride=SUBLANES), :]
        for k in range(1, TOP_K):
            acc = acc + g[:, k:k + 1] * buf_ref[k, pl.ds(c, tc, stride=SUBLANES), :]
        o_ref[:, cs] = x1_ref[:, cs] + mod_ref[0, 5:6, cs] * acc


def _combine(dest_flat, x1, gates_t, mod3, ys, L, tc):
    T, D = x1.shape
    per_b = L // tc
    return pl.pallas_call(
        functools.partial(_combine_kernel, tc=tc),
        grid=(T // tc,),
        in_specs=[
            pl.BlockSpec((tc * TOP_K,), lambda i: (i,), memory_space=pltpu.SMEM),
            pl.BlockSpec((tc, D), lambda i: (i, 0)),
            pl.BlockSpec((tc, TOP_K), lambda i: (i, 0)),
            pl.BlockSpec((1, N_MOD, D), lambda i: (i // per_b, 0, 0)),
            pl.BlockSpec(memory_space=pl.ANY),
        ],
        out_specs=pl.BlockSpec((tc, D), lambda i: (i, 0)),
        out_shape=jax.ShapeDtypeStruct((T, D), F32),
        scratch_shapes=[pltpu.VMEM((TOP_K, tc * SUBLANES, LANES), F32), pltpu.SemaphoreType.DMA],
        compiler_params=_cparams(("arbitrary",)),
        name="combine",
    )(dest_flat, x1, gates_t, mod3, ys)


def _rope_tables(L, dim, lane_off, width):
    rows = L // GRID_W
    nf = dim // 4
    inv = jnp.power(ROPE_BASE, -jnp.arange(nf, dtype=F32) / nf)
    row = jnp.repeat(jnp.arange(rows, dtype=F32), GRID_W)
    col = jnp.tile(jnp.arange(GRID_W, dtype=F32), rows)
    pos = jnp.stack([row, col], axis=-1)
    ang = pos[:, :, None] * inv
    ang = jnp.broadcast_to(ang[:, :, None, :], (L, 2, 2, nf)).reshape(L, dim)
    sign = jnp.where((jnp.arange(dim) % (dim // 2)) < nf, -1.0, 1.0).astype(F32)
    cos, sin = jnp.cos(ang), jnp.sin(ang) * sign
    if lane_off is None:
        reps = width // dim
        return jnp.tile(cos, (1, reps)), jnp.tile(sin, (1, reps))
    cfull = jnp.ones((L, width), F32).at[:, lane_off:lane_off + dim].set(cos)
    sfull = jnp.zeros((L, width), F32).at[:, lane_off:lane_off + dim].set(sin)
    return cfull, sfull


def _identity_tables(L):
    return jnp.ones((L, LANES), F32), jnp.zeros((L, LANES), F32)


def _prep_weights(w_in, w_q_up, w_kv_up, g_q_head, g_k_head):
    D = w_in.shape[0]
    o = 0
    wq = w_in[:, o:o + Q_LORA]; o += Q_LORA
    wkv = w_in[:, o:o + KV_LORA]; o += KV_LORA
    wpe = w_in[:, o:o + MLA_ROPE]; o += MLA_ROPE
    rest = w_in[:, o:]
    pe_blk = jnp.zeros((D, LANES), w_in.dtype).at[:, MLA_NOPE:MLA_NOPE + MLA_ROPE].set(wpe)
    w_in_r = jnp.concatenate([wq, wkv, pe_blk, rest], axis=1).astype(BF16)
    pad_h = LANES - MLA_QK
    w_q_r = jnp.pad(w_q_up.reshape(Q_LORA, MLA_HEADS, MLA_QK), ((0, 0), (0, 0), (0, pad_h)))
    w_q_r = w_q_r.reshape(Q_LORA, MLA_HEADS * LANES).astype(BF16)
    kv = w_kv_up.reshape(KV_LORA, MLA_HEADS, MLA_NOPE + MLA_V)
    kpart = jnp.pad(kv[:, :, :MLA_NOPE], ((0, 0), (0, 0), (0, LANES - MLA_NOPE)))
    vpart = jnp.pad(kv[:, :, MLA_NOPE:], ((0, 0), (0, 0), (0, LANES - MLA_V)))
    w_kv_r = jnp.concatenate([kpart.reshape(KV_LORA, -1), vpart.reshape(KV_LORA, -1)], axis=1).astype(BF16)
    gqh = jnp.pad(g_q_head, (0, pad_h))[None, :]
    gkh = jnp.pad(g_k_head, (0, pad_h))[None, :]
    return w_in_r, w_q_r, w_kv_r, gqh, gkh


def _routing_tables(idx, rank, counts, tm, blk):
    T = idx.shape[1]
    i32 = jnp.int32
    tot = jnp.sum(counts, axis=0)
    end = jnp.cumsum(tot)
    start = end - tot
    tile_base = start[None, :] + jnp.cumsum(counts, axis=0) - counts
    tile_of_tok = (jnp.arange(T, dtype=i32) // tm)[None, :]
    dest = jnp.take(tile_base.reshape(-1), tile_of_tok * N_EXPERTS + idx) + rank

    n_work = (T * TOP_K) // blk + N_EXPERTS
    first_blk = start // blk
    last_blk = (end - 1) // blk
    per_e = jnp.where(tot > 0, last_blk - first_blk + 1, 0)
    item_end = jnp.cumsum(per_e)
    item_start = item_end - per_e
    n_items = item_end[-1]
    j = jnp.minimum(jnp.arange(n_work, dtype=i32), n_items - 1)
    item_e = jnp.minimum(jnp.searchsorted(item_end, j, side='right'), N_EXPERTS - 1).astype(i32)
    item_blk = first_blk[item_e] + j - item_start[item_e]
    item_lo = jnp.clip(start[item_e] - item_blk * blk, 0, blk)
    item_hi = jnp.clip(end[item_e] - item_blk * blk, 0, blk)
    prev_blk = jnp.concatenate([jnp.full((1,), -1, i32), item_blk[:-1].astype(i32)])
    item_first = (item_blk != prev_blk).astype(i32)
    items = (item_e, item_blk.astype(i32), item_lo.astype(i32), item_hi.astype(i32), item_first,
             n_items.astype(i32).reshape(1))
    return dest.astype(i32), items


def kernel(x, c, ctx, c_ctx, g_attn, g_ffn, w_ada, b_ada, w_in, g_q_lora, w_q_up, g_q_head,
           g_kv_lora, w_kv_up, g_k_head, ret_decay_logit, g_ret_out, w_out, w_router, b_router,
           w_mlp1, b_mlp1, w_mlp2, b_mlp2):
    B, L, D = x.shape
    Lc = ctx.shape[1]
    T = B * L
    l = 0
    assert w_ada.shape[0] == 1

    rows = ((B + 1 + SUBLANES - 1) // SUBLANES) * SUBLANES
    cc = jnp.zeros((rows, D), F32).at[:B].set(c).at[B].set(c_ctx)
    mod3 = _adaln(cc, w_ada[l], b_ada[l][None, :]).reshape(rows, N_MOD, D)

    w_in_r, w_q_r, w_kv_r, gqh, gkh = _prep_weights(w_in[l], w_q_up[l], w_kv_up[l],
                                                     g_q_head[l], g_k_head[l])
    tabs_x = _rope_tables(L, MLA_ROPE, MLA_NOPE, LANES) + _rope_tables(L, RET_DK, None, LANES)
    tabs_c = _identity_tables(Lc) + _identity_tables(Lc)
    proj = functools.partial(_in_proj, g_attn=g_attn[l][None, :], w_in_r=w_in_r,
                             g_q_lora=g_q_lora[l][None, :], w_q_r=w_q_r, gqh=gqh,
                             g_kv_lora=g_kv_lora[l][None, :], w_kv_r=w_kv_r, gkh=gkh)
    q, kx, vx, rq, rk, rv, rg = proj(x, mod3, lambda b: b, tabs=tabs_x, tm=min(PROJ_TM, L))
    _, kc, vc, _, rk_c, rv_c, _ = proj(ctx, mod3, lambda b: B, tabs=tabs_c, tm=Lc)

    mla = _attention(q, kx, vx, kc, vc, min(ATT_TQ, L), min(ATT_TK, L))

    log_gamma = jax.nn.log_sigmoid(ret_decay_logit[l].astype(F32))
    ret = _retention(log_gamma, g_ret_out[l][None, :], rq, rk, rv, rg, rk_c, rv_c, min(RET_C, L))

    w_r = jnp.pad(w_router[l], ((0, 0), (0, LANES - N_EXPERTS)))
    b_r = jnp.pad(b_router[l], (0, LANES - N_EXPERTS))[None, :]
    tm = min(OUT_TM, L)
    x1, hf8, idx, rank, gates, cnt = _out_router(
        mla.reshape(T, -1), ret.reshape(T, -1), x.reshape(T, D), mod3,
        w_out[l].astype(BF16), g_ffn[l][None, :], w_r, b_r, L, tm)

    blk = MOE_BLK
    assert (T * TOP_K) % blk == 0
    dest, items = _routing_tables(idx, rank, cnt[:, :, 0], tm, blk)
    dest_flat = dest.T.reshape(-1)

    xs = _dispatch(dest_flat, hf8, T * TOP_K, min(DISP_T, L))

    w1 = w_mlp1[l]
    w1g = w1[:, :, 0::2].astype(BF16)
    w1l = w1[:, :, 1::2].astype(BF16)
    b1g = b_mlp1[l][:, None, 0::2]
    b1l = b_mlp1[l][:, None, 1::2]
    ys = _experts(items, xs, w1g, w1l, w_mlp2[l].astype(BF16),
                  b1g, b1l, b_mlp2[l][:, None, :], blk)

    out = _combine(dest_flat, x1, gates.T, mod3, ys, L, min(COMB_T, L))
    return out.reshape(B, L, D)
```

```python
import functools
import math

import jax
import jax.numpy as jnp
from jax import lax
from jax.experimental import pallas as pl
from jax.experimental.pallas import tpu as pltpu

F32 = jnp.float32
BF16 = jnp.bfloat16

LANES = 128
SUBLANES = 8
VMEM_LIMIT_BYTES = 56 * 1024 * 1024

EPS = 1e-6
ROPE_BASE = 10000.0
GRID_W = 64
N_MOD = 6
MLA_HEADS = 8
MLA_NOPE = 64
MLA_ROPE = 32
MLA_QK = MLA_NOPE + MLA_ROPE
MLA_V = 64
Q_LORA = 256
KV_LORA = 128
RET_HEADS = 4
RET_DK = 64
RET_DV = 128
N_EXPERTS = 32
TOP_K = 4
SWIGLU_LIMIT = 7.0
SWIGLU_ALPHA = 1.702
LOG2E = 1.4426950408889634

PROJ_TM = 512
ATT_TQ = 512
ATT_TK = 512
RET_C = 256
OUT_TM = 512
MOE_BLK = 512
DISP_T = 256
COMB_T = 256


def _cparams(sem):
    return pltpu.CompilerParams(dimension_semantics=sem, vmem_limit_bytes=VMEM_LIMIT_BYTES)


def _sigmoid(x):
    return 1.0 / (1.0 + jnp.exp(-x))


def _adaln_kernel(c_ref, w_ref, b_ref, o_ref):
    c = c_ref[...]
    s = (c * _sigmoid(c)).astype(BF16)
    o_ref[...] = jnp.dot(s, w_ref[...].astype(BF16), preferred_element_type=F32) + b_ref[...]


def _adaln(cc, w_ada, b_ada):
    rows, d = cc.shape
    n = w_ada.shape[1]
    tn = 1536
    return pl.pallas_call(
        _adaln_kernel,
        grid=(n // tn,),
        in_specs=[pl.BlockSpec((rows, d), lambda j: (0, 0)),
                  pl.BlockSpec((d, tn), lambda j: (0, j)),
                  pl.BlockSpec((1, tn), lambda j: (0, j))],
        out_specs=pl.BlockSpec((rows, tn), lambda j: (0, j)),
        out_shape=jax.ShapeDtypeStruct((rows, n), F32),
        compiler_params=_cparams(("arbitrary",)),
        name="adaln",
    )(cc, w_ada, b_ada)


def _rope(x, cos, sin, lo_mask, back, fwd):
    rot = jnp.where(lo_mask, pltpu.roll(x, back, 1), pltpu.roll(x, fwd, 1))
    return x * cos + rot * sin


def _in_proj_kernel(x_ref, mod_ref, gattn_ref, win_ref, gql_ref, wq_ref, gqh_ref,
                    gkvl_ref, wkv_ref, gkh_ref, cm_ref, sm_ref, cr_ref, sr_ref,
                    q_ref, k_ref, v_ref, rq_ref, rk_ref, rv_ref, rg_ref, *, q_scale):
    x = x_ref[0]
    shift = mod_ref[0, 0:1, :]
    scale = mod_ref[0, 1:2, :]
    ms = jnp.mean(x * x, axis=-1, keepdims=True)
    h = x * lax.rsqrt(ms + EPS) * gattn_ref[...]
    h = h * (1.0 + scale) + shift
    p = jnp.dot(h.astype(BF16), win_ref[...], preferred_element_type=F32)

    lane = lax.broadcasted_iota(jnp.int32, (1, LANES), 1)
    mla_lo = (lane % (MLA_ROPE // 2)) < (MLA_ROPE // 4)
    ret_lo = (lane % (RET_DK // 2)) < (RET_DK // 4)
    cm, sm = cm_ref[...], sm_ref[...]
    cr, sr = cr_ref[...], sr_ref[...]
    mla_back, mla_fwd = LANES - MLA_ROPE // 4, MLA_ROPE // 4
    ret_back, ret_fwd = LANES - RET_DK // 4, RET_DK // 4

    cq = p[:, 0:Q_LORA]
    cq = cq * lax.rsqrt(jnp.mean(cq * cq, axis=-1, keepdims=True) + EPS) * gql_ref[...]
    qf = jnp.dot(cq.astype(BF16), wq_ref[...], preferred_element_type=F32)
    for hd in range(MLA_HEADS):
        blk = qf[:, hd * LANES:(hd + 1) * LANES]
        r = lax.rsqrt(jnp.sum(blk * blk, axis=-1, keepdims=True) * (1.0 / MLA_QK) + EPS)
        blk = _rope(blk * r * gqh_ref[...], cm, sm, mla_lo, mla_back, mla_fwd)
        q_ref[0, hd] = (blk * q_scale).astype(BF16)

    o_kv = Q_LORA
    ckv = p[:, o_kv:o_kv + KV_LORA]
    ckv = ckv * lax.rsqrt(jnp.mean(ckv * ckv, axis=-1, keepdims=True) + EPS) * gkvl_ref[...]
    kvf = jnp.dot(ckv.astype(BF16), wkv_ref[...], preferred_element_type=F32)
    o_pe = o_kv + KV_LORA
    pe = p[:, o_pe:o_pe + LANES]
    ones_col = (lane == MLA_V).astype(F32)
    for hd in range(MLA_HEADS):
        kk = kvf[:, hd * LANES:(hd + 1) * LANES] + pe
        r = lax.rsqrt(jnp.sum(kk * kk, axis=-1, keepdims=True) * (1.0 / MLA_QK) + EPS)
        kk = _rope(kk * r * gkh_ref[...], cm, sm, mla_lo, mla_back, mla_fwd)
        k_ref[0, hd] = kk.T.astype(BF16)
        vv = kvf[:, (MLA_HEADS + hd) * LANES:(MLA_HEADS + hd + 1) * LANES] + ones_col
        v_ref[0, hd] = vv.astype(BF16)

    o_rq = o_pe + LANES
    n_qk = RET_HEADS * RET_DK
    for j in range(n_qk // LANES):
        blk = p[:, o_rq + j * LANES:o_rq + (j + 1) * LANES]
        rq_ref[0, :, j * LANES:(j + 1) * LANES] = _rope(
            blk, cr, sr, ret_lo, ret_back, ret_fwd).astype(BF16)
    o_rk = o_rq + n_qk
    for j in range(n_qk // LANES):
        blk = p[:, o_rk + j * LANES:o_rk + (j + 1) * LANES] * (RET_DK ** -0.5)
        rk_ref[0, :, j * LANES:(j + 1) * LANES] = _rope(
            blk, cr, sr, ret_lo, ret_back, ret_fwd).astype(BF16)
    o_rv = o_rk + n_qk
    n_v = RET_HEADS * RET_DV
    rv_ref[0] = p[:, o_rv:o_rv + n_v].astype(BF16)
    rg_ref[0] = p[:, o_rv + n_v:o_rv + 2 * n_v]


def _in_proj(x, mod3, mod_row_of_batch, g_attn, w_in_r, g_q_lora, w_q_r, gqh, g_kv_lora,
             w_kv_r, gkh, tabs, tm):
    B, L, D = x.shape
    cm, sm, cr, sr = tabs
    n_in = w_in_r.shape[1]
    const = lambda b, i: (0, 0)
    tab_spec = pl.BlockSpec((tm, LANES), lambda b, i: (i, 0))
    head_spec = pl.BlockSpec((1, MLA_HEADS, tm, LANES), lambda b, i: (b, 0, i, 0))
    n_qk = RET_HEADS * RET_DK
    n_v = RET_HEADS * RET_DV
    seq_spec = lambda w: pl.BlockSpec((1, tm, w), lambda b, i: (b, i, 0))
    head_shape = jax.ShapeDtypeStruct((B, MLA_HEADS, L, LANES), BF16)
    q_scale = MLA_QK ** -0.5 * LOG2E
    return pl.pallas_call(
        functools.partial(_in_proj_kernel, q_scale=q_scale),
        grid=(B, L // tm),
        in_specs=[
            pl.BlockSpec((1, tm, D), lambda b, i: (b, i, 0)),
            pl.BlockSpec((1, N_MOD, D), lambda b, i: (mod_row_of_batch(b), 0, 0)),
            pl.BlockSpec((1, D), const),
            pl.BlockSpec((D, n_in), const),
            pl.BlockSpec((1, Q_LORA), const),
            pl.BlockSpec(w_q_r.shape, const),
            pl.BlockSpec((1, LANES), const),
            pl.BlockSpec((1, KV_LORA), const),
            pl.BlockSpec(w_kv_r.shape, const),
            pl.BlockSpec((1, LANES), const),
            tab_spec, tab_spec, tab_spec, tab_spec,
        ],
        out_specs=[head_spec,
                   pl.BlockSpec((1, MLA_HEADS, LANES, tm), lambda b, i: (b, 0, 0, i)),
                   head_spec,
                   seq_spec(n_qk), seq_spec(n_qk), seq_spec(n_v), seq_spec(n_v)],
        out_shape=[head_shape,
                   jax.ShapeDtypeStruct((B, MLA_HEADS, LANES, L), BF16),
                   head_shape,
                   jax.ShapeDtypeStruct((B, L, n_qk), BF16),
                   jax.ShapeDtypeStruct((B, L, n_qk), BF16),
                   jax.ShapeDtypeStruct((B, L, n_v), BF16),
                   jax.ShapeDtypeStruct((B, L, n_v), F32)],
        compiler_params=_cparams(("arbitrary", "arbitrary")),
        name="in_proj",
    )(x, mod3, g_attn, w_in_r, g_q_lora, w_q_r, gqh, g_kv_lora, w_kv_r, gkh, cm, sm, cr, sr)


def _attn_kernel(q_ref, kx_ref, vx_ref, kc_ref, vc_ref, o_ref, m_ref, acc_ref, s_ref, *, tk):
    n_kv = kx_ref.shape[3] // tk
    n_heads = q_ref.shape[1]

    def scores(hh, start, slot):
        kb = kx_ref[0, hh, :, pl.ds(start, tk)]
        s_ref[hh, slot] = jnp.dot(q_ref[0, hh], kb, preferred_element_type=F32)

    def consume(hh, start, slot):
        s = s_ref[hh, slot]
        m_old = m_ref[hh]
        m_new = jnp.maximum(m_old, jnp.max(s, axis=-1, keepdims=True))
        alpha = jnp.exp2(m_old - m_new)
        pr = jnp.exp2(s - m_new)
        vb = vx_ref[0, hh, pl.ds(start, tk), :]
        acc_ref[hh] = alpha * acc_ref[hh] + jnp.dot(pr.astype(BF16), vb,
                                                    preferred_element_type=F32)
        m_ref[hh] = m_new

    def step(j, slot, prefetch):
        if prefetch:
            nxt = pl.multiple_of((j + 1) * tk, tk)
            for hh in range(n_heads):
                scores(hh, nxt, 1 - slot)
        cur = pl.multiple_of(j * tk, tk)
        for hh in range(n_heads):
            consume(hh, cur, slot)

    for hh in range(n_heads):
        scores(hh, 0, 0)
        s = jnp.dot(q_ref[0, hh], kc_ref[0, hh], preferred_element_type=F32)
        m0 = jnp.max(s, axis=-1, keepdims=True)
        m_ref[hh] = m0
        acc_ref[hh] = jnp.dot(jnp.exp2(s - m0).astype(BF16), vc_ref[0, hh],
                              preferred_element_type=F32)

    def body(jj, carry):
        step(2 * jj, 0, True)
        step(2 * jj + 1, 1, True)
        return carry

    lax.fori_loop(0, n_kv // 2 - 1, body, 0)
    step(n_kv - 2, 0, True)
    step(n_kv - 1, 1, False)
    outs = []
    for hh in range(n_heads):
        acc = acc_ref[hh]
        outs.append(acc[:, :MLA_V] / acc[:, MLA_V:MLA_V + 1])
    o_ref[0] = jnp.concatenate(outs, axis=-1).astype(BF16)


def _attention(q, kx, vx, kc, vc, tq, tk):
    B, H, L, _ = q.shape
    Lc = kc.shape[3]
    return pl.pallas_call(
        functools.partial(_attn_kernel, tk=tk),
        grid=(B, H // 2, L // tq),
        in_specs=[
            pl.BlockSpec((1, 2, tq, LANES), lambda b, h, i: (b, h, i, 0)),
            pl.BlockSpec((1, 2, LANES, L), lambda b, h, i: (b, h, 0, 0)),
            pl.BlockSpec((1, 2, L, LANES), lambda b, h, i: (b, h, 0, 0)),
            pl.BlockSpec((1, 2, LANES, Lc), lambda b, h, i: (b, h, 0, 0)),
            pl.BlockSpec((1, 2, Lc, LANES), lambda b, h, i: (b, h, 0, 0)),
        ],
        out_specs=pl.BlockSpec((1, tq, LANES), lambda b, h, i: (b, i, h)),
        out_shape=jax.ShapeDtypeStruct((B, L, H * MLA_V), BF16),
        scratch_shapes=[pltpu.VMEM((2, tq, 1), F32), pltpu.VMEM((2, tq, LANES), F32),
                        pltpu.VMEM((2, 2, tq, tk), F32)],
        compiler_params=_cparams(("arbitrary", "arbitrary", "arbitrary")),
        name="attention",
    )(q, kx, vx, kc, vc)


def _ret_kernel(lg_ref, lgf_ref, lgb_ref, lgvf_ref, lgvb_ref, gout_ref, rq_ref, rk_ref, rv_ref, rg_ref,
                kc_ref, vc_ref, o_ref, dm_ref, qdf_ref, qdb_ref, kdf_ref, kdb_ref,
                f_ref, r_ref, rs_ref, *, n_chunks):
    C = rq_ref.shape[1]
    Lc = kc_ref.shape[1]
    ps = pl.program_id(1)
    i = pl.program_id(2)
    n_pairs = RET_HEADS // 2
    pw = 2 * RET_DK
    vw = 2 * RET_DV
    tdn = (((0,), (0,)), ((), ()))
    ndn = (((1,), (1,)), ((), ()))
    lgf = lgf_ref[...]
    lgb = lgb_ref[...]

    @pl.when((pl.program_id(0) == 0) & (ps == 0) & (i == 0))
    def _tables():
        a = lax.broadcasted_iota(jnp.int32, (C, C), 0)
        b = lax.broadcasted_iota(jnp.int32, (C, C), 1)
        dab = (a - b).astype(F32)
        for hd in range(RET_HEADS):
            fwd = jnp.where(a >= b, jnp.exp(jnp.where(a >= b, dab, 0.0) * lg_ref[0, hd]), 0.0)
            bwd = jnp.where(b >= a, jnp.exp(jnp.where(b >= a, -dab, 0.0) * lg_ref[1, hd]), 0.0)
            dm_ref[hd] = fwd + bwd
        row = lax.broadcasted_iota(jnp.int32, (C, 1), 0).astype(F32)
        qdf_ref[...] = jnp.exp((row + 1.0) * lgf)
        qdb_ref[...] = jnp.exp((C - row) * lgb)
        kdf_ref[...] = jnp.exp((C - 1.0 - row) * lgf)
        kdb_ref[...] = jnp.exp(row * lgb)

    @pl.when((ps == 0) & (i == 0))
    def _init_states():
        rowc = lax.broadcasted_iota(jnp.int32, (Lc, 1), 0).astype(F32)
        wf = jnp.exp((Lc - 1.0 - rowc) * lgf)
        wb = jnp.exp(rowc * lgb)
        kc = kc_ref[0].astype(F32)
        for pr in range(n_pairs):
            kp = kc[:, pr * pw:(pr + 1) * pw]
            vp = vc_ref[0, :, pr * vw:(pr + 1) * vw]
            f_ref[pr] = lax.dot_general((kp * wf[:, pr * pw:(pr + 1) * pw]).astype(BF16), vp, tdn,
                                        preferred_element_type=F32)
            r_ref[pr] = lax.dot_general((kp * wb[:, pr * pw:(pr + 1) * pw]).astype(BF16), vp, tdn,
                                        preferred_element_type=F32)

    @pl.when(ps == 0)
    def _backward_states():
        c = n_chunks - 1 - i
        k = rk_ref[0].astype(F32)
        cdb = jnp.exp(C * lgvb_ref[...])
        for pr in range(n_pairs):
            r_old = r_ref[pr]
            rs_ref[c, pr] = r_old.astype(BF16)
            kp = (k[:, pr * pw:(pr + 1) * pw] * kdb_ref[:, pr * pw:(pr + 1) * pw]).astype(BF16)
            vp = rv_ref[0, :, pr * vw:(pr + 1) * vw]
            upd = lax.dot_general(kp, vp, tdn, preferred_element_type=F32)
            r_ref[pr] = r_old * cdb[:, pr * vw:(pr + 1) * vw] + upd

    @pl.when(ps == 1)
    def _forward_outputs():
        q = rq_ref[0].astype(F32)
        k = rk_ref[0].astype(F32)
        cdf = jnp.exp(C * lgvf_ref[...])
        lane = lax.broadcasted_iota(jnp.int32, (1, pw), 1)
        for pr in range(n_pairs):
            sl = slice(pr * pw, (pr + 1) * pw)
            qp = q[:, sl]
            kpb = rk_ref[0, :, sl]
            qf = qp * qdf_ref[:, sl]
            qb = qp * qdb_ref[:, sl]
            fb = f_ref[pr].astype(BF16)
            rb = rs_ref[i, pr]
            for hh in range(2):
                hd = 2 * pr + hh
                hm = (lane // RET_DK) == hh
                vs = slice(hd * RET_DV, (hd + 1) * RET_DV)
                fs = slice(hh * RET_DV, (hh + 1) * RET_DV)
                a = lax.dot_general(jnp.where(hm, qp, 0.0).astype(BF16), kpb, ndn,
                                    preferred_element_type=F32)
                a = (a * dm_ref[hd]).astype(BF16)
                o = jnp.dot(a, rv_ref[0, :, vs], preferred_element_type=F32)
                o = o + jnp.dot(jnp.where(hm, qf, 0.0).astype(BF16), fb[:, fs],
                                preferred_element_type=F32)
                o = o + jnp.dot(jnp.where(hm, qb, 0.0).astype(BF16), rb[:, fs],
                                preferred_element_type=F32)
                o = o * lax.rsqrt(jnp.mean(o * o, axis=-1, keepdims=True) + EPS) * gout_ref[:, vs]
                g = rg_ref[0, :, vs]
                o_ref[0, :, vs] = (o * (g * _sigmoid(g))).astype(BF16)
            kp = (k[:, sl] * kdf_ref[:, sl]).astype(BF16)
            vp = rv_ref[0, :, pr * vw:(pr + 1) * vw]
            upd = lax.dot_general(kp, vp, tdn, preferred_element_type=F32)
            f_ref[pr] = f_ref[pr] * cdf[:, pr * vw:(pr + 1) * vw] + upd


def _retention(log_gamma, g_ret_out, rq, rk, rv, rg, rk_c, rv_c, C):
    B, L, n_qk = rq.shape
    n_v = rv.shape[2]
    Lc = rk_c.shape[1]
    n = L // C
    lgf = jnp.repeat(log_gamma[0], RET_DK)[None, :]
    lgb = jnp.repeat(log_gamma[1], RET_DK)[None, :]
    lgvf = jnp.repeat(log_gamma[0], RET_DV)[None, :]
    lgvb = jnp.repeat(log_gamma[1], RET_DV)[None, :]
    chunk = lambda p, i: jnp.where(p == 0, n - 1 - i, i)
    fwd_only = lambda p, i: jnp.where(p == 0, 0, i)
    const2 = lambda b, p, i: (0, 0)
    n_pairs = RET_HEADS // 2
    return pl.pallas_call(
        functools.partial(_ret_kernel, n_chunks=n),
        grid=(B, 2, n),
        in_specs=[
            pl.BlockSpec(memory_space=pltpu.SMEM),
            pl.BlockSpec((1, n_qk), const2),
            pl.BlockSpec((1, n_qk), const2),
            pl.BlockSpec((1, n_v), const2),
            pl.BlockSpec((1, n_v), const2),
            pl.BlockSpec((1, n_v), const2),
            pl.BlockSpec((1, C, n_qk), lambda b, p, i: (b, fwd_only(p, i), 0)),
            pl.BlockSpec((1, C, n_qk), lambda b, p, i: (b, chunk(p, i), 0)),
            pl.BlockSpec((1, C, n_v), lambda b, p, i: (b, chunk(p, i), 0)),
            pl.BlockSpec((1, C, n_v), lambda b, p, i: (b, fwd_only(p, i), 0)),
            pl.BlockSpec((1, Lc, n_qk), lambda b, p, i: (b, 0, 0)),
            pl.BlockSpec((1, Lc, n_v), lambda b, p, i: (b, 0, 0)),
        ],
        out_specs=pl.BlockSpec((1, C, n_v), lambda b, p, i: (b, fwd_only(p, i), 0)),
        out_shape=jax.ShapeDtypeStruct((B, L, n_v), BF16),
        scratch_shapes=[
            pltpu.VMEM((RET_HEADS, C, C), F32),
            pltpu.VMEM((C, n_qk), F32), pltpu.VMEM((C, n_qk), F32),
            pltpu.VMEM((C, n_qk), F32), pltpu.VMEM((C, n_qk), F32),
            pltpu.VMEM((n_pairs, 2 * RET_DK, 2 * RET_DV), F32),
            pltpu.VMEM((n_pairs, 2 * RET_DK, 2 * RET_DV), F32),
            pltpu.VMEM((n, n_pairs, 2 * RET_DK, 2 * RET_DV), BF16),
        ],
        compiler_params=_cparams(("arbitrary", "arbitrary", "arbitrary")),
        name="retention",
    )(log_gamma, lgf, lgb, lgvf, lgvb, g_ret_out, rq, rk, rv, rg, rk_c, rv_c)


def _out_router_kernel(mla_ref, ret_ref, x_ref, mod_ref, wo_ref, gffn_ref, wr_ref, br_ref,
                       x1_ref, hf_ref, idx_ref, rank_ref, gate_ref, cnt_ref):
    tm = x_ref.shape[0]
    n_mla = mla_ref.shape[1]
    y = jnp.dot(mla_ref[...], wo_ref[0:n_mla, :], preferred_element_type=F32)
    y = y + jnp.dot(ret_ref[...], wo_ref[n_mla:, :], preferred_element_type=F32)
    x1 = x_ref[...] + mod_ref[0, 2:3, :] * y
    x1_ref[...] = x1
    ms = jnp.mean(x1 * x1, axis=-1, keepdims=True)
    hf = x1 * lax.rsqrt(ms + EPS) * gffn_ref[...]
    hf = hf * (1.0 + mod_ref[0, 4:5, :]) + mod_ref[0, 3:4, :]
    for c in range(hf.shape[1] // LANES):
        hf_ref[pl.ds(c, tm, stride=SUBLANES), :] = hf[:, c * LANES:(c + 1) * LANES]

    logits = jnp.dot(hf, wr_ref[...], preferred_element_type=F32,
                     precision=lax.Precision.HIGHEST) + br_ref[...]
    lt = logits.T[0:N_EXPERTS, :]
    e_iota = lax.broadcasted_iota(jnp.int32, (N_EXPERTS, tm), 0).astype(F32)
    s_iota = lax.broadcasted_iota(jnp.int32, (tm, tm), 0)
    t_iota = lax.broadcasted_iota(jnp.int32, (tm, tm), 1)
    upper = (s_iota <= t_iota).astype(BF16)
    vals, idxs, ranks = [], [], []
    seen = jnp.zeros((N_EXPERTS, 1), F32)
    for _ in range(TOP_K):
        mx = jnp.max(lt, axis=0, keepdims=True)
        ix = jnp.min(jnp.where(lt == mx, e_iota, float(N_EXPERTS)), axis=0, keepdims=True)
        hit = e_iota == ix
        lt = jnp.where(hit, -jnp.inf, lt)
        onehot = hit.astype(F32)
        prefix = jnp.dot(hit.astype(BF16), upper, preferred_element_type=F32)
        rank = jnp.sum(onehot * (prefix - 1.0 + seen), axis=0, keepdims=True)
        seen = seen + jnp.sum(onehot, axis=1, keepdims=True)
        vals.append(mx)
        idxs.append(ix)
        ranks.append(rank.astype(jnp.int32))
    ex = [jnp.exp(v - vals[0]) for v in vals]
    den = ex[0] + ex[1] + ex[2] + ex[3]
    idx_ref[...] = jnp.concatenate(idxs, axis=0).astype(jnp.int32)
    rank_ref[...] = jnp.concatenate(ranks, axis=0)
    gate_ref[...] = jnp.concatenate([e / den for e in ex], axis=0)
    cnt_ref[0] = jnp.broadcast_to(seen, (N_EXPERTS, LANES)).astype(jnp.int32)


def _out_router(mla, ret, x2, mod3, w_out_b, g_ffn, w_r, b_r, L, tm):
    T, D = x2.shape
    n_tiles = T // tm
    per_b = L // tm
    const = lambda i: (0, 0)
    return pl.pallas_call(
        _out_router_kernel,
        grid=(n_tiles,),
        in_specs=[
            pl.BlockSpec((tm, mla.shape[1]), lambda i: (i, 0)),
            pl.BlockSpec((tm, ret.shape[1]), lambda i: (i, 0)),
            pl.BlockSpec((tm, D), lambda i: (i, 0)),
            pl.BlockSpec((1, N_MOD, D), lambda i: (i // per_b, 0, 0)),
            pl.BlockSpec(w_out_b.shape, const),
            pl.BlockSpec((1, D), const),
            pl.BlockSpec(w_r.shape, const),
            pl.BlockSpec((1, LANES), const),
        ],
        out_specs=[
            pl.BlockSpec((tm, D), lambda i: (i, 0)),
            pl.BlockSpec((tm * SUBLANES, LANES), lambda i: (i, 0)),
            pl.BlockSpec((TOP_K, tm), lambda i: (0, i)),
            pl.BlockSpec((TOP_K, tm), lambda i: (0, i)),
            pl.BlockSpec((TOP_K, tm), lambda i: (0, i)),
            pl.BlockSpec((1, N_EXPERTS, LANES), lambda i: (i, 0, 0)),
        ],
        out_shape=[
            jax.ShapeDtypeStruct((T, D), F32),
            jax.ShapeDtypeStruct((T * SUBLANES, LANES), F32),
            jax.ShapeDtypeStruct((TOP_K, T), jnp.int32),
            jax.ShapeDtypeStruct((TOP_K, T), jnp.int32),
            jax.ShapeDtypeStruct((TOP_K, T), F32),
            jax.ShapeDtypeStruct((n_tiles, N_EXPERTS, LANES), jnp.int32),
        ],
        compiler_params=_cparams(("arbitrary",)),
        name="out_router",
    )(mla, ret, x2, mod3, w_out_b, g_ffn, w_r, b_r)


def _row_copy(src, dst, sem):
    return pltpu.make_async_copy(src, dst, sem)


def _dispatch_kernel(dest_ref, hf_ref, xs_ref, sem, *, td):
    def issue(t, carry):
        src = hf_ref.at[pl.ds(pl.multiple_of(t * SUBLANES, SUBLANES), SUBLANES), :]
        for k in range(TOP_K):
            d = dest_ref[t * TOP_K + k]
            dst = xs_ref.at[pl.ds(pl.multiple_of(d * SUBLANES, SUBLANES), SUBLANES), :]
            _row_copy(src, dst, sem).start()
        return carry

    lax.fori_loop(0, td, issue, 0)
    for k in range(TOP_K):
        _row_copy(hf_ref, xs_ref.at[pl.ds(0, td * SUBLANES), :], sem).wait()


def _dispatch(dest_flat, hf8, n_pad, td):
    T8, _ = hf8.shape
    T = T8 // SUBLANES
    return pl.pallas_call(
        functools.partial(_dispatch_kernel, td=td),
        grid=(T // td,),
        in_specs=[
            pl.BlockSpec((td * TOP_K,), lambda i: (i,), memory_space=pltpu.SMEM),
            pl.BlockSpec((td * SUBLANES, LANES), lambda i: (i, 0)),
        ],
        out_specs=pl.BlockSpec(memory_space=pl.ANY),
        out_shape=jax.ShapeDtypeStruct((n_pad * SUBLANES, LANES), F32),
        scratch_shapes=[pltpu.SemaphoreType.DMA],
        compiler_params=_cparams(("arbitrary",)),
        name="dispatch",
    )(dest_flat, hf8)


def _split_glu_kernel(w_ref, g_ref, l_ref):
    w = w_ref[0].astype(BF16)
    sub = 2 * LANES
    r = lax.broadcasted_iota(jnp.int32, (sub, sub), 0)
    c = lax.broadcasted_iota(jnp.int32, (sub, sub), 1)
    src_col = jnp.where(c < LANES, 2 * c, 2 * (c - LANES) + 1)
    sel = (r == src_col).astype(F32).astype(BF16)
    for s in range(w.shape[1] // sub):
        t = jnp.dot(w[:, s * sub:(s + 1) * sub], sel, preferred_element_type=F32).astype(BF16)
        g_ref[0, :, s * LANES:(s + 1) * LANES] = t[:, :LANES]
        l_ref[0, :, s * LANES:(s + 1) * LANES] = t[:, LANES:]


def _split_glu(w1):
    E, d, f2 = w1.shape
    tn = 1024
    half = jax.ShapeDtypeStruct((E, d, f2 // 2), BF16)
    return pl.pallas_call(
        _split_glu_kernel,
        grid=(E, f2 // tn),
        in_specs=[pl.BlockSpec((1, d, tn), lambda e, j: (e, 0, j))],
        out_specs=[pl.BlockSpec((1, d, tn // 2), lambda e, j: (e, 0, j)),
                   pl.BlockSpec((1, d, tn // 2), lambda e, j: (e, 0, j))],
        out_shape=[half, half],
        compiler_params=_cparams(("arbitrary", "arbitrary")),
        name="split_glu",
    )(w1)


def _experts_kernel(ie_ref, ib_ref, lo_ref, hi_ref, first_ref, ni_ref, xs_ref, w1g_ref, w1l_ref,
                    w2_ref, b1g_ref, b1l_ref, b2_ref, ys_ref):
    i = pl.program_id(0)
    blk = xs_ref.shape[0] // SUBLANES
    d = w1g_ref.shape[1]

    @pl.when(i < ni_ref[0])
    def _():
        cols = [xs_ref[pl.ds(c, blk, stride=SUBLANES), :] for c in range(d // LANES)]
        x = jnp.concatenate(cols, axis=-1)
        row = lax.broadcasted_iota(jnp.int32, (blk, 1), 0)
        mine = (row >= lo_ref[i]) & (row < hi_ref[i])
        x = jnp.where(mine, x, 0.0).astype(BF16)
        hg = jnp.dot(x, w1g_ref[0], preferred_element_type=F32) + b1g_ref[0]
        hl = jnp.dot(x, w1l_ref[0], preferred_element_type=F32) + b1l_ref[0]
        glu = jnp.minimum(hg, SWIGLU_LIMIT)
        lin = jnp.clip(hl, -SWIGLU_LIMIT, SWIGLU_LIMIT)
        act = glu * _sigmoid(SWIGLU_ALPHA * glu) * (lin + 1.0)
        y = jnp.dot(act.astype(BF16), w2_ref[0], preferred_element_type=F32) + b2_ref[0]
        y = jnp.where(mine, y, 0.0)

        @pl.when(first_ref[i] == 1)
        def _():
            for c in range(d // LANES):
                ys_ref[pl.ds(c, blk, stride=SUBLANES), :] = y[:, c * LANES:(c + 1) * LANES]

        @pl.when(first_ref[i] == 0)
        def _():
            for c in range(d // LANES):
                ys_ref[pl.ds(c, blk, stride=SUBLANES), :] += y[:, c * LANES:(c + 1) * LANES]


def _experts(items, xs, w1g, w1l, w2, b1g, b1l, b2, blk):
    item_e, item_blk, item_lo, item_hi, item_first, n_items = items
    d = w1g.shape[1]
    f = w1g.shape[2]
    row_map = lambda i, ie, ib, lo, hi, fi, ni: (ib[i], 0)
    exp_map = lambda i, ie, ib, lo, hi, fi, ni: (ie[i], 0, 0)
    grid_spec = pltpu.PrefetchScalarGridSpec(
        num_scalar_prefetch=6,
        grid=(item_e.shape[0],),
        in_specs=[
            pl.BlockSpec((blk * SUBLANES, LANES), row_map),
            pl.BlockSpec((1, d, f), exp_map),
            pl.BlockSpec((1, d, f), exp_map),
            pl.BlockSpec((1, f, d), exp_map),
            pl.BlockSpec((1, 1, f), exp_map),
            pl.BlockSpec((1, 1, f), exp_map),
            pl.BlockSpec((1, 1, d), exp_map),
        ],
        out_specs=pl.BlockSpec((blk * SUBLANES, LANES), row_map),
    )
    return pl.pallas_call(
        _experts_kernel,
        grid_spec=grid_spec,
        out_shape=jax.ShapeDtypeStruct(xs.shape, F32),
        compiler_params=_cparams(("arbitrary",)),
        name="experts",
    )(item_e, item_blk, item_lo, item_hi, item_first, n_items, xs, w1g, w1l, w2, b1g, b1l, b2)


def _combine_kernel(dest_ref, x1_ref, gate_ref, mod_ref, ys_ref, o_ref, buf_ref, sem, *, tc):
    def issue(t, carry):
        for k in range(TOP_K):
            d = dest_ref[t * TOP_K + k]
            src = ys_ref.at[pl.ds(pl.multiple_of(d * SUBLANES, SUBLANES), SUBLANES), :]
            dst = buf_ref.at[k, pl.ds(pl.multiple_of(t * SUBLANES, SUBLANES), SUBLANES), :]
            _row_copy(src, dst, sem).start()
        return carry

    lax.fori_loop(0, tc, issue, 0)
    for k in range(TOP_K):
        _row_copy(ys_ref.at[pl.ds(0, tc * SUBLANES), :], buf_ref.at[k], sem).wait()

    g = gate_ref[...]
    for c in range(o_ref.shape[1] // LANES):
        cs = slice(c * LANES, (c + 1) * LANES)
        acc = g[:, 0:1] * buf_ref[0, pl.ds(c, tc, stride=SUBLANES), :]
        for k in range(1, TOP_K):
            acc = acc + g[:, k:k + 1] * buf_ref[k, pl.ds(c, tc, stride=SUBLANES), :]
        o_ref[:, cs] = x1_ref[:, cs] + mod_ref[0, 5:6, cs] * acc


def _combine(dest_flat, x1, gates_t, mod3, ys, L, tc):
    T, D = x1.shape
    per_b = L // tc
    return pl.pallas_call(
        functools.partial(_combine_kernel, tc=tc),
        grid=(T // tc,),
        in_specs=[
            pl.BlockSpec((tc * TOP_K,), lambda i: (i,), memory_space=pltpu.SMEM),
            pl.BlockSpec((tc, D), lambda i: (i, 0)),
            pl.BlockSpec((tc, TOP_K), lambda i: (i, 0)),
            pl.BlockSpec((1, N_MOD, D), lambda i: (i // per_b, 0, 0)),
            pl.BlockSpec(memory_space=pl.ANY),
        ],
        out_specs=pl.BlockSpec((tc, D), lambda i: (i, 0)),
        out_shape=jax.ShapeDtypeStruct((T, D), F32),
        scratch_shapes=[pltpu.VMEM((TOP_K, tc * SUBLANES, LANES), F32), pltpu.SemaphoreType.DMA],
        compiler_params=_cparams(("arbitrary",)),
        name="combine",
    )(dest_flat, x1, gates_t, mod3, ys)


def _rope_tables(L, dim, lane_off, width):
    rows = L // GRID_W
    nf = dim // 4
    inv = jnp.power(ROPE_BASE, -jnp.arange(nf, dtype=F32) / nf)
    row = jnp.repeat(jnp.arange(rows, dtype=F32), GRID_W)
    col = jnp.tile(jnp.arange(GRID_W, dtype=F32), rows)
    pos = jnp.stack([row, col], axis=-1)
    ang = pos[:, :, None] * inv
    ang = jnp.broadcast_to(ang[:, :, None, :], (L, 2, 2, nf)).reshape(L, dim)
    sign = jnp.where((jnp.arange(dim) % (dim // 2)) < nf, -1.0, 1.0).astype(F32)
    cos, sin = jnp.cos(ang), jnp.sin(ang) * sign
    if lane_off is None:
        reps = width // dim
        return jnp.tile(cos, (1, reps)), jnp.tile(sin, (1, reps))
    cfull = jnp.ones((L, width), F32).at[:, lane_off:lane_off + dim].set(cos)
    sfull = jnp.zeros((L, width), F32).at[:, lane_off:lane_off + dim].set(sin)
    return cfull, sfull


def _identity_tables(L):
    return jnp.ones((L, LANES), F32), jnp.zeros((L, LANES), F32)


def _prep_weights(w_in, w_q_up, w_kv_up, g_q_head, g_k_head):
    D = w_in.shape[0]
    o = 0
    wq = w_in[:, o:o + Q_LORA]; o += Q_LORA
    wkv = w_in[:, o:o + KV_LORA]; o += KV_LORA
    wpe = w_in[:, o:o + MLA_ROPE]; o += MLA_ROPE
    rest = w_in[:, o:]
    pe_blk = jnp.zeros((D, LANES), w_in.dtype).at[:, MLA_NOPE:MLA_NOPE + MLA_ROPE].set(wpe)
    w_in_r = jnp.concatenate([wq, wkv, pe_blk, rest], axis=1).astype(BF16)
    pad_h = LANES - MLA_QK
    w_q_r = jnp.pad(w_q_up.reshape(Q_LORA, MLA_HEADS, MLA_QK), ((0, 0), (0, 0), (0, pad_h)))
    w_q_r = w_q_r.reshape(Q_LORA, MLA_HEADS * LANES).astype(BF16)
    kv = w_kv_up.reshape(KV_LORA, MLA_HEADS, MLA_NOPE + MLA_V)
    kpart = jnp.pad(kv[:, :, :MLA_NOPE], ((0, 0), (0, 0), (0, LANES - MLA_NOPE)))
    vpart = jnp.pad(kv[:, :, MLA_NOPE:], ((0, 0), (0, 0), (0, LANES - MLA_V)))
    w_kv_r = jnp.concatenate([kpart.reshape(KV_LORA, -1), vpart.reshape(KV_LORA, -1)], axis=1).astype(BF16)
    gqh = jnp.pad(g_q_head, (0, pad_h))[None, :]
    gkh = jnp.pad(g_k_head, (0, pad_h))[None, :]
    return w_in_r, w_q_r, w_kv_r, gqh, gkh


def _routing_tables(idx, rank, counts, tm, blk):
    T = idx.shape[1]
    i32 = jnp.int32
    tot = jnp.sum(counts, axis=0)
    end = jnp.cumsum(tot)
    start = end - tot
    tile_base = start[None, :] + jnp.cumsum(counts, axis=0) - counts
    base_tok = jnp.repeat(tile_base, tm, axis=0)
    hit = idx[:, :, None] == jnp.arange(N_EXPERTS, dtype=i32)
    dest = jnp.sum(jnp.where(hit, base_tok[None], 0), axis=-1) + rank

    n_work = (T * TOP_K) // blk + N_EXPERTS
    first_blk = start // blk
    last_blk = (end - 1) // blk
    per_e = jnp.where(tot > 0, last_blk - first_blk + 1, 0)
    item_end = jnp.cumsum(per_e)
    item_start = item_end - per_e
    n_items = item_end[-1]
    j = jnp.minimum(jnp.arange(n_work, dtype=i32), n_items - 1)
    item_e = jnp.minimum(jnp.sum(item_end[None, :] <= j[:, None], axis=1), N_EXPERTS - 1).astype(i32)
    item_blk = first_blk[item_e] + j - item_start[item_e]
    item_lo = jnp.clip(start[item_e] - item_blk * blk, 0, blk)
    item_hi = jnp.clip(end[item_e] - item_blk * blk, 0, blk)
    prev_blk = jnp.concatenate([jnp.full((1,), -1, i32), item_blk[:-1].astype(i32)])
    item_first = (item_blk != prev_blk).astype(i32)
    items = (item_e, item_blk.astype(i32), item_lo.astype(i32), item_hi.astype(i32), item_first,
             n_items.astype(i32).reshape(1))
    return dest.astype(i32), items


def kernel(x, c, ctx, c_ctx, g_attn, g_ffn, w_ada, b_ada, w_in, g_q_lora, w_q_up, g_q_head,
           g_kv_lora, w_kv_up, g_k_head, ret_decay_logit, g_ret_out, w_out, w_router, b_router,
           w_mlp1, b_mlp1, w_mlp2, b_mlp2):
    B, L, D = x.shape
    Lc = ctx.shape[1]
    T = B * L
    l = 0
    assert w_ada.shape[0] == 1

    rows = ((B + 1 + SUBLANES - 1) // SUBLANES) * SUBLANES
    cc = jnp.zeros((rows, D), F32).at[:B].set(c).at[B].set(c_ctx)
    mod3 = _adaln(cc, w_ada[l], b_ada[l][None, :]).reshape(rows, N_MOD, D)

    w_in_r, w_q_r, w_kv_r, gqh, gkh = _prep_weights(w_in[l], w_q_up[l], w_kv_up[l],
                                                     g_q_head[l], g_k_head[l])
    tabs_x = _rope_tables(L, MLA_ROPE, MLA_NOPE, LANES) + _rope_tables(L, RET_DK, None, LANES)
    tabs_c = _identity_tables(Lc) + _identity_tables(Lc)
    proj = functools.partial(_in_proj, g_attn=g_attn[l][None, :], w_in_r=w_in_r,
                             g_q_lora=g_q_lora[l][None, :], w_q_r=w_q_r, gqh=gqh,
                             g_kv_lora=g_kv_lora[l][None, :], w_kv_r=w_kv_r, gkh=gkh)
    q, kx, vx, rq, rk, rv, rg = proj(x, mod3, lambda b: b, tabs=tabs_x, tm=min(PROJ_TM, L))
    _, kc, vc, _, rk_c, rv_c, _ = proj(ctx, mod3, lambda b: B, tabs=tabs_c, tm=Lc)

    mla = _attention(q, kx, vx, kc, vc, min(ATT_TQ, L), min(ATT_TK, L // 2))

    log_gamma = jax.nn.log_sigmoid(ret_decay_logit[l].astype(F32))
    ret = _retention(log_gamma, g_ret_out[l][None, :], rq, rk, rv, rg, rk_c, rv_c, min(RET_C, L))

    w_r = jnp.pad(w_router[l], ((0, 0), (0, LANES - N_EXPERTS)))
    b_r = jnp.pad(b_router[l], (0, LANES - N_EXPERTS))[None, :]
    tm = min(OUT_TM, L)
    x1, hf8, idx, rank, gates, cnt = _out_router(
        mla.reshape(T, -1), ret.reshape(T, -1), x.reshape(T, D), mod3,
        w_out[l].astype(BF16), g_ffn[l][None, :], w_r, b_r, L, tm)

    blk = MOE_BLK
    assert (T * TOP_K) % blk == 0
    dest, items = _routing_tables(idx, rank, cnt[:, :, 0], tm, blk)
    dest_flat = dest.T.reshape(-1)

    xs = _dispatch(dest_flat, hf8, T * TOP_K, min(DISP_T, L))

    w1g, w1l = _split_glu(w_mlp1[l])
    b1g = b_mlp1[l][:, None, 0::2]
    b1l = b_mlp1[l][:, None, 1::2]
    ys = _experts(items, xs, w1g, w1l, w_mlp2[l].astype(BF16),
                  b1g, b1l, b_mlp2[l][:, None, :], blk)

    out = _combine(dest_flat, x1, gates.T, mod3, ys, L, min(COMB_T, L))
    return out.reshape(B, L, D)
```

```python
import functools
import math

import jax
import jax.numpy as jnp
from jax import lax
from jax.experimental import pallas as pl
from jax.experimental.pallas import tpu as pltpu

F32 = jnp.float32
BF16 = jnp.bfloat16

LANES = 128
SUBLANES = 8
VMEM_LIMIT_BYTES = 56 * 1024 * 1024

EPS = 1e-6
ROPE_BASE = 10000.0
GRID_W = 64
N_MOD = 6
MLA_HEADS = 8
MLA_NOPE = 64
MLA_ROPE = 32
MLA_QK = MLA_NOPE + MLA_ROPE
MLA_V = 64
Q_LORA = 256
KV_LORA = 128
RET_HEADS = 4
RET_DK = 64
RET_DV = 128
N_EXPERTS = 32
TOP_K = 4
SWIGLU_LIMIT = 7.0
SWIGLU_ALPHA = 1.702
LOG2E = 1.4426950408889634

PROJ_TM = 512
ATT_TQ = 512
ATT_TK = 1024
RET_C = 256
OUT_TM = 512
MOE_BLK = 512
DISP_T = 1024
COMB_T = 256


def _cparams(sem):
    return pltpu.CompilerParams(dimension_semantics=sem, vmem_limit_bytes=VMEM_LIMIT_BYTES)


def _sigmoid(x):
    return 1.0 / (1.0 + jnp.exp(-x))


def _adaln_kernel(c_ref, w_ref, b_ref, o_ref):
    c = c_ref[...]
    s = (c * _sigmoid(c)).astype(BF16)
    o_ref[...] = jnp.dot(s, w_ref[...].astype(BF16), preferred_element_type=F32) + b_ref[...]


def _adaln(cc, w_ada, b_ada):
    rows, d = cc.shape
    n = w_ada.shape[1]
    tn = 1536
    return pl.pallas_call(
        _adaln_kernel,
        grid=(n // tn,),
        in_specs=[pl.BlockSpec((rows, d), lambda j: (0, 0)),
                  pl.BlockSpec((d, tn), lambda j: (0, j)),
                  pl.BlockSpec((1, tn), lambda j: (0, j))],
        out_specs=pl.BlockSpec((rows, tn), lambda j: (0, j)),
        out_shape=jax.ShapeDtypeStruct((rows, n), F32),
        compiler_params=_cparams(("arbitrary",)),
        name="adaln",
    )(cc, w_ada, b_ada)


def _in_proj_kernel(x_ref, mod_ref, gattn_ref, win_ref, gql_ref, wq_ref, gqh_ref, gqr_ref,
                    gkvl_ref, wkv_ref, gkh_ref, gkr_ref, cm_ref, sm_ref, cr_ref, sr_ref,
                    q_ref, k_ref, v_ref, rq_ref, rk_ref, rv_ref, rg_ref, *, q_scale):
    x = x_ref[0]
    shift = mod_ref[0, 0:1, :]
    scale = mod_ref[0, 1:2, :]
    ms = jnp.mean(x * x, axis=-1, keepdims=True)
    h = x * lax.rsqrt(ms + EPS) * gattn_ref[...]
    h = h * (1.0 + scale) + shift
    p = jnp.dot(h.astype(BF16), win_ref[...], preferred_element_type=F32)

    lane = lax.broadcasted_iota(jnp.int32, (1, LANES), 1)
    cm, sm = cm_ref[...], sm_ref[...]
    cr, sr = cr_ref[...], sr_ref[...]
    n_hl = MLA_HEADS * LANES

    cq = p[:, 0:Q_LORA]
    cq = cq * lax.rsqrt(jnp.mean(cq * cq, axis=-1, keepdims=True) + EPS) * gql_ref[...]
    qf = jnp.dot(cq.astype(BF16), wq_ref[...], preferred_element_type=F32)
    gq_cos = gqh_ref[...] * cm
    gq_sin = gqr_ref[...] * sm
    for hd in range(MLA_HEADS):
        blk = qf[:, hd * LANES:(hd + 1) * LANES]
        perm = qf[:, n_hl + hd * LANES:n_hl + (hd + 1) * LANES]
        r = lax.rsqrt(jnp.sum(blk * blk, axis=-1, keepdims=True) * (1.0 / MLA_QK) + EPS)
        q_ref[0, hd] = ((blk * gq_cos + perm * gq_sin) * (r * q_scale)).astype(BF16)

    o_kv = Q_LORA
    ckv = p[:, o_kv:o_kv + KV_LORA]
    ckv = ckv * lax.rsqrt(jnp.mean(ckv * ckv, axis=-1, keepdims=True) + EPS) * gkvl_ref[...]
    kvf = jnp.dot(ckv.astype(BF16), wkv_ref[...], preferred_element_type=F32)
    o_pe = o_kv + KV_LORA
    pe = p[:, o_pe:o_pe + LANES]
    pe_perm = p[:, o_pe + LANES:o_pe + 2 * LANES]
    gk = gkh_ref[...]
    pe_rope = pe * (gk * cm) + pe_perm * (gkr_ref[...] * sm)
    pe_ss = jnp.sum(pe * pe, axis=-1, keepdims=True)
    ones_col = (lane == MLA_V).astype(F32)
    for hd in range(MLA_HEADS):
        kn = kvf[:, hd * LANES:(hd + 1) * LANES]
        ss = jnp.sum(kn * kn, axis=-1, keepdims=True) + pe_ss
        r = lax.rsqrt(ss * (1.0 / MLA_QK) + EPS)
        kk = (kn * gk + pe_rope) * r
        k_ref[0, hd] = kk.T.astype(BF16)
        vv = kvf[:, (MLA_HEADS + hd) * LANES:(MLA_HEADS + hd + 1) * LANES] + ones_col
        v_ref[0, hd] = vv.astype(BF16)

    n_qk = RET_HEADS * RET_DK
    o_rq = o_pe + 2 * LANES
    o_rk = o_rq + 2 * n_qk
    for j in range(n_qk // LANES):
        js = slice(j * LANES, (j + 1) * LANES)
        a = p[:, o_rq + j * LANES:o_rq + (j + 1) * LANES]
        b = p[:, o_rq + n_qk + j * LANES:o_rq + n_qk + (j + 1) * LANES]
        rq_ref[0, :, js] = (a * cr + b * sr).astype(BF16)
        a = p[:, o_rk + j * LANES:o_rk + (j + 1) * LANES]
        b = p[:, o_rk + n_qk + j * LANES:o_rk + n_qk + (j + 1) * LANES]
        rk_ref[0, :, js] = ((a * cr + b * sr) * (RET_DK ** -0.5)).astype(BF16)
    o_rv = o_rk + 2 * n_qk
    n_v = RET_HEADS * RET_DV
    rv_ref[0] = p[:, o_rv:o_rv + n_v].astype(BF16)
    rg_ref[0] = p[:, o_rv + n_v:o_rv + 2 * n_v]


def _in_proj(x, mod3, mod_row_of_batch, g_attn, w_in_r, g_q_lora, w_q_r, gqh, gqr, g_kv_lora,
             w_kv_r, gkh, gkr, tabs, tm):
    B, L, D = x.shape
    cm, sm, cr, sr = tabs
    n_in = w_in_r.shape[1]
    const = lambda b, i: (0, 0)
    tab_spec = pl.BlockSpec((tm, LANES), lambda b, i: (i, 0))
    head_spec = pl.BlockSpec((1, MLA_HEADS, tm, LANES), lambda b, i: (b, 0, i, 0))
    n_qk = RET_HEADS * RET_DK
    n_v = RET_HEADS * RET_DV
    seq_spec = lambda w: pl.BlockSpec((1, tm, w), lambda b, i: (b, i, 0))
    head_shape = jax.ShapeDtypeStruct((B, MLA_HEADS, L, LANES), BF16)
    q_scale = MLA_QK ** -0.5 * LOG2E
    return pl.pallas_call(
        functools.partial(_in_proj_kernel, q_scale=q_scale),
        grid=(B, L // tm),
        in_specs=[
            pl.BlockSpec((1, tm, D), lambda b, i: (b, i, 0)),
            pl.BlockSpec((1, N_MOD, D), lambda b, i: (mod_row_of_batch(b), 0, 0)),
            pl.BlockSpec((1, D), const),
            pl.BlockSpec((D, n_in), const),
            pl.BlockSpec((1, Q_LORA), const),
            pl.BlockSpec(w_q_r.shape, const),
            pl.BlockSpec((1, LANES), const),
            pl.BlockSpec((1, LANES), const),
            pl.BlockSpec((1, KV_LORA), const),
            pl.BlockSpec(w_kv_r.shape, const),
            pl.BlockSpec((1, LANES), const),
            pl.BlockSpec((1, LANES), const),
            tab_spec, tab_spec, tab_spec, tab_spec,
        ],
        out_specs=[head_spec,
                   pl.BlockSpec((1, MLA_HEADS, LANES, tm), lambda b, i: (b, 0, 0, i)),
                   head_spec,
                   seq_spec(n_qk), seq_spec(n_qk), seq_spec(n_v), seq_spec(n_v)],
        out_shape=[head_shape,
                   jax.ShapeDtypeStruct((B, MLA_HEADS, LANES, L), BF16),
                   head_shape,
                   jax.ShapeDtypeStruct((B, L, n_qk), BF16),
                   jax.ShapeDtypeStruct((B, L, n_qk), BF16),
                   jax.ShapeDtypeStruct((B, L, n_v), BF16),
                   jax.ShapeDtypeStruct((B, L, n_v), F32)],
        compiler_params=_cparams(("arbitrary", "arbitrary")),
        name="in_proj",
    )(x, mod3, g_attn, w_in_r, g_q_lora, w_q_r, gqh, gqr, g_kv_lora, w_kv_r, gkh, gkr,
      cm, sm, cr, sr)


def _attn_kernel(q_ref, kx_ref, vx_ref, kc_ref, vc_ref, o_ref, m_ref, acc_ref, s_ref, *, tk):
    n_kv = kx_ref.shape[3] // tk
    n_heads = q_ref.shape[1]

    def scores(hh, start, slot):
        kb = kx_ref[0, hh, :, pl.ds(start, tk)]
        s_ref[hh, slot] = jnp.dot(q_ref[0, hh], kb, preferred_element_type=F32)

    def consume(hh, start, slot):
        s = s_ref[hh, slot]
        m_old = m_ref[hh]
        m_new = jnp.maximum(m_old, jnp.max(s, axis=-1, keepdims=True))
        alpha = jnp.exp2(m_old - m_new)
        pr = jnp.exp2(s - m_new)
        vb = vx_ref[0, hh, pl.ds(start, tk), :]
        acc_ref[hh] = alpha * acc_ref[hh] + jnp.dot(pr.astype(BF16), vb,
                                                    preferred_element_type=F32)
        m_ref[hh] = m_new

    def step(j, slot, prefetch):
        if prefetch:
            nxt = pl.multiple_of((j + 1) * tk, tk)
            for hh in range(n_heads):
                scores(hh, nxt, 1 - slot)
        cur = pl.multiple_of(j * tk, tk)
        for hh in range(n_heads):
            consume(hh, cur, slot)

    for hh in range(n_heads):
        scores(hh, 0, 0)
        s = jnp.dot(q_ref[0, hh], kc_ref[0, hh], preferred_element_type=F32)
        m0 = jnp.max(s, axis=-1, keepdims=True)
        m_ref[hh] = m0
        acc_ref[hh] = jnp.dot(jnp.exp2(s - m0).astype(BF16), vc_ref[0, hh],
                              preferred_element_type=F32)

    def body(jj, carry):
        step(2 * jj, 0, True)
        step(2 * jj + 1, 1, True)
        return carry

    lax.fori_loop(0, n_kv // 2 - 1, body, 0)
    step(n_kv - 2, 0, True)
    step(n_kv - 1, 1, False)
    outs = []
    for hh in range(n_heads):
        acc = acc_ref[hh]
        outs.append(acc[:, :MLA_V] / acc[:, MLA_V:MLA_V + 1])
    o_ref[0] = jnp.concatenate(outs, axis=-1).astype(BF16)


def _attention(q, kx, vx, kc, vc, tq, tk):
    B, H, L, _ = q.shape
    Lc = kc.shape[3]
    return pl.pallas_call(
        functools.partial(_attn_kernel, tk=tk),
        grid=(B, H // 2, L // tq),
        in_specs=[
            pl.BlockSpec((1, 2, tq, LANES), lambda b, h, i: (b, h, i, 0)),
            pl.BlockSpec((1, 2, LANES, L), lambda b, h, i: (b, h, 0, 0)),
            pl.BlockSpec((1, 2, L, LANES), lambda b, h, i: (b, h, 0, 0)),
            pl.BlockSpec((1, 2, LANES, Lc), lambda b, h, i: (b, h, 0, 0)),
            pl.BlockSpec((1, 2, Lc, LANES), lambda b, h, i: (b, h, 0, 0)),
        ],
        out_specs=pl.BlockSpec((1, tq, LANES), lambda b, h, i: (b, i, h)),
        out_shape=jax.ShapeDtypeStruct((B, L, H * MLA_V), BF16),
        scratch_shapes=[pltpu.VMEM((2, tq, 1), F32), pltpu.VMEM((2, tq, LANES), F32),
                        pltpu.VMEM((2, 2, tq, tk), F32)],
        compiler_params=_cparams(("arbitrary", "arbitrary", "arbitrary")),
        name="attention",
    )(q, kx, vx, kc, vc)


def _ret_kernel(lg_ref, lgf_ref, lgb_ref, lgvf_ref, lgvb_ref, gout_ref, rq_ref, rk_ref, rv_ref, rg_ref,
                kc_ref, vc_ref, o_ref, dm_ref, qdf_ref, qdb_ref, kdf_ref, kdb_ref,
                f_ref, r_ref, rs_ref, *, n_chunks):
    C = rq_ref.shape[1]
    Lc = kc_ref.shape[1]
    ps = pl.program_id(1)
    i = pl.program_id(2)
    n_pairs = RET_HEADS // 2
    pw = 2 * RET_DK
    vw = 2 * RET_DV
    tdn = (((0,), (0,)), ((), ()))
    ndn = (((1,), (1,)), ((), ()))
    lgf = lgf_ref[...]
    lgb = lgb_ref[...]

    @pl.when((pl.program_id(0) == 0) & (ps == 0) & (i == 0))
    def _tables():
        a = lax.broadcasted_iota(jnp.int32, (C, C), 0)
        b = lax.broadcasted_iota(jnp.int32, (C, C), 1)
        dab = (a - b).astype(F32)
        for hd in range(RET_HEADS):
            fwd = jnp.where(a >= b, jnp.exp(jnp.where(a >= b, dab, 0.0) * lg_ref[0, hd]), 0.0)
            bwd = jnp.where(b >= a, jnp.exp(jnp.where(b >= a, -dab, 0.0) * lg_ref[1, hd]), 0.0)
            dm_ref[hd] = fwd + bwd
        row = lax.broadcasted_iota(jnp.int32, (C, 1), 0).astype(F32)
        qdf_ref[...] = jnp.exp((row + 1.0) * lgf)
        qdb_ref[...] = jnp.exp((C - row) * lgb)
        kdf_ref[...] = jnp.exp((C - 1.0 - row) * lgf)
        kdb_ref[...] = jnp.exp(row * lgb)

    @pl.when((ps == 0) & (i == 0))
    def _init_states():
        rowc = lax.broadcasted_iota(jnp.int32, (Lc, 1), 0).astype(F32)
        wf = jnp.exp((Lc - 1.0 - rowc) * lgf)
        wb = jnp.exp(rowc * lgb)
        kc = kc_ref[0].astype(F32)
        for pr in range(n_pairs):
            kp = kc[:, pr * pw:(pr + 1) * pw]
            vp = vc_ref[0, :, pr * vw:(pr + 1) * vw]
            f_ref[pr] = lax.dot_general((kp * wf[:, pr * pw:(pr + 1) * pw]).astype(BF16), vp, tdn,
                                        preferred_element_type=F32)
            r_ref[pr] = lax.dot_general((kp * wb[:, pr * pw:(pr + 1) * pw]).astype(BF16), vp, tdn,
                                        preferred_element_type=F32)

    @pl.when(ps == 0)
    def _backward_states():
        c = n_chunks - 1 - i
        k = rk_ref[0].astype(F32)
        cdb = jnp.exp(C * lgvb_ref[...])
        for pr in range(n_pairs):
            r_old = r_ref[pr]
            rs_ref[c, pr] = r_old.astype(BF16)
            kp = (k[:, pr * pw:(pr + 1) * pw] * kdb_ref[:, pr * pw:(pr + 1) * pw]).astype(BF16)
            vp = rv_ref[0, :, pr * vw:(pr + 1) * vw]
            upd = lax.dot_general(kp, vp, tdn, preferred_element_type=F32)
            r_ref[pr] = r_old * cdb[:, pr * vw:(pr + 1) * vw] + upd

    @pl.when(ps == 1)
    def _forward_outputs():
        q = rq_ref[0].astype(F32)
        k = rk_ref[0].astype(F32)
        cdf = jnp.exp(C * lgvf_ref[...])
        lane = lax.broadcasted_iota(jnp.int32, (1, pw), 1)
        for pr in range(n_pairs):
            sl = slice(pr * pw, (pr + 1) * pw)
            qp = q[:, sl]
            kpb = rk_ref[0, :, sl]
            qf = qp * qdf_ref[:, sl]
            qb = qp * qdb_ref[:, sl]
            fb = f_ref[pr].astype(BF16)
            rb = rs_ref[i, pr]
            for hh in range(2):
                hd = 2 * pr + hh
                hm = (lane // RET_DK) == hh
                vs = slice(hd * RET_DV, (hd + 1) * RET_DV)
                fs = slice(hh * RET_DV, (hh + 1) * RET_DV)
                a = lax.dot_general(jnp.where(hm, qp, 0.0).astype(BF16), kpb, ndn,
                                    preferred_element_type=F32)
                a = (a * dm_ref[hd]).astype(BF16)
                o = jnp.dot(a, rv_ref[0, :, vs], preferred_element_type=F32)
                o = o + jnp.dot(jnp.where(hm, qf, 0.0).astype(BF16), fb[:, fs],
                                preferred_element_type=F32)
                o = o + jnp.dot(jnp.where(hm, qb, 0.0).astype(BF16), rb[:, fs],
                                preferred_element_type=F32)
                o = o * lax.rsqrt(jnp.mean(o * o, axis=-1, keepdims=True) + EPS) * gout_ref[:, vs]
                g = rg_ref[0, :, vs]
                o_ref[0, :, vs] = (o * (g * _sigmoid(g))).astype(BF16)
            kp = (k[:, sl] * kdf_ref[:, sl]).astype(BF16)
            vp = rv_ref[0, :, pr * vw:(pr + 1) * vw]
            upd = lax.dot_general(kp, vp, tdn, preferred_element_type=F32)
            f_ref[pr] = f_ref[pr] * cdf[:, pr * vw:(pr + 1) * vw] + upd


def _retention(log_gamma, g_ret_out, rq, rk, rv, rg, rk_c, rv_c, C):
    B, L, n_qk = rq.shape
    n_v = rv.shape[2]
    Lc = rk_c.shape[1]
    n = L // C
    lgf = jnp.repeat(log_gamma[0], RET_DK)[None, :]
    lgb = jnp.repeat(log_gamma[1], RET_DK)[None, :]
    lgvf = jnp.repeat(log_gamma[0], RET_DV)[None, :]
    lgvb = jnp.repeat(log_gamma[1], RET_DV)[None, :]
    chunk = lambda p, i: jnp.where(p == 0, n - 1 - i, i)
    fwd_only = lambda p, i: jnp.where(p == 0, 0, i)
    const2 = lambda b, p, i: (0, 0)
    n_pairs = RET_HEADS // 2
    return pl.pallas_call(
        functools.partial(_ret_kernel, n_chunks=n),
        grid=(B, 2, n),
        in_specs=[
            pl.BlockSpec(memory_space=pltpu.SMEM),
            pl.BlockSpec((1, n_qk), const2),
            pl.BlockSpec((1, n_qk), const2),
            pl.BlockSpec((1, n_v), const2),
            pl.BlockSpec((1, n_v), const2),
            pl.BlockSpec((1, n_v), const2),
            pl.BlockSpec((1, C, n_qk), lambda b, p, i: (b, fwd_only(p, i), 0)),
            pl.BlockSpec((1, C, n_qk), lambda b, p, i: (b, chunk(p, i), 0)),
            pl.BlockSpec((1, C, n_v), lambda b, p, i: (b, chunk(p, i), 0)),
            pl.BlockSpec((1, C, n_v), lambda b, p, i: (b, fwd_only(p, i), 0)),
            pl.BlockSpec((1, Lc, n_qk), lambda b, p, i: (b, 0, 0)),
            pl.BlockSpec((1, Lc, n_v), lambda b, p, i: (b, 0, 0)),
        ],
        out_specs=pl.BlockSpec((1, C, n_v), lambda b, p, i: (b, fwd_only(p, i), 0)),
        out_shape=jax.ShapeDtypeStruct((B, L, n_v), BF16),
        scratch_shapes=[
            pltpu.VMEM((RET_HEADS, C, C), F32),
            pltpu.VMEM((C, n_qk), F32), pltpu.VMEM((C, n_qk), F32),
            pltpu.VMEM((C, n_qk), F32), pltpu.VMEM((C, n_qk), F32),
            pltpu.VMEM((n_pairs, 2 * RET_DK, 2 * RET_DV), F32),
            pltpu.VMEM((n_pairs, 2 * RET_DK, 2 * RET_DV), F32),
            pltpu.VMEM((n, n_pairs, 2 * RET_DK, 2 * RET_DV), BF16),
        ],
        compiler_params=_cparams(("arbitrary", "arbitrary", "arbitrary")),
        name="retention",
    )(log_gamma, lgf, lgb, lgvf, lgvb, g_ret_out, rq, rk, rv, rg, rk_c, rv_c)


def _out_router_kernel(mla_ref, ret_ref, x_ref, mod_ref, wo_ref, gffn_ref, wr_ref, br_ref,
                       x1_ref, hf_ref, idx_ref, rank_ref, gate_ref, cnt_ref):
    tm = x_ref.shape[0]
    n_mla = mla_ref.shape[1]
    y = jnp.dot(mla_ref[...], wo_ref[0:n_mla, :], preferred_element_type=F32)
    y = y + jnp.dot(ret_ref[...], wo_ref[n_mla:, :], preferred_element_type=F32)
    x1 = x_ref[...] + mod_ref[0, 2:3, :] * y
    x1_ref[...] = x1
    ms = jnp.mean(x1 * x1, axis=-1, keepdims=True)
    hf = x1 * lax.rsqrt(ms + EPS) * gffn_ref[...]
    hf = hf * (1.0 + mod_ref[0, 4:5, :]) + mod_ref[0, 3:4, :]
    for c in range(hf.shape[1] // LANES):
        hf_ref[pl.ds(c, tm, stride=SUBLANES), :] = hf[:, c * LANES:(c + 1) * LANES]

    logits = jnp.dot(hf, wr_ref[...], preferred_element_type=F32,
                     precision=lax.Precision.HIGHEST) + br_ref[...]
    lt = logits.T[0:N_EXPERTS, :]
    e_iota = lax.broadcasted_iota(jnp.int32, (N_EXPERTS, tm), 0).astype(F32)
    s_iota = lax.broadcasted_iota(jnp.int32, (tm, tm), 0)
    t_iota = lax.broadcasted_iota(jnp.int32, (tm, tm), 1)
    upper = (s_iota <= t_iota).astype(BF16)
    vals, idxs, ranks = [], [], []
    seen = jnp.zeros((N_EXPERTS, 1), F32)
    for _ in range(TOP_K):
        mx = jnp.max(lt, axis=0, keepdims=True)
        ix = jnp.min(jnp.where(lt == mx, e_iota, float(N_EXPERTS)), axis=0, keepdims=True)
        hit = e_iota == ix
        lt = jnp.where(hit, -jnp.inf, lt)
        onehot = hit.astype(F32)
        prefix = jnp.dot(hit.astype(BF16), upper, preferred_element_type=F32)
        rank = jnp.sum(onehot * (prefix - 1.0 + seen), axis=0, keepdims=True)
        seen = seen + jnp.sum(onehot, axis=1, keepdims=True)
        vals.append(mx)
        idxs.append(ix)
        ranks.append(rank.astype(jnp.int32))
    ex = [jnp.exp(v - vals[0]) for v in vals]
    den = ex[0] + ex[1] + ex[2] + ex[3]
    idx_ref[...] = jnp.concatenate(idxs, axis=0).astype(jnp.int32)
    rank_ref[...] = jnp.concatenate(ranks, axis=0)
    gate_ref[...] = jnp.concatenate([e / den for e in ex], axis=0)
    cnt_ref[0] = jnp.broadcast_to(seen, (N_EXPERTS, LANES)).astype(jnp.int32)


def _out_router(mla, ret, x2, mod3, w_out_b, g_ffn, w_r, b_r, L, tm):
    T, D = x2.shape
    n_tiles = T // tm
    per_b = L // tm
    const = lambda i: (0, 0)
    return pl.pallas_call(
        _out_router_kernel,
        grid=(n_tiles,),
        in_specs=[
            pl.BlockSpec((tm, mla.shape[1]), lambda i: (i, 0)),
            pl.BlockSpec((tm, ret.shape[1]), lambda i: (i, 0)),
            pl.BlockSpec((tm, D), lambda i: (i, 0)),
            pl.BlockSpec((1, N_MOD, D), lambda i: (i // per_b, 0, 0)),
            pl.BlockSpec(w_out_b.shape, const),
            pl.BlockSpec((1, D), const),
            pl.BlockSpec(w_r.shape, const),
            pl.BlockSpec((1, LANES), const),
        ],
        out_specs=[
            pl.BlockSpec((tm, D), lambda i: (i, 0)),
            pl.BlockSpec((tm * SUBLANES, LANES), lambda i: (i, 0)),
            pl.BlockSpec((TOP_K, tm), lambda i: (0, i)),
            pl.BlockSpec((TOP_K, tm), lambda i: (0, i)),
            pl.BlockSpec((TOP_K, tm), lambda i: (0, i)),
            pl.BlockSpec((1, N_EXPERTS, LANES), lambda i: (i, 0, 0)),
        ],
        out_shape=[
            jax.ShapeDtypeStruct((T, D), F32),
            jax.ShapeDtypeStruct((T * SUBLANES, LANES), F32),
            jax.ShapeDtypeStruct((TOP_K, T), jnp.int32),
            jax.ShapeDtypeStruct((TOP_K, T), jnp.int32),
            jax.ShapeDtypeStruct((TOP_K, T), F32),
            jax.ShapeDtypeStruct((n_tiles, N_EXPERTS, LANES), jnp.int32),
        ],
        compiler_params=_cparams(("arbitrary",)),
        name="out_router",
    )(mla, ret, x2, mod3, w_out_b, g_ffn, w_r, b_r)


def _row_copy(src, dst, sem):
    return pltpu.make_async_copy(src, dst, sem)


def _dispatch_kernel(dest_ref, hf_ref, xs_ref, sem, *, td):
    def issue(t, carry):
        src = hf_ref.at[pl.ds(pl.multiple_of(t * SUBLANES, SUBLANES), SUBLANES), :]
        for k in range(TOP_K):
            d = dest_ref[t * TOP_K + k]
            dst = xs_ref.at[pl.ds(pl.multiple_of(d * SUBLANES, SUBLANES), SUBLANES), :]
            _row_copy(src, dst, sem).start(priority=k % 2)
        return carry

    lax.fori_loop(0, td, issue, 0)
    for k in range(TOP_K):
        _row_copy(hf_ref, xs_ref.at[pl.ds(0, td * SUBLANES), :], sem).wait()


def _dispatch(dest_flat, hf8, n_pad, td):
    T8, _ = hf8.shape
    T = T8 // SUBLANES
    return pl.pallas_call(
        functools.partial(_dispatch_kernel, td=td),
        grid=(T // td,),
        in_specs=[
            pl.BlockSpec((td * TOP_K,), lambda i: (i,), memory_space=pltpu.SMEM),
            pl.BlockSpec((td * SUBLANES, LANES), lambda i: (i, 0)),
        ],
        out_specs=pl.BlockSpec(memory_space=pl.ANY),
        out_shape=jax.ShapeDtypeStruct((n_pad * SUBLANES, LANES), F32),
        scratch_shapes=[pltpu.SemaphoreType.DMA],
        compiler_params=_cparams(("arbitrary",)),
        name="dispatch",
    )(dest_flat, hf8)


def _split_glu_kernel(w_ref, g_ref, l_ref):
    w = w_ref[0].astype(BF16)
    sub = 2 * LANES
    r = lax.broadcasted_iota(jnp.int32, (sub, sub), 0)
    c = lax.broadcasted_iota(jnp.int32, (sub, sub), 1)
    src_col = jnp.where(c < LANES, 2 * c, 2 * (c - LANES) + 1)
    sel = (r == src_col).astype(F32).astype(BF16)
    for s in range(w.shape[1] // sub):
        t = jnp.dot(w[:, s * sub:(s + 1) * sub], sel, preferred_element_type=F32).astype(BF16)
        g_ref[0, :, s * LANES:(s + 1) * LANES] = t[:, :LANES]
        l_ref[0, :, s * LANES:(s + 1) * LANES] = t[:, LANES:]


def _split_glu(w1):
    E, d, f2 = w1.shape
    tn = 1024
    half = jax.ShapeDtypeStruct((E, d, f2 // 2), BF16)
    return pl.pallas_call(
        _split_glu_kernel,
        grid=(E, f2 // tn),
        in_specs=[pl.BlockSpec((1, d, tn), lambda e, j: (e, 0, j))],
        out_specs=[pl.BlockSpec((1, d, tn // 2), lambda e, j: (e, 0, j)),
                   pl.BlockSpec((1, d, tn // 2), lambda e, j: (e, 0, j))],
        out_shape=[half, half],
        compiler_params=_cparams(("arbitrary", "arbitrary")),
        name="split_glu",
    )(w1)


def _experts_kernel(ie_ref, ib_ref, lo_ref, hi_ref, first_ref, ni_ref, xs_ref, w1g_ref, w1l_ref,
                    w2_ref, b1g_ref, b1l_ref, b2_ref, ys_ref):
    i = pl.program_id(0)
    blk = xs_ref.shape[0] // SUBLANES
    d = w1g_ref.shape[1]

    @pl.when(i < ni_ref[0])
    def _():
        cols = [xs_ref[pl.ds(c, blk, stride=SUBLANES), :] for c in range(d // LANES)]
        x = jnp.concatenate(cols, axis=-1)
        row = lax.broadcasted_iota(jnp.int32, (blk, 1), 0)
        mine = (row >= lo_ref[i]) & (row < hi_ref[i])
        x = jnp.where(mine, x, 0.0).astype(BF16)
        hg = jnp.dot(x, w1g_ref[0], preferred_element_type=F32) + b1g_ref[0]
        hl = jnp.dot(x, w1l_ref[0], preferred_element_type=F32) + b1l_ref[0]
        glu = jnp.minimum(hg, SWIGLU_LIMIT)
        lin = jnp.clip(hl, -SWIGLU_LIMIT, SWIGLU_LIMIT)
        act = glu * _sigmoid(SWIGLU_ALPHA * glu) * (lin + 1.0)
        y = jnp.dot(act.astype(BF16), w2_ref[0], preferred_element_type=F32) + b2_ref[0]
        y = jnp.where(mine, y, 0.0)

        @pl.when(first_ref[i] == 1)
        def _():
            for c in range(d // LANES):
                ys_ref[pl.ds(c, blk, stride=SUBLANES), :] = y[:, c * LANES:(c + 1) * LANES]

        @pl.when(first_ref[i] == 0)
        def _():
            for c in range(d // LANES):
                ys_ref[pl.ds(c, blk, stride=SUBLANES), :] += y[:, c * LANES:(c + 1) * LANES]


def _experts(items, xs, w1g, w1l, w2, b1g, b1l, b2, blk):
    item_e, item_blk, item_lo, item_hi, item_first, n_items = items
    d = w1g.shape[1]
    f = w1g.shape[2]
    row_map = lambda i, ie, ib, lo, hi, fi, ni: (ib[i], 0)
    exp_map = lambda i, ie, ib, lo, hi, fi, ni: (ie[i], 0, 0)
    grid_spec = pltpu.PrefetchScalarGridSpec(
        num_scalar_prefetch=6,
        grid=(item_e.shape[0],),
        in_specs=[
            pl.BlockSpec((blk * SUBLANES, LANES), row_map),
            pl.BlockSpec((1, d, f), exp_map),
            pl.BlockSpec((1, d, f), exp_map),
            pl.BlockSpec((1, f, d), exp_map),
            pl.BlockSpec((1, 1, f), exp_map),
            pl.BlockSpec((1, 1, f), exp_map),
            pl.BlockSpec((1, 1, d), exp_map),
        ],
        out_specs=pl.BlockSpec((blk * SUBLANES, LANES), row_map),
    )
    return pl.pallas_call(
        _experts_kernel,
        grid_spec=grid_spec,
        out_shape=jax.ShapeDtypeStruct(xs.shape, F32),
        compiler_params=_cparams(("arbitrary",)),
        name="experts",
    )(item_e, item_blk, item_lo, item_hi, item_first, n_items, xs, w1g, w1l, w2, b1g, b1l, b2)


def _combine_kernel(dest_ref, dnext_ref, x1_ref, gate_ref, mod_ref, ys_ref, o_ref, buf_ref, sems,
                    *, tc):
    i = pl.program_id(0)
    n = pl.num_programs(0)
    slot = i % 2

    def gather(idx_ref, s):
        def issue(t, carry):
            for k in range(TOP_K):
                d = idx_ref[t * TOP_K + k]
                src = ys_ref.at[pl.ds(pl.multiple_of(d * SUBLANES, SUBLANES), SUBLANES), :]
                dst = buf_ref.at[s, k, pl.ds(pl.multiple_of(t * SUBLANES, SUBLANES), SUBLANES), :]
                _row_copy(src, dst, sems.at[s]).start(priority=k % 2)
            return carry
        lax.fori_loop(0, tc, issue, 0)

    @pl.when(i == 0)
    def _():
        gather(dest_ref, 0)

    @pl.when(i + 1 < n)
    def _():
        gather(dnext_ref, 1 - slot)

    for k in range(TOP_K):
        _row_copy(ys_ref.at[pl.ds(0, tc * SUBLANES), :], buf_ref.at[slot, k], sems.at[slot]).wait()

    g = gate_ref[...]
    for c in range(o_ref.shape[1] // LANES):
        cs = slice(c * LANES, (c + 1) * LANES)
        acc = g[:, 0:1] * buf_ref[slot, 0, pl.ds(c, tc, stride=SUBLANES), :]
        for k in range(1, TOP_K):
            acc = acc + g[:, k:k + 1] * buf_ref[slot, k, pl.ds(c, tc, stride=SUBLANES), :]
        o_ref[:, cs] = x1_ref[:, cs] + mod_ref[0, 5:6, cs] * acc


def _combine(dest_flat, x1, gates_t, mod3, ys, L, tc):
    T, D = x1.shape
    per_b = L // tc
    n = T // tc
    return pl.pallas_call(
        functools.partial(_combine_kernel, tc=tc),
        grid=(n,),
        in_specs=[
            pl.BlockSpec((tc * TOP_K,), lambda i: (i,), memory_space=pltpu.SMEM),
            pl.BlockSpec((tc * TOP_K,), lambda i: (jnp.minimum(i + 1, n - 1),),
                         memory_space=pltpu.SMEM),
            pl.BlockSpec((tc, D), lambda i: (i, 0)),
            pl.BlockSpec((tc, TOP_K), lambda i: (i, 0)),
            pl.BlockSpec((1, N_MOD, D), lambda i: (i // per_b, 0, 0)),
            pl.BlockSpec(memory_space=pl.ANY),
        ],
        out_specs=pl.BlockSpec((tc, D), lambda i: (i, 0)),
        out_shape=jax.ShapeDtypeStruct((T, D), F32),
        scratch_shapes=[pltpu.VMEM((2, TOP_K, tc * SUBLANES, LANES), F32),
                        pltpu.SemaphoreType.DMA((2,))],
        compiler_params=_cparams(("arbitrary",)),
        name="combine",
    )(dest_flat, dest_flat, x1, gates_t, mod3, ys)


def _rope_tables(L, dim, lane_off, width):
    rows = L // GRID_W
    nf = dim // 4
    inv = jnp.power(ROPE_BASE, -jnp.arange(nf, dtype=F32) / nf)
    row = jnp.repeat(jnp.arange(rows, dtype=F32), GRID_W)
    col = jnp.tile(jnp.arange(GRID_W, dtype=F32), rows)
    pos = jnp.stack([row, col], axis=-1)
    ang = pos[:, :, None] * inv
    ang = jnp.broadcast_to(ang[:, :, None, :], (L, 2, 2, nf)).reshape(L, dim)
    sign = jnp.where((jnp.arange(dim) % (dim // 2)) < nf, -1.0, 1.0).astype(F32)
    cos, sin = jnp.cos(ang), jnp.sin(ang) * sign
    if lane_off is None:
        reps = width // dim
        return jnp.tile(cos, (1, reps)), jnp.tile(sin, (1, reps))
    cfull = jnp.ones((L, width), F32).at[:, lane_off:lane_off + dim].set(cos)
    sfull = jnp.zeros((L, width), F32).at[:, lane_off:lane_off + dim].set(sin)
    return cfull, sfull


def _identity_tables(L):
    return jnp.ones((L, LANES), F32), jnp.zeros((L, LANES), F32)


def _half_rot_src(dim):
    j = jnp.arange(dim)
    return jnp.where((j % (dim // 2)) < dim // 4, j + dim // 4, j - dim // 4)


def _prep_weights(w_in, w_q_up, w_kv_up, g_q_head, g_k_head):
    D = w_in.shape[0]
    o = 0
    wq = w_in[:, o:o + Q_LORA]; o += Q_LORA
    wkv = w_in[:, o:o + KV_LORA]; o += KV_LORA
    wpe = w_in[:, o:o + MLA_ROPE]; o += MLA_ROPE
    n_qk = RET_HEADS * RET_DK
    wrq = w_in[:, o:o + n_qk]; o += n_qk
    wrk = w_in[:, o:o + n_qk]; o += n_qk
    rest = w_in[:, o:]
    src_m = _half_rot_src(MLA_ROPE)
    src_r = _half_rot_src(RET_DK)
    rope_lanes = slice(MLA_NOPE, MLA_NOPE + MLA_ROPE)
    pe_blk = jnp.zeros((D, LANES), w_in.dtype).at[:, rope_lanes].set(wpe)
    pe_perm = jnp.zeros((D, LANES), w_in.dtype).at[:, rope_lanes].set(wpe[:, src_m])
    perm_heads = lambda w: w.reshape(D, RET_HEADS, RET_DK)[:, :, src_r].reshape(D, n_qk)
    w_in_r = jnp.concatenate([wq, wkv, pe_blk, pe_perm, wrq, perm_heads(wrq), wrk, perm_heads(wrk),
                              rest], axis=1).astype(BF16)

    pad_h = LANES - MLA_QK
    wq3 = w_q_up.reshape(Q_LORA, MLA_HEADS, MLA_QK)
    q_main = jnp.pad(wq3, ((0, 0), (0, 0), (0, pad_h)))
    q_perm = jnp.zeros_like(q_main).at[:, :, rope_lanes].set(wq3[:, :, MLA_NOPE:][:, :, src_m])
    w_q_r = jnp.concatenate([q_main.reshape(Q_LORA, -1), q_perm.reshape(Q_LORA, -1)], axis=1).astype(BF16)

    kv = w_kv_up.reshape(KV_LORA, MLA_HEADS, MLA_NOPE + MLA_V)
    kpart = jnp.pad(kv[:, :, :MLA_NOPE], ((0, 0), (0, 0), (0, LANES - MLA_NOPE)))
    vpart = jnp.pad(kv[:, :, MLA_NOPE:], ((0, 0), (0, 0), (0, LANES - MLA_V)))
    w_kv_r = jnp.concatenate([kpart.reshape(KV_LORA, -1), vpart.reshape(KV_LORA, -1)], axis=1).astype(BF16)

    def gains(g):
        main = jnp.pad(g, (0, pad_h))[None, :]
        perm = jnp.zeros((1, LANES), g.dtype).at[0, rope_lanes].set(g[MLA_NOPE:][src_m])
        return main, perm

    gqh, gqr = gains(g_q_head)
    gkh, gkr = gains(g_k_head)
    return w_in_r, w_q_r, w_kv_r, gqh, gqr, gkh, gkr


def _routing_tables(idx, rank, counts, tm, blk):
    T = idx.shape[1]
    i32 = jnp.int32
    tot = jnp.sum(counts, axis=0)
    end = jnp.cumsum(tot)
    start = end - tot
    tile_base = start[None, :] + jnp.cumsum(counts, axis=0) - counts
    base_tok = jnp.repeat(tile_base, tm, axis=0)
    hit = idx[:, :, None] == jnp.arange(N_EXPERTS, dtype=i32)
    dest = jnp.sum(jnp.where(hit, base_tok[None], 0), axis=-1) + rank

    n_work = (T * TOP_K) // blk + N_EXPERTS
    first_blk = start // blk
    last_blk = (end - 1) // blk
    per_e = jnp.where(tot > 0, last_blk - first_blk + 1, 0)
    item_end = jnp.cumsum(per_e)
    item_start = item_end - per_e
    n_items = item_end[-1]
    j = jnp.minimum(jnp.arange(n_work, dtype=i32), n_items - 1)
    item_e = jnp.minimum(jnp.sum(item_end[None, :] <= j[:, None], axis=1), N_EXPERTS - 1).astype(i32)
    item_blk = first_blk[item_e] + j - item_start[item_e]
    item_lo = jnp.clip(start[item_e] - item_blk * blk, 0, blk)
    item_hi = jnp.clip(end[item_e] - item_blk * blk, 0, blk)
    prev_blk = jnp.concatenate([jnp.full((1,), -1, i32), item_blk[:-1].astype(i32)])
    item_first = (item_blk != prev_blk).astype(i32)
    items = (item_e, item_blk.astype(i32), item_lo.astype(i32), item_hi.astype(i32), item_first,
             n_items.astype(i32).reshape(1))
    return dest.astype(i32), items


def kernel(x, c, ctx, c_ctx, g_attn, g_ffn, w_ada, b_ada, w_in, g_q_lora, w_q_up, g_q_head,
           g_kv_lora, w_kv_up, g_k_head, ret_decay_logit, g_ret_out, w_out, w_router, b_router,
           w_mlp1, b_mlp1, w_mlp2, b_mlp2):
    B, L, D = x.shape
    Lc = ctx.shape[1]
    T = B * L
    l = 0
    assert w_ada.shape[0] == 1

    rows = ((B + 1 + SUBLANES - 1) // SUBLANES) * SUBLANES
    cc = jnp.zeros((rows, D), F32).at[:B].set(c).at[B].set(c_ctx)
    mod3 = _adaln(cc, w_ada[l], b_ada[l][None, :]).reshape(rows, N_MOD, D)

    w_in_r, w_q_r, w_kv_r, gqh, gqr, gkh, gkr = _prep_weights(
        w_in[l], w_q_up[l], w_kv_up[l], g_q_head[l], g_k_head[l])
    tabs_x = _rope_tables(L, MLA_ROPE, MLA_NOPE, LANES) + _rope_tables(L, RET_DK, None, LANES)
    tabs_c = _identity_tables(Lc) + _identity_tables(Lc)
    proj = functools.partial(_in_proj, g_attn=g_attn[l][None, :], w_in_r=w_in_r,
                             g_q_lora=g_q_lora[l][None, :], w_q_r=w_q_r, gqh=gqh, gqr=gqr,
                             g_kv_lora=g_kv_lora[l][None, :], w_kv_r=w_kv_r, gkh=gkh, gkr=gkr)
    q, kx, vx, rq, rk, rv, rg = proj(x, mod3, lambda b: b, tabs=tabs_x, tm=min(PROJ_TM, L))
    _, kc, vc, _, rk_c, rv_c, _ = proj(ctx, mod3, lambda b: B, tabs=tabs_c, tm=Lc)

    mla = _attention(q, kx, vx, kc, vc, min(ATT_TQ, L), min(ATT_TK, L // 2))

    log_gamma = jax.nn.log_sigmoid(ret_decay_logit[l].astype(F32))
    ret = _retention(log_gamma, g_ret_out[l][None, :], rq, rk, rv, rg, rk_c, rv_c, min(RET_C, L))

    w_r = jnp.pad(w_router[l], ((0, 0), (0, LANES - N_EXPERTS)))
    b_r = jnp.pad(b_router[l], (0, LANES - N_EXPERTS))[None, :]
    tm = min(OUT_TM, L)
    x1, hf8, idx, rank, gates, cnt = _out_router(
        mla.reshape(T, -1), ret.reshape(T, -1), x.reshape(T, D), mod3,
        w_out[l].astype(BF16), g_ffn[l][None, :], w_r, b_r, L, tm)

    blk = MOE_BLK
    assert (T * TOP_K) % blk == 0
    dest, items = _routing_tables(idx, rank, cnt[:, :, 0], tm, blk)
    dest_flat = dest.T.reshape(-1)

    xs = _dispatch(dest_flat, hf8, T * TOP_K, min(DISP_T, L))

    w1g, w1l = _split_glu(w_mlp1[l])
    b1g = b_mlp1[l][:, None, 0::2]
    b1l = b_mlp1[l][:, None, 1::2]
    ys = _experts(items, xs, w1g, w1l, w_mlp2[l].astype(BF16),
                  b1g, b1l, b_mlp2[l][:, None, :], blk)

    out = _combine(dest_flat, x1, gates.T, mod3, ys, L, min(COMB_T, L))
    return out.reshape(B, L, D)
```

```python
import functools
import math

import jax
import jax.numpy as jnp
from jax import lax
from jax.experimental import pallas as pl
from jax.experimental.pallas import tpu as pltpu

F32 = jnp.float32
BF16 = jnp.bfloat16

LANES = 128
SUBLANES = 8
VMEM_LIMIT_BYTES = 56 * 1024 * 1024

EPS = 1e-6
ROPE_BASE = 10000.0
GRID_W = 64
N_MOD = 6
MLA_HEADS = 8
MLA_NOPE = 64
MLA_ROPE = 32
MLA_QK = MLA_NOPE + MLA_ROPE
MLA_V = 64
Q_LORA = 256
KV_LORA = 128
RET_HEADS = 4
RET_DK = 64
RET_DV = 128
N_EXPERTS = 32
TOP_K = 4
SWIGLU_LIMIT = 7.0
SWIGLU_ALPHA = 1.702
LOG2E = 1.4426950408889634

PROJ_TM = 512
ATT_TQ = 512
ATT_TK = 1024
RET_C = 256
OUT_TM = 512
MOE_BLK = 512
DISP_T = 1024
COMB_T = 256
ISSUE_UNROLL = 8


def _cparams(sem):
    return pltpu.CompilerParams(dimension_semantics=sem, vmem_limit_bytes=VMEM_LIMIT_BYTES)


def _sigmoid(x):
    return 1.0 / (1.0 + jnp.exp(-x))


def _adaln_kernel(c_ref, w_ref, b_ref, o_ref):
    c = c_ref[...]
    s = (c * _sigmoid(c)).astype(BF16)
    o_ref[...] = jnp.dot(s, w_ref[...].astype(BF16), preferred_element_type=F32) + b_ref[...]


def _adaln(cc, w_ada, b_ada):
    rows, d = cc.shape
    n = w_ada.shape[1]
    tn = 1536
    return pl.pallas_call(
        _adaln_kernel,
        grid=(n // tn,),
        in_specs=[pl.BlockSpec((rows, d), lambda j: (0, 0)),
                  pl.BlockSpec((d, tn), lambda j: (0, j)),
                  pl.BlockSpec((1, tn), lambda j: (0, j))],
        out_specs=pl.BlockSpec((rows, tn), lambda j: (0, j)),
        out_shape=jax.ShapeDtypeStruct((rows, n), F32),
        compiler_params=_cparams(("arbitrary",)),
        name="adaln",
    )(cc, w_ada, b_ada)


def _in_proj_kernel(x_ref, mod_ref, gattn_ref, win_ref, gql_ref, wq_ref, gqh_ref, gqr_ref,
                    gkvl_ref, wkv_ref, gkh_ref, gkr_ref, cm_ref, sm_ref, cr_ref, sr_ref,
                    q_ref, k_ref, v_ref, rq_ref, rk_ref, rv_ref, rg_ref, *, q_scale):
    x = x_ref[0]
    shift = mod_ref[0, 0:1, :]
    scale = mod_ref[0, 1:2, :]
    ms = jnp.mean(x * x, axis=-1, keepdims=True)
    h = x * lax.rsqrt(ms + EPS) * gattn_ref[...]
    h = h * (1.0 + scale) + shift
    p = jnp.dot(h.astype(BF16), win_ref[...], preferred_element_type=F32)

    lane = lax.broadcasted_iota(jnp.int32, (1, LANES), 1)
    cm, sm = cm_ref[...], sm_ref[...]
    cr, sr = cr_ref[...], sr_ref[...]
    n_hl = MLA_HEADS * LANES

    cq = p[:, 0:Q_LORA]
    cq = cq * lax.rsqrt(jnp.mean(cq * cq, axis=-1, keepdims=True) + EPS) * gql_ref[...]
    qf = jnp.dot(cq.astype(BF16), wq_ref[...], preferred_element_type=F32)
    gq_cos = gqh_ref[...] * cm
    gq_sin = gqr_ref[...] * sm
    for hd in range(MLA_HEADS):
        blk = qf[:, hd * LANES:(hd + 1) * LANES]
        perm = qf[:, n_hl + hd * LANES:n_hl + (hd + 1) * LANES]
        r = lax.rsqrt(jnp.sum(blk * blk, axis=-1, keepdims=True) * (1.0 / MLA_QK) + EPS)
        q_ref[0, hd] = ((blk * gq_cos + perm * gq_sin) * (r * q_scale)).astype(BF16)

    o_kv = Q_LORA
    ckv = p[:, o_kv:o_kv + KV_LORA]
    ckv = ckv * lax.rsqrt(jnp.mean(ckv * ckv, axis=-1, keepdims=True) + EPS) * gkvl_ref[...]
    kvf = jnp.dot(ckv.astype(BF16), wkv_ref[...], preferred_element_type=F32)
    o_pe = o_kv + KV_LORA
    pe = p[:, o_pe:o_pe + LANES]
    pe_perm = p[:, o_pe + LANES:o_pe + 2 * LANES]
    gk = gkh_ref[...]
    pe_rope = pe * (gk * cm) + pe_perm * (gkr_ref[...] * sm)
    pe_ss = jnp.sum(pe * pe, axis=-1, keepdims=True)
    ones_col = (lane == MLA_V).astype(F32)
    for hd in range(MLA_HEADS):
        kn = kvf[:, hd * LANES:(hd + 1) * LANES]
        ss = jnp.sum(kn * kn, axis=-1, keepdims=True) + pe_ss
        r = lax.rsqrt(ss * (1.0 / MLA_QK) + EPS)
        kk = (kn * gk + pe_rope) * r
        k_ref[0, hd] = kk.T.astype(BF16)
        vv = kvf[:, (MLA_HEADS + hd) * LANES:(MLA_HEADS + hd + 1) * LANES] + ones_col
        v_ref[0, hd] = vv.astype(BF16)

    n_qk = RET_HEADS * RET_DK
    o_rq = o_pe + 2 * LANES
    o_rk = o_rq + 2 * n_qk
    for j in range(n_qk // LANES):
        js = slice(j * LANES, (j + 1) * LANES)
        a = p[:, o_rq + j * LANES:o_rq + (j + 1) * LANES]
        b = p[:, o_rq + n_qk + j * LANES:o_rq + n_qk + (j + 1) * LANES]
        rq_ref[0, :, js] = (a * cr + b * sr).astype(BF16)
        a = p[:, o_rk + j * LANES:o_rk + (j + 1) * LANES]
        b = p[:, o_rk + n_qk + j * LANES:o_rk + n_qk + (j + 1) * LANES]
        rk_ref[0, :, js] = ((a * cr + b * sr) * (RET_DK ** -0.5)).astype(BF16)
    o_rv = o_rk + 2 * n_qk
    n_v = RET_HEADS * RET_DV
    rv_ref[0] = p[:, o_rv:o_rv + n_v].astype(BF16)
    rg_ref[0] = p[:, o_rv + n_v:o_rv + 2 * n_v]


def _in_proj(x, mod3, mod_row_of_batch, g_attn, w_in_r, g_q_lora, w_q_r, gqh, gqr, g_kv_lora,
             w_kv_r, gkh, gkr, tabs, tm):
    B, L, D = x.shape
    cm, sm, cr, sr = tabs
    n_in = w_in_r.shape[1]
    const = lambda b, i: (0, 0)
    tab_spec = pl.BlockSpec((tm, LANES), lambda b, i: (i, 0))
    head_spec = pl.BlockSpec((1, MLA_HEADS, tm, LANES), lambda b, i: (b, 0, i, 0))
    n_qk = RET_HEADS * RET_DK
    n_v = RET_HEADS * RET_DV
    seq_spec = lambda w: pl.BlockSpec((1, tm, w), lambda b, i: (b, i, 0))
    head_shape = jax.ShapeDtypeStruct((B, MLA_HEADS, L, LANES), BF16)
    q_scale = MLA_QK ** -0.5 * LOG2E
    return pl.pallas_call(
        functools.partial(_in_proj_kernel, q_scale=q_scale),
        grid=(B, L // tm),
        in_specs=[
            pl.BlockSpec((1, tm, D), lambda b, i: (b, i, 0)),
            pl.BlockSpec((1, N_MOD, D), lambda b, i: (mod_row_of_batch(b), 0, 0)),
            pl.BlockSpec((1, D), const),
            pl.BlockSpec((D, n_in), const),
            pl.BlockSpec((1, Q_LORA), const),
            pl.BlockSpec(w_q_r.shape, const),
            pl.BlockSpec((1, LANES), const),
            pl.BlockSpec((1, LANES), const),
            pl.BlockSpec((1, KV_LORA), const),
            pl.BlockSpec(w_kv_r.shape, const),
            pl.BlockSpec((1, LANES), const),
            pl.BlockSpec((1, LANES), const),
            tab_spec, tab_spec, tab_spec, tab_spec,
        ],
        out_specs=[head_spec,
                   pl.BlockSpec((1, MLA_HEADS, LANES, tm), lambda b, i: (b, 0, 0, i)),
                   head_spec,
                   seq_spec(n_qk), seq_spec(n_qk), seq_spec(n_v), seq_spec(n_v)],
        out_shape=[head_shape,
                   jax.ShapeDtypeStruct((B, MLA_HEADS, LANES, L), BF16),
                   head_shape,
                   jax.ShapeDtypeStruct((B, L, n_qk), BF16),
                   jax.ShapeDtypeStruct((B, L, n_qk), BF16),
                   jax.ShapeDtypeStruct((B, L, n_v), BF16),
                   jax.ShapeDtypeStruct((B, L, n_v), F32)],
        compiler_params=_cparams(("arbitrary", "arbitrary")),
        name="in_proj",
    )(x, mod3, g_attn, w_in_r, g_q_lora, w_q_r, gqh, gqr, g_kv_lora, w_kv_r, gkh, gkr,
      cm, sm, cr, sr)


def _attn_kernel(q_ref, kx_ref, vx_ref, kc_ref, vc_ref, o_ref, m_ref, acc_ref, s_ref, *, tk):
    n_kv = kx_ref.shape[3] // tk
    n_heads = q_ref.shape[1]

    def scores(hh, start, slot):
        kb = kx_ref[0, hh, :, pl.ds(start, tk)]
        s_ref[hh, slot] = jnp.dot(q_ref[0, hh], kb, preferred_element_type=F32)

    def consume(hh, start, slot):
        s = s_ref[hh, slot]
        m_old = m_ref[hh]
        m_new = jnp.maximum(m_old, jnp.max(s, axis=-1, keepdims=True))
        alpha = jnp.exp2(m_old - m_new)
        pr = jnp.exp2(s - m_new)
        vb = vx_ref[0, hh, pl.ds(start, tk), :]
        acc_ref[hh] = alpha * acc_ref[hh] + jnp.dot(pr.astype(BF16), vb,
                                                    preferred_element_type=F32)
        m_ref[hh] = m_new

    def step(j, slot, prefetch):
        if prefetch:
            nxt = pl.multiple_of((j + 1) * tk, tk)
            for hh in range(n_heads):
                scores(hh, nxt, 1 - slot)
        cur = pl.multiple_of(j * tk, tk)
        for hh in range(n_heads):
            consume(hh, cur, slot)

    for hh in range(n_heads):
        scores(hh, 0, 0)
        s = jnp.dot(q_ref[0, hh], kc_ref[0, hh], preferred_element_type=F32)
        m0 = jnp.max(s, axis=-1, keepdims=True)
        m_ref[hh] = m0
        acc_ref[hh] = jnp.dot(jnp.exp2(s - m0).astype(BF16), vc_ref[0, hh],
                              preferred_element_type=F32)

    def body(jj, carry):
        step(2 * jj, 0, True)
        step(2 * jj + 1, 1, True)
        return carry

    lax.fori_loop(0, n_kv // 2 - 1, body, 0)
    step(n_kv - 2, 0, True)
    step(n_kv - 1, 1, False)
    outs = []
    for hh in range(n_heads):
        acc = acc_ref[hh]
        outs.append(acc[:, :MLA_V] / acc[:, MLA_V:MLA_V + 1])
    o_ref[0] = jnp.concatenate(outs, axis=-1).astype(BF16)


def _attention(q, kx, vx, kc, vc, tq, tk):
    B, H, L, _ = q.shape
    Lc = kc.shape[3]
    return pl.pallas_call(
        functools.partial(_attn_kernel, tk=tk),
        grid=(B, H // 2, L // tq),
        in_specs=[
            pl.BlockSpec((1, 2, tq, LANES), lambda b, h, i: (b, h, i, 0)),
            pl.BlockSpec((1, 2, LANES, L), lambda b, h, i: (b, h, 0, 0)),
            pl.BlockSpec((1, 2, L, LANES), lambda b, h, i: (b, h, 0, 0)),
            pl.BlockSpec((1, 2, LANES, Lc), lambda b, h, i: (b, h, 0, 0)),
            pl.BlockSpec((1, 2, Lc, LANES), lambda b, h, i: (b, h, 0, 0)),
        ],
        out_specs=pl.BlockSpec((1, tq, LANES), lambda b, h, i: (b, i, h)),
        out_shape=jax.ShapeDtypeStruct((B, L, H * MLA_V), BF16),
        scratch_shapes=[pltpu.VMEM((2, tq, 1), F32), pltpu.VMEM((2, tq, LANES), F32),
                        pltpu.VMEM((2, 2, tq, tk), F32)],
        compiler_params=_cparams(("arbitrary", "arbitrary", "arbitrary")),
        name="attention",
    )(q, kx, vx, kc, vc)


def _ret_kernel(lg_ref, lgf_ref, lgb_ref, lgvf_ref, lgvb_ref, gout_ref, rq_ref, rk_ref, rv_ref, rg_ref,
                kc_ref, vc_ref, o_ref, dm_ref, qdf_ref, qdb_ref, kdf_ref, kdb_ref,
                f_ref, r_ref, rs_ref, *, n_chunks):
    C = rq_ref.shape[1]
    Lc = kc_ref.shape[1]
    ps = pl.program_id(1)
    i = pl.program_id(2)
    n_pairs = RET_HEADS // 2
    pw = 2 * RET_DK
    vw = 2 * RET_DV
    tdn = (((0,), (0,)), ((), ()))
    ndn = (((1,), (1,)), ((), ()))
    lgf = lgf_ref[...]
    lgb = lgb_ref[...]

    @pl.when((pl.program_id(0) == 0) & (ps == 0) & (i == 0))
    def _tables():
        a = lax.broadcasted_iota(jnp.int32, (C, C), 0)
        b = lax.broadcasted_iota(jnp.int32, (C, C), 1)
        dab = (a - b).astype(F32)
        for hd in range(RET_HEADS):
            fwd = jnp.where(a >= b, jnp.exp(jnp.where(a >= b, dab, 0.0) * lg_ref[0, hd]), 0.0)
            bwd = jnp.where(b >= a, jnp.exp(jnp.where(b >= a, -dab, 0.0) * lg_ref[1, hd]), 0.0)
            dm_ref[hd] = fwd + bwd
        row = lax.broadcasted_iota(jnp.int32, (C, 1), 0).astype(F32)
        qdf_ref[...] = jnp.exp((row + 1.0) * lgf)
        qdb_ref[...] = jnp.exp((C - row) * lgb)
        kdf_ref[...] = jnp.exp((C - 1.0 - row) * lgf)
        kdb_ref[...] = jnp.exp(row * lgb)

    @pl.when((ps == 0) & (i == 0))
    def _init_states():
        rowc = lax.broadcasted_iota(jnp.int32, (Lc, 1), 0).astype(F32)
        wf = jnp.exp((Lc - 1.0 - rowc) * lgf)
        wb = jnp.exp(rowc * lgb)
        kc = kc_ref[0].astype(F32)
        for pr in range(n_pairs):
            kp = kc[:, pr * pw:(pr + 1) * pw]
            vp = vc_ref[0, :, pr * vw:(pr + 1) * vw]
            f_ref[pr] = lax.dot_general((kp * wf[:, pr * pw:(pr + 1) * pw]).astype(BF16), vp, tdn,
                                        preferred_element_type=F32)
            r_ref[pr] = lax.dot_general((kp * wb[:, pr * pw:(pr + 1) * pw]).astype(BF16), vp, tdn,
                                        preferred_element_type=F32)

    @pl.when(ps == 0)
    def _backward_states():
        c = n_chunks - 1 - i
        k = rk_ref[0].astype(F32)
        cdb = jnp.exp(C * lgvb_ref[...])
        for pr in range(n_pairs):
            r_old = r_ref[pr]
            rs_ref[c, pr] = r_old.astype(BF16)
            kp = (k[:, pr * pw:(pr + 1) * pw] * kdb_ref[:, pr * pw:(pr + 1) * pw]).astype(BF16)
            vp = rv_ref[0, :, pr * vw:(pr + 1) * vw]
            upd = lax.dot_general(kp, vp, tdn, preferred_element_type=F32)
            r_ref[pr] = r_old * cdb[:, pr * vw:(pr + 1) * vw] + upd

    @pl.when(ps == 1)
    def _forward_outputs():
        q = rq_ref[0].astype(F32)
        k = rk_ref[0].astype(F32)
        cdf = jnp.exp(C * lgvf_ref[...])
        lane = lax.broadcasted_iota(jnp.int32, (1, pw), 1)
        for pr in range(n_pairs):
            sl = slice(pr * pw, (pr + 1) * pw)
            qp = q[:, sl]
            kpb = rk_ref[0, :, sl]
            qf = qp * qdf_ref[:, sl]
            qb = qp * qdb_ref[:, sl]
            fb = f_ref[pr].astype(BF16)
            rb = rs_ref[i, pr]
            for hh in range(2):
                hd = 2 * pr + hh
                hm = (lane // RET_DK) == hh
                vs = slice(hd * RET_DV, (hd + 1) * RET_DV)
                fs = slice(hh * RET_DV, (hh + 1) * RET_DV)
                a = lax.dot_general(jnp.where(hm, qp, 0.0).astype(BF16), kpb, ndn,
                                    preferred_element_type=F32)
                a = (a * dm_ref[hd]).astype(BF16)
                o = jnp.dot(a, rv_ref[0, :, vs], preferred_element_type=F32)
                o = o + jnp.dot(jnp.where(hm, qf, 0.0).astype(BF16), fb[:, fs],
                                preferred_element_type=F32)
                o = o + jnp.dot(jnp.where(hm, qb, 0.0).astype(BF16), rb[:, fs],
                                preferred_element_type=F32)
                o = o * lax.rsqrt(jnp.mean(o * o, axis=-1, keepdims=True) + EPS) * gout_ref[:, vs]
                g = rg_ref[0, :, vs]
                o_ref[0, :, vs] = (o * (g * _sigmoid(g))).astype(BF16)
            kp = (k[:, sl] * kdf_ref[:, sl]).astype(BF16)
            vp = rv_ref[0, :, pr * vw:(pr + 1) * vw]
            upd = lax.dot_general(kp, vp, tdn, preferred_element_type=F32)
            f_ref[pr] = f_ref[pr] * cdf[:, pr * vw:(pr + 1) * vw] + upd


def _retention(log_gamma, g_ret_out, rq, rk, rv, rg, rk_c, rv_c, C):
    B, L, n_qk = rq.shape
    n_v = rv.shape[2]
    Lc = rk_c.shape[1]
    n = L // C
    lgf = jnp.repeat(log_gamma[0], RET_DK)[None, :]
    lgb = jnp.repeat(log_gamma[1], RET_DK)[None, :]
    lgvf = jnp.repeat(log_gamma[0], RET_DV)[None, :]
    lgvb = jnp.repeat(log_gamma[1], RET_DV)[None, :]
    chunk = lambda p, i: jnp.where(p == 0, n - 1 - i, i)
    fwd_only = lambda p, i: jnp.where(p == 0, 0, i)
    const2 = lambda b, p, i: (0, 0)
    n_pairs = RET_HEADS // 2
    return pl.pallas_call(
        functools.partial(_ret_kernel, n_chunks=n),
        grid=(B, 2, n),
        in_specs=[
            pl.BlockSpec(memory_space=pltpu.SMEM),
            pl.BlockSpec((1, n_qk), const2),
            pl.BlockSpec((1, n_qk), const2),
            pl.BlockSpec((1, n_v), const2),
            pl.BlockSpec((1, n_v), const2),
            pl.BlockSpec((1, n_v), const2),
            pl.BlockSpec((1, C, n_qk), lambda b, p, i: (b, fwd_only(p, i), 0)),
            pl.BlockSpec((1, C, n_qk), lambda b, p, i: (b, chunk(p, i), 0)),
            pl.BlockSpec((1, C, n_v), lambda b, p, i: (b, chunk(p, i), 0)),
            pl.BlockSpec((1, C, n_v), lambda b, p, i: (b, fwd_only(p, i), 0)),
            pl.BlockSpec((1, Lc, n_qk), lambda b, p, i: (b, 0, 0)),
            pl.BlockSpec((1, Lc, n_v), lambda b, p, i: (b, 0, 0)),
        ],
        out_specs=pl.BlockSpec((1, C, n_v), lambda b, p, i: (b, fwd_only(p, i), 0)),
        out_shape=jax.ShapeDtypeStruct((B, L, n_v), BF16),
        scratch_shapes=[
            pltpu.VMEM((RET_HEADS, C, C), F32),
            pltpu.VMEM((C, n_qk), F32), pltpu.VMEM((C, n_qk), F32),
            pltpu.VMEM((C, n_qk), F32), pltpu.VMEM((C, n_qk), F32),
            pltpu.VMEM((n_pairs, 2 * RET_DK, 2 * RET_DV), F32),
            pltpu.VMEM((n_pairs, 2 * RET_DK, 2 * RET_DV), F32),
            pltpu.VMEM((n, n_pairs, 2 * RET_DK, 2 * RET_DV), BF16),
        ],
        compiler_params=_cparams(("arbitrary", "arbitrary", "arbitrary")),
        name="retention",
    )(log_gamma, lgf, lgb, lgvf, lgvb, g_ret_out, rq, rk, rv, rg, rk_c, rv_c)


def _out_router_kernel(mla_ref, ret_ref, x_ref, mod_ref, wo_ref, gffn_ref, wr_ref, br_ref,
                       x1_ref, hf_ref, idx_ref, rank_ref, gate_ref, cnt_ref):
    tm = x_ref.shape[0]
    n_mla = mla_ref.shape[1]
    y = jnp.dot(mla_ref[...], wo_ref[0:n_mla, :], preferred_element_type=F32)
    y = y + jnp.dot(ret_ref[...], wo_ref[n_mla:, :], preferred_element_type=F32)
    x1 = x_ref[...] + mod_ref[0, 2:3, :] * y
    x1_ref[...] = x1
    ms = jnp.mean(x1 * x1, axis=-1, keepdims=True)
    hf = x1 * lax.rsqrt(ms + EPS) * gffn_ref[...]
    hf = hf * (1.0 + mod_ref[0, 4:5, :]) + mod_ref[0, 3:4, :]
    for c in range(hf.shape[1] // LANES):
        hf_ref[pl.ds(c, tm, stride=SUBLANES), :] = hf[:, c * LANES:(c + 1) * LANES]

    hi = hf.astype(BF16)
    lo = (hf - hi.astype(F32)).astype(BF16)
    both = jnp.dot(hi, wr_ref[...], preferred_element_type=F32)
    logits = (both[:, :LANES] + both[:, LANES:]
              + jnp.dot(lo, wr_ref[:, :LANES], preferred_element_type=F32)) + br_ref[...]
    lt = logits.T[0:N_EXPERTS, :]
    e_iota = lax.broadcasted_iota(jnp.int32, (N_EXPERTS, tm), 0).astype(F32)
    vals, idxs, hits = [], [], []
    for _ in range(TOP_K):
        mx = jnp.max(lt, axis=0, keepdims=True)
        ix = jnp.min(jnp.where(lt == mx, e_iota, float(N_EXPERTS)), axis=0, keepdims=True)
        hit = e_iota == ix
        lt = jnp.where(hit, -jnp.inf, lt)
        vals.append(mx)
        idxs.append(ix)
        hits.append(hit.astype(F32))

    s_iota = lax.broadcasted_iota(jnp.int32, (tm, tm), 0)
    t_iota = lax.broadcasted_iota(jnp.int32, (tm, tm), 1)
    upper = (s_iota <= t_iota).astype(F32).astype(BF16)
    prefix = jnp.dot(jnp.concatenate(hits, axis=0).astype(BF16), upper,
                     preferred_element_type=F32)
    ranks = []
    seen = jnp.zeros((N_EXPERTS, 1), F32)
    for k in range(TOP_K):
        pk = prefix[k * N_EXPERTS:(k + 1) * N_EXPERTS, :]
        rank = jnp.sum(hits[k] * (pk - 1.0 + seen), axis=0, keepdims=True)
        seen = seen + jnp.sum(hits[k], axis=1, keepdims=True)
        ranks.append(rank.astype(jnp.int32))
    ex = [jnp.exp(v - vals[0]) for v in vals]
    den = ex[0] + ex[1] + ex[2] + ex[3]
    idx_ref[...] = jnp.concatenate(idxs, axis=0).astype(jnp.int32)
    rank_ref[...] = jnp.concatenate(ranks, axis=0)
    gate_ref[...] = jnp.concatenate([e / den for e in ex], axis=0)
    cnt_ref[0] = jnp.broadcast_to(seen, (N_EXPERTS, LANES)).astype(jnp.int32)


def _out_router(mla, ret, x2, mod3, w_out_b, g_ffn, w_r, b_r, L, tm):
    T, D = x2.shape
    n_tiles = T // tm
    per_b = L // tm
    const = lambda i: (0, 0)
    return pl.pallas_call(
        _out_router_kernel,
        grid=(n_tiles,),
        in_specs=[
            pl.BlockSpec((tm, mla.shape[1]), lambda i: (i, 0)),
            pl.BlockSpec((tm, ret.shape[1]), lambda i: (i, 0)),
            pl.BlockSpec((tm, D), lambda i: (i, 0)),
            pl.BlockSpec((1, N_MOD, D), lambda i: (i // per_b, 0, 0)),
            pl.BlockSpec(w_out_b.shape, const),
            pl.BlockSpec((1, D), const),
            pl.BlockSpec(w_r.shape, const),
            pl.BlockSpec((1, LANES), const),
        ],
        out_specs=[
            pl.BlockSpec((tm, D), lambda i: (i, 0)),
            pl.BlockSpec((tm * SUBLANES, LANES), lambda i: (i, 0)),
            pl.BlockSpec((TOP_K, tm), lambda i: (0, i)),
            pl.BlockSpec((TOP_K, tm), lambda i: (0, i)),
            pl.BlockSpec((TOP_K, tm), lambda i: (0, i)),
            pl.BlockSpec((1, N_EXPERTS, LANES), lambda i: (i, 0, 0)),
        ],
        out_shape=[
            jax.ShapeDtypeStruct((T, D), F32),
            jax.ShapeDtypeStruct((T * SUBLANES, LANES), F32),
            jax.ShapeDtypeStruct((TOP_K, T), jnp.int32),
            jax.ShapeDtypeStruct((TOP_K, T), jnp.int32),
            jax.ShapeDtypeStruct((TOP_K, T), F32),
            jax.ShapeDtypeStruct((n_tiles, N_EXPERTS, LANES), jnp.int32),
        ],
        compiler_params=_cparams(("arbitrary",)),
        name="out_router",
    )(mla, ret, x2, mod3, w_out_b, g_ffn, w_r, b_r)


def _row_copy(src, dst, sem):
    return pltpu.make_async_copy(src, dst, sem)


def _dispatch_kernel(dest_ref, hf_ref, xs_ref, sem, *, td):
    def issue(t, carry):
        src = hf_ref.at[pl.ds(pl.multiple_of(t * SUBLANES, SUBLANES), SUBLANES), :]
        for k in range(TOP_K):
            d = dest_ref[t * TOP_K + k]
            dst = xs_ref.at[pl.ds(pl.multiple_of(d * SUBLANES, SUBLANES), SUBLANES), :]
            _row_copy(src, dst, sem).start(priority=k % 2)
        return carry

    lax.fori_loop(0, td, issue, 0, unroll=ISSUE_UNROLL)
    for k in range(TOP_K):
        _row_copy(hf_ref, xs_ref.at[pl.ds(0, td * SUBLANES), :], sem).wait()


def _dispatch(dest_flat, hf8, n_pad, td):
    T8, _ = hf8.shape
    T = T8 // SUBLANES
    return pl.pallas_call(
        functools.partial(_dispatch_kernel, td=td),
        grid=(T // td,),
        in_specs=[
            pl.BlockSpec((td * TOP_K,), lambda i: (i,), memory_space=pltpu.SMEM),
            pl.BlockSpec((td * SUBLANES, LANES), lambda i: (i, 0)),
        ],
        out_specs=pl.BlockSpec(memory_space=pl.ANY),
        out_shape=jax.ShapeDtypeStruct((n_pad * SUBLANES, LANES), F32),
        scratch_shapes=[pltpu.SemaphoreType.DMA],
        compiler_params=_cparams(("arbitrary",)),
        name="dispatch",
    )(dest_flat, hf8)


def _split_glu_kernel(w_ref, g_ref, l_ref):
    w = w_ref[0].astype(BF16)
    sub = 2 * LANES
    r = lax.broadcasted_iota(jnp.int32, (sub, sub), 0)
    c = lax.broadcasted_iota(jnp.int32, (sub, sub), 1)
    src_col = jnp.where(c < LANES, 2 * c, 2 * (c - LANES) + 1)
    sel = (r == src_col).astype(F32).astype(BF16)
    for s in range(w.shape[1] // sub):
        t = jnp.dot(w[:, s * sub:(s + 1) * sub], sel, preferred_element_type=F32).astype(BF16)
        g_ref[0, :, s * LANES:(s + 1) * LANES] = t[:, :LANES]
        l_ref[0, :, s * LANES:(s + 1) * LANES] = t[:, LANES:]


def _split_glu(w1):
    E, d, f2 = w1.shape
    tn = 1024
    half = jax.ShapeDtypeStruct((E, d, f2 // 2), BF16)
    return pl.pallas_call(
        _split_glu_kernel,
        grid=(E, f2 // tn),
        in_specs=[pl.BlockSpec((1, d, tn), lambda e, j: (e, 0, j))],
        out_specs=[pl.BlockSpec((1, d, tn // 2), lambda e, j: (e, 0, j)),
                   pl.BlockSpec((1, d, tn // 2), lambda e, j: (e, 0, j))],
        out_shape=[half, half],
        compiler_params=_cparams(("arbitrary", "arbitrary")),
        name="split_glu",
    )(w1)


def _experts_kernel(ie_ref, ib_ref, lo_ref, hi_ref, first_ref, ni_ref, xs_ref, w1g_ref, w1l_ref,
                    w2_ref, b1g_ref, b1l_ref, b2_ref, ys_ref):
    i = pl.program_id(0)
    blk = xs_ref.shape[0] // SUBLANES
    d = w1g_ref.shape[1]

    @pl.when(i < ni_ref[0])
    def _():
        cols = [xs_ref[pl.ds(c, blk, stride=SUBLANES), :] for c in range(d // LANES)]
        x = jnp.concatenate(cols, axis=-1)
        row = lax.broadcasted_iota(jnp.int32, (blk, 1), 0)
        mine = (row >= lo_ref[i]) & (row < hi_ref[i])
        x = jnp.where(mine, x, 0.0).astype(BF16)
        hg = jnp.dot(x, w1g_ref[0], preferred_element_type=F32) + b1g_ref[0]
        hl = jnp.dot(x, w1l_ref[0], preferred_element_type=F32) + b1l_ref[0]
        glu = jnp.minimum(hg, SWIGLU_LIMIT)
        lin = jnp.clip(hl, -SWIGLU_LIMIT, SWIGLU_LIMIT)
        act = glu * _sigmoid(SWIGLU_ALPHA * glu) * (lin + 1.0)
        y = jnp.dot(act.astype(BF16), w2_ref[0], preferred_element_type=F32) + b2_ref[0]
        y = jnp.where(mine, y, 0.0)

        @pl.when(first_ref[i] == 1)
        def _():
            for c in range(d // LANES):
                ys_ref[pl.ds(c, blk, stride=SUBLANES), :] = y[:, c * LANES:(c + 1) * LANES]

        @pl.when(first_ref[i] == 0)
        def _():
            for c in range(d // LANES):
                ys_ref[pl.ds(c, blk, stride=SUBLANES), :] += y[:, c * LANES:(c + 1) * LANES]


def _experts(items, xs, w1g, w1l, w2, b1g, b1l, b2, blk):
    item_e, item_blk, item_lo, item_hi, item_first, n_items = items
    d = w1g.shape[1]
    f = w1g.shape[2]
    row_map = lambda i, ie, ib, lo, hi, fi, ni: (ib[i], 0)
    exp_map = lambda i, ie, ib, lo, hi, fi, ni: (ie[i], 0, 0)
    grid_spec = pltpu.PrefetchScalarGridSpec(
        num_scalar_prefetch=6,
        grid=(item_e.shape[0],),
        in_specs=[
            pl.BlockSpec((blk * SUBLANES, LANES), row_map),
            pl.BlockSpec((1, d, f), exp_map),
            pl.BlockSpec((1, d, f), exp_map),
            pl.BlockSpec((1, f, d), exp_map),
            pl.BlockSpec((1, 1, f), exp_map),
            pl.BlockSpec((1, 1, f), exp_map),
            pl.BlockSpec((1, 1, d), exp_map),
        ],
        out_specs=pl.BlockSpec((blk * SUBLANES, LANES), row_map),
    )
    return pl.pallas_call(
        _experts_kernel,
        grid_spec=grid_spec,
        out_shape=jax.ShapeDtypeStruct(xs.shape, F32),
        compiler_params=_cparams(("arbitrary",)),
        name="experts",
    )(item_e, item_blk, item_lo, item_hi, item_first, n_items, xs, w1g, w1l, w2, b1g, b1l, b2)


def _combine_kernel(dest_ref, dnext_ref, x1_ref, gate_ref, mod_ref, ys_ref, o_ref, buf_ref, sems,
                    *, tc):
    i = pl.program_id(0)
    n = pl.num_programs(0)
    slot = i % 2

    def gather(idx_ref, s):
        def issue(t, carry):
            for k in range(TOP_K):
                d = idx_ref[t * TOP_K + k]
                src = ys_ref.at[pl.ds(pl.multiple_of(d * SUBLANES, SUBLANES), SUBLANES), :]
                dst = buf_ref.at[s, k, pl.ds(pl.multiple_of(t * SUBLANES, SUBLANES), SUBLANES), :]
                _row_copy(src, dst, sems.at[s]).start(priority=k % 2)
            return carry
        lax.fori_loop(0, tc, issue, 0, unroll=ISSUE_UNROLL)

    @pl.when(i == 0)
    def _():
        gather(dest_ref, 0)

    @pl.when(i + 1 < n)
    def _():
        gather(dnext_ref, 1 - slot)

    for k in range(TOP_K):
        _row_copy(ys_ref.at[pl.ds(0, tc * SUBLANES), :], buf_ref.at[slot, k], sems.at[slot]).wait()

    g = gate_ref[...]
    for c in range(o_ref.shape[1] // LANES):
        cs = slice(c * LANES, (c + 1) * LANES)
        acc = g[:, 0:1] * buf_ref[slot, 0, pl.ds(c, tc, stride=SUBLANES), :]
        for k in range(1, TOP_K):
            acc = acc + g[:, k:k + 1] * buf_ref[slot, k, pl.ds(c, tc, stride=SUBLANES), :]
        o_ref[:, cs] = x1_ref[:, cs] + mod_ref[0, 5:6, cs] * acc


def _combine(dest_flat, x1, gates_t, mod3, ys, L, tc):
    T, D = x1.shape
    per_b = L // tc
    n = T // tc
    return pl.pallas_call(
        functools.partial(_combine_kernel, tc=tc),
        grid=(n,),
        in_specs=[
            pl.BlockSpec((tc * TOP_K,), lambda i: (i,), memory_space=pltpu.SMEM),
            pl.BlockSpec((tc * TOP_K,), lambda i: (jnp.minimum(i + 1, n - 1),),
                         memory_space=pltpu.SMEM),
            pl.BlockSpec((tc, D), lambda i: (i, 0)),
            pl.BlockSpec((tc, TOP_K), lambda i: (i, 0)),
            pl.BlockSpec((1, N_MOD, D), lambda i: (i // per_b, 0, 0)),
            pl.BlockSpec(memory_space=pl.ANY),
        ],
        out_specs=pl.BlockSpec((tc, D), lambda i: (i, 0)),
        out_shape=jax.ShapeDtypeStruct((T, D), F32),
        scratch_shapes=[pltpu.VMEM((2, TOP_K, tc * SUBLANES, LANES), F32),
                        pltpu.SemaphoreType.DMA((2,))],
        compiler_params=_cparams(("arbitrary",)),
        name="combine",
    )(dest_flat, dest_flat, x1, gates_t, mod3, ys)


def _rope_tables(L, dim, lane_off, width):
    rows = L // GRID_W
    nf = dim // 4
    inv = jnp.power(ROPE_BASE, -jnp.arange(nf, dtype=F32) / nf)
    row = jnp.repeat(jnp.arange(rows, dtype=F32), GRID_W)
    col = jnp.tile(jnp.arange(GRID_W, dtype=F32), rows)
    pos = jnp.stack([row, col], axis=-1)
    ang = pos[:, :, None] * inv
    ang = jnp.broadcast_to(ang[:, :, None, :], (L, 2, 2, nf)).reshape(L, dim)
    sign = jnp.where((jnp.arange(dim) % (dim // 2)) < nf, -1.0, 1.0).astype(F32)
    cos, sin = jnp.cos(ang), jnp.sin(ang) * sign
    if lane_off is None:
        reps = width // dim
        return jnp.tile(cos, (1, reps)), jnp.tile(sin, (1, reps))
    cfull = jnp.ones((L, width), F32).at[:, lane_off:lane_off + dim].set(cos)
    sfull = jnp.zeros((L, width), F32).at[:, lane_off:lane_off + dim].set(sin)
    return cfull, sfull


def _identity_tables(L):
    return jnp.ones((L, LANES), F32), jnp.zeros((L, LANES), F32)


def _half_rot_src(dim):
    j = jnp.arange(dim)
    return jnp.where((j % (dim // 2)) < dim // 4, j + dim // 4, j - dim // 4)


def _prep_weights(w_in, w_q_up, w_kv_up, g_q_head, g_k_head):
    D = w_in.shape[0]
    o = 0
    wq = w_in[:, o:o + Q_LORA]; o += Q_LORA
    wkv = w_in[:, o:o + KV_LORA]; o += KV_LORA
    wpe = w_in[:, o:o + MLA_ROPE]; o += MLA_ROPE
    n_qk = RET_HEADS * RET_DK
    wrq = w_in[:, o:o + n_qk]; o += n_qk
    wrk = w_in[:, o:o + n_qk]; o += n_qk
    rest = w_in[:, o:]
    src_m = _half_rot_src(MLA_ROPE)
    src_r = _half_rot_src(RET_DK)
    rope_lanes = slice(MLA_NOPE, MLA_NOPE + MLA_ROPE)
    pe_blk = jnp.zeros((D, LANES), w_in.dtype).at[:, rope_lanes].set(wpe)
    pe_perm = jnp.zeros((D, LANES), w_in.dtype).at[:, rope_lanes].set(wpe[:, src_m])
    perm_heads = lambda w: w.reshape(D, RET_HEADS, RET_DK)[:, :, src_r].reshape(D, n_qk)
    w_in_r = jnp.concatenate([wq, wkv, pe_blk, pe_perm, wrq, perm_heads(wrq), wrk, perm_heads(wrk),
                              rest], axis=1).astype(BF16)

    pad_h = LANES - MLA_QK
    wq3 = w_q_up.reshape(Q_LORA, MLA_HEADS, MLA_QK)
    q_main = jnp.pad(wq3, ((0, 0), (0, 0), (0, pad_h)))
    q_perm = jnp.zeros_like(q_main).at[:, :, rope_lanes].set(wq3[:, :, MLA_NOPE:][:, :, src_m])
    w_q_r = jnp.concatenate([q_main.reshape(Q_LORA, -1), q_perm.reshape(Q_LORA, -1)], axis=1).astype(BF16)

    kv = w_kv_up.reshape(KV_LORA, MLA_HEADS, MLA_NOPE + MLA_V)
    kpart = jnp.pad(kv[:, :, :MLA_NOPE], ((0, 0), (0, 0), (0, LANES - MLA_NOPE)))
    vpart = jnp.pad(kv[:, :, MLA_NOPE:], ((0, 0), (0, 0), (0, LANES - MLA_V)))
    w_kv_r = jnp.concatenate([kpart.reshape(KV_LORA, -1), vpart.reshape(KV_LORA, -1)], axis=1).astype(BF16)

    def gains(g):
        main = jnp.pad(g, (0, pad_h))[None, :]
        perm = jnp.zeros((1, LANES), g.dtype).at[0, rope_lanes].set(g[MLA_NOPE:][src_m])
        return main, perm

    gqh, gqr = gains(g_q_head)
    gkh, gkr = gains(g_k_head)
    return w_in_r, w_q_r, w_kv_r, gqh, gqr, gkh, gkr


def _routing_tables(idx, rank, counts, tm, blk):
    T = idx.shape[1]
    i32 = jnp.int32
    tot = jnp.sum(counts, axis=0)
    end = jnp.cumsum(tot)
    start = end - tot
    tile_base = start[None, :] + jnp.cumsum(counts, axis=0) - counts
    base_tok = jnp.repeat(tile_base, tm, axis=0)
    hit = idx[:, :, None] == jnp.arange(N_EXPERTS, dtype=i32)
    dest = jnp.sum(jnp.where(hit, base_tok[None], 0), axis=-1) + rank

    n_work = (T * TOP_K) // blk + N_EXPERTS
    first_blk = start // blk
    last_blk = (end - 1) // blk
    per_e = jnp.where(tot > 0, last_blk - first_blk + 1, 0)
    item_end = jnp.cumsum(per_e)
    item_start = item_end - per_e
    n_items = item_end[-1]
    j = jnp.minimum(jnp.arange(n_work, dtype=i32), n_items - 1)
    item_e = jnp.minimum(jnp.sum(item_end[None, :] <= j[:, None], axis=1), N_EXPERTS - 1).astype(i32)
    item_blk = first_blk[item_e] + j - item_start[item_e]
    item_lo = jnp.clip(start[item_e] - item_blk * blk, 0, blk)
    item_hi = jnp.clip(end[item_e] - item_blk * blk, 0, blk)
    prev_blk = jnp.concatenate([jnp.full((1,), -1, i32), item_blk[:-1].astype(i32)])
    item_first = (item_blk != prev_blk).astype(i32)
    items = (item_e, item_blk.astype(i32), item_lo.astype(i32), item_hi.astype(i32), item_first,
             n_items.astype(i32).reshape(1))
    return dest.astype(i32), items


def kernel(x, c, ctx, c_ctx, g_attn, g_ffn, w_ada, b_ada, w_in, g_q_lora, w_q_up, g_q_head,
           g_kv_lora, w_kv_up, g_k_head, ret_decay_logit, g_ret_out, w_out, w_router, b_router,
           w_mlp1, b_mlp1, w_mlp2, b_mlp2):
    B, L, D = x.shape
    Lc = ctx.shape[1]
    T = B * L
    l = 0
    assert w_ada.shape[0] == 1

    rows = ((B + 1 + SUBLANES - 1) // SUBLANES) * SUBLANES
    cc = jnp.zeros((rows, D), F32).at[:B].set(c).at[B].set(c_ctx)
    mod3 = _adaln(cc, w_ada[l], b_ada[l][None, :]).reshape(rows, N_MOD, D)

    w_in_r, w_q_r, w_kv_r, gqh, gqr, gkh, gkr = _prep_weights(
        w_in[l], w_q_up[l], w_kv_up[l], g_q_head[l], g_k_head[l])
    tabs_x = _rope_tables(L, MLA_ROPE, MLA_NOPE, LANES) + _rope_tables(L, RET_DK, None, LANES)
    tabs_c = _identity_tables(Lc) + _identity_tables(Lc)
    proj = functools.partial(_in_proj, g_attn=g_attn[l][None, :], w_in_r=w_in_r,
                             g_q_lora=g_q_lora[l][None, :], w_q_r=w_q_r, gqh=gqh, gqr=gqr,
                             g_kv_lora=g_kv_lora[l][None, :], w_kv_r=w_kv_r, gkh=gkh, gkr=gkr)
    q, kx, vx, rq, rk, rv, rg = proj(x, mod3, lambda b: b, tabs=tabs_x, tm=min(PROJ_TM, L))
    _, kc, vc, _, rk_c, rv_c, _ = proj(ctx, mod3, lambda b: B, tabs=tabs_c, tm=Lc)

    mla = _attention(q, kx, vx, kc, vc, min(ATT_TQ, L), min(ATT_TK, L // 2))

    log_gamma = jax.nn.log_sigmoid(ret_decay_logit[l].astype(F32))
    ret = _retention(log_gamma, g_ret_out[l][None, :], rq, rk, rv, rg, rk_c, rv_c, min(RET_C, L))

    w_r32 = jnp.pad(w_router[l], ((0, 0), (0, LANES - N_EXPERTS)))
    w_r_hi = w_r32.astype(BF16)
    w_r = jnp.concatenate([w_r_hi, (w_r32 - w_r_hi.astype(F32)).astype(BF16)], axis=1)
    b_r = jnp.pad(b_router[l], (0, LANES - N_EXPERTS))[None, :]
    tm = min(OUT_TM, L)
    x1, hf8, idx, rank, gates, cnt = _out_router(
        mla.reshape(T, -1), ret.reshape(T, -1), x.reshape(T, D), mod3,
        w_out[l].astype(BF16), g_ffn[l][None, :], w_r, b_r, L, tm)

    blk = MOE_BLK
    assert (T * TOP_K) % blk == 0
    dest, items = _routing_tables(idx, rank, cnt[:, :, 0], tm, blk)
    dest_flat = dest.T.reshape(-1)

    xs = _dispatch(dest_flat, hf8, T * TOP_K, min(DISP_T, L))

    w1g, w1l = _split_glu(w_mlp1[l])
    b1g = b_mlp1[l][:, None, 0::2]
    b1l = b_mlp1[l][:, None, 1::2]
    ys = _experts(items, xs, w1g, w1l, w_mlp2[l].astype(BF16),
                  b1g, b1l, b_mlp2[l][:, None, :], blk)

    out = _combine(dest_flat, x1, gates.T, mod3, ys, L, min(COMB_T, L))
    return out.reshape(B, L, D)
```

```python
import functools
import math

import jax
import jax.numpy as jnp
from jax import lax
from jax.experimental import pallas as pl
from jax.experimental.pallas import tpu as pltpu

F32 = jnp.float32
BF16 = jnp.bfloat16

LANES = 128
SUBLANES = 8
VMEM_LIMIT_BYTES = 56 * 1024 * 1024

EPS = 1e-6
ROPE_BASE = 10000.0
GRID_W = 64
N_MOD = 6
MLA_HEADS = 8
MLA_NOPE = 64
MLA_ROPE = 32
MLA_QK = MLA_NOPE + MLA_ROPE
MLA_V = 64
Q_LORA = 256
KV_LORA = 128
RET_HEADS = 4
RET_DK = 64
RET_DV = 128
N_EXPERTS = 32
TOP_K = 4
SWIGLU_LIMIT = 7.0
SWIGLU_ALPHA = 1.702
LOG2E = 1.4426950408889634

PROJ_TM = 512
ATT_TQ = 512
ATT_TK = 1024
RET_C = 256
RET_ROWS = 1024
OUT_TM = 512
MOE_BLK = 512
DISP_T = 1024
COMB_T = 256
ISSUE_UNROLL = 8


def _cparams(sem):
    return pltpu.CompilerParams(dimension_semantics=sem, vmem_limit_bytes=VMEM_LIMIT_BYTES)


def _sigmoid(x):
    return 1.0 / (1.0 + jnp.exp(-x))


def _adaln_kernel(c_ref, w_ref, b_ref, o_ref):
    c = c_ref[...]
    s = (c * _sigmoid(c)).astype(BF16)
    o_ref[...] = jnp.dot(s, w_ref[...].astype(BF16), preferred_element_type=F32) + b_ref[...]


def _adaln(cc, w_ada, b_ada):
    rows, d = cc.shape
    n = w_ada.shape[1]
    tn = 1536
    return pl.pallas_call(
        _adaln_kernel,
        grid=(n // tn,),
        in_specs=[pl.BlockSpec((rows, d), lambda j: (0, 0)),
                  pl.BlockSpec((d, tn), lambda j: (0, j)),
                  pl.BlockSpec((1, tn), lambda j: (0, j))],
        out_specs=pl.BlockSpec((rows, tn), lambda j: (0, j)),
        out_shape=jax.ShapeDtypeStruct((rows, n), F32),
        compiler_params=_cparams(("arbitrary",)),
        name="adaln",
    )(cc, w_ada, b_ada)


def _in_proj_kernel(x_ref, mod_ref, gattn_ref, win_ref, gql_ref, wq_ref, gqh_ref, gqr_ref,
                    gkvl_ref, wkv_ref, gkh_ref, gkr_ref, cm_ref, sm_ref, cr_ref, sr_ref,
                    q_ref, k_ref, v_ref, rq_ref, rk_ref, rv_ref, rg_ref, *, q_scale):
    x = x_ref[0]
    shift = mod_ref[0, 0:1, :]
    scale = mod_ref[0, 1:2, :]
    ms = jnp.mean(x * x, axis=-1, keepdims=True)
    h = x * lax.rsqrt(ms + EPS) * gattn_ref[...]
    h = h * (1.0 + scale) + shift
    p = jnp.dot(h.astype(BF16), win_ref[...], preferred_element_type=F32)

    lane = lax.broadcasted_iota(jnp.int32, (1, LANES), 1)
    cm, sm = cm_ref[...], sm_ref[...]
    cr, sr = cr_ref[...], sr_ref[...]
    n_hl = MLA_HEADS * LANES

    cq = p[:, 0:Q_LORA]
    cq = cq * lax.rsqrt(jnp.mean(cq * cq, axis=-1, keepdims=True) + EPS) * gql_ref[...]
    qf = jnp.dot(cq.astype(BF16), wq_ref[...], preferred_element_type=F32)
    gq_cos = gqh_ref[...] * cm
    gq_sin = gqr_ref[...] * sm
    for hd in range(MLA_HEADS):
        blk = qf[:, hd * LANES:(hd + 1) * LANES]
        perm = qf[:, n_hl + hd * LANES:n_hl + (hd + 1) * LANES]
        r = lax.rsqrt(jnp.sum(blk * blk, axis=-1, keepdims=True) * (1.0 / MLA_QK) + EPS)
        q_ref[0, hd] = ((blk * gq_cos + perm * gq_sin) * (r * q_scale)).astype(BF16)

    o_kv = Q_LORA
    ckv = p[:, o_kv:o_kv + KV_LORA]
    ckv = ckv * lax.rsqrt(jnp.mean(ckv * ckv, axis=-1, keepdims=True) + EPS) * gkvl_ref[...]
    kvf = jnp.dot(ckv.astype(BF16), wkv_ref[...], preferred_element_type=F32)
    o_pe = o_kv + KV_LORA
    pe = p[:, o_pe:o_pe + LANES]
    pe_perm = p[:, o_pe + LANES:o_pe + 2 * LANES]
    gk = gkh_ref[...]
    pe_rope = pe * (gk * cm) + pe_perm * (gkr_ref[...] * sm)
    pe_ss = jnp.sum(pe * pe, axis=-1, keepdims=True)
    ones_col = (lane == MLA_V).astype(F32)
    for hd in range(MLA_HEADS):
        kn = kvf[:, hd * LANES:(hd + 1) * LANES]
        ss = jnp.sum(kn * kn, axis=-1, keepdims=True) + pe_ss
        r = lax.rsqrt(ss * (1.0 / MLA_QK) + EPS)
        kk = (kn * gk + pe_rope) * r
        k_ref[0, hd] = kk.T.astype(BF16)
        vv = kvf[:, (MLA_HEADS + hd) * LANES:(MLA_HEADS + hd + 1) * LANES] + ones_col
        v_ref[0, hd] = vv.astype(BF16)

    n_qk = RET_HEADS * RET_DK
    o_rq = o_pe + 2 * LANES
    o_rk = o_rq + 2 * n_qk
    for j in range(n_qk // LANES):
        js = slice(j * LANES, (j + 1) * LANES)
        a = p[:, o_rq + j * LANES:o_rq + (j + 1) * LANES]
        b = p[:, o_rq + n_qk + j * LANES:o_rq + n_qk + (j + 1) * LANES]
        rq_ref[0, :, js] = (a * cr + b * sr).astype(BF16)
        a = p[:, o_rk + j * LANES:o_rk + (j + 1) * LANES]
        b = p[:, o_rk + n_qk + j * LANES:o_rk + n_qk + (j + 1) * LANES]
        rk_ref[0, :, js] = ((a * cr + b * sr) * (RET_DK ** -0.5)).astype(BF16)
    o_rv = o_rk + 2 * n_qk
    n_v = RET_HEADS * RET_DV
    rv_ref[0] = p[:, o_rv:o_rv + n_v].astype(BF16)
    rg_ref[0] = p[:, o_rv + n_v:o_rv + 2 * n_v]


def _in_proj(x, mod3, mod_row_of_batch, g_attn, w_in_r, g_q_lora, w_q_r, gqh, gqr, g_kv_lora,
             w_kv_r, gkh, gkr, tabs, tm):
    B, L, D = x.shape
    cm, sm, cr, sr = tabs
    n_in = w_in_r.shape[1]
    const = lambda b, i: (0, 0)
    tab_spec = pl.BlockSpec((tm, LANES), lambda b, i: (i, 0))
    head_spec = pl.BlockSpec((1, MLA_HEADS, tm, LANES), lambda b, i: (b, 0, i, 0))
    n_qk = RET_HEADS * RET_DK
    n_v = RET_HEADS * RET_DV
    seq_spec = lambda w: pl.BlockSpec((1, tm, w), lambda b, i: (b, i, 0))
    head_shape = jax.ShapeDtypeStruct((B, MLA_HEADS, L, LANES), BF16)
    q_scale = MLA_QK ** -0.5 * LOG2E
    return pl.pallas_call(
        functools.partial(_in_proj_kernel, q_scale=q_scale),
        grid=(B, L // tm),
        in_specs=[
            pl.BlockSpec((1, tm, D), lambda b, i: (b, i, 0)),
            pl.BlockSpec((1, N_MOD, D), lambda b, i: (mod_row_of_batch(b), 0, 0)),
            pl.BlockSpec((1, D), const),
            pl.BlockSpec((D, n_in), const),
            pl.BlockSpec((1, Q_LORA), const),
            pl.BlockSpec(w_q_r.shape, const),
            pl.BlockSpec((1, LANES), const),
            pl.BlockSpec((1, LANES), const),
            pl.BlockSpec((1, KV_LORA), const),
            pl.BlockSpec(w_kv_r.shape, const),
            pl.BlockSpec((1, LANES), const),
            pl.BlockSpec((1, LANES), const),
            tab_spec, tab_spec, tab_spec, tab_spec,
        ],
        out_specs=[head_spec,
                   pl.BlockSpec((1, MLA_HEADS, LANES, tm), lambda b, i: (b, 0, 0, i)),
                   head_spec,
                   seq_spec(n_qk), seq_spec(n_qk), seq_spec(n_v), seq_spec(n_v)],
        out_shape=[head_shape,
                   jax.ShapeDtypeStruct((B, MLA_HEADS, LANES, L), BF16),
                   head_shape,
                   jax.ShapeDtypeStruct((B, L, n_qk), BF16),
                   jax.ShapeDtypeStruct((B, L, n_qk), BF16),
                   jax.ShapeDtypeStruct((B, L, n_v), BF16),
                   jax.ShapeDtypeStruct((B, L, n_v), F32)],
        compiler_params=_cparams(("arbitrary", "arbitrary")),
        name="in_proj",
    )(x, mod3, g_attn, w_in_r, g_q_lora, w_q_r, gqh, gqr, g_kv_lora, w_kv_r, gkh, gkr,
      cm, sm, cr, sr)


def _attn_kernel(q_ref, kx_ref, vx_ref, kc_ref, vc_ref, o_ref, m_ref, acc_ref, s_ref, *, tk):
    n_kv = kx_ref.shape[3] // tk
    n_heads = q_ref.shape[1]

    def scores(hh, start, slot):
        kb = kx_ref[0, hh, :, pl.ds(start, tk)]
        s_ref[hh, slot] = jnp.dot(q_ref[0, hh], kb, preferred_element_type=F32)

    def consume(hh, start, slot):
        s = s_ref[hh, slot]
        m_old = m_ref[hh]
        m_new = jnp.maximum(m_old, jnp.max(s, axis=-1, keepdims=True))
        alpha = jnp.exp2(m_old - m_new)
        pr = jnp.exp2(s - m_new)
        vb = vx_ref[0, hh, pl.ds(start, tk), :]
        acc_ref[hh] = alpha * acc_ref[hh] + jnp.dot(pr.astype(BF16), vb,
                                                    preferred_element_type=F32)
        m_ref[hh] = m_new

    def step(j, slot, prefetch):
        if prefetch:
            nxt = pl.multiple_of((j + 1) * tk, tk)
            for hh in range(n_heads):
                scores(hh, nxt, 1 - slot)
        cur = pl.multiple_of(j * tk, tk)
        for hh in range(n_heads):
            consume(hh, cur, slot)

    for hh in range(n_heads):
        scores(hh, 0, 0)
        s = jnp.dot(q_ref[0, hh], kc_ref[0, hh], preferred_element_type=F32)
        m0 = jnp.max(s, axis=-1, keepdims=True)
        m_ref[hh] = m0
        acc_ref[hh] = jnp.dot(jnp.exp2(s - m0).astype(BF16), vc_ref[0, hh],
                              preferred_element_type=F32)

    def body(jj, carry):
        step(2 * jj, 0, True)
        step(2 * jj + 1, 1, True)
        return carry

    lax.fori_loop(0, n_kv // 2 - 1, body, 0)
    step(n_kv - 2, 0, True)
    step(n_kv - 1, 1, False)
    outs = []
    for hh in range(n_heads):
        acc = acc_ref[hh]
        outs.append(acc[:, :MLA_V] / acc[:, MLA_V:MLA_V + 1])
    o_ref[0] = jnp.concatenate(outs, axis=-1).astype(BF16)


def _attention(q, kx, vx, kc, vc, tq, tk):
    B, H, L, _ = q.shape
    Lc = kc.shape[3]
    return pl.pallas_call(
        functools.partial(_attn_kernel, tk=tk),
        grid=(B, H // 2, L // tq),
        in_specs=[
            pl.BlockSpec((1, 2, tq, LANES), lambda b, h, i: (b, h, i, 0)),
            pl.BlockSpec((1, 2, LANES, L), lambda b, h, i: (b, h, 0, 0)),
            pl.BlockSpec((1, 2, L, LANES), lambda b, h, i: (b, h, 0, 0)),
            pl.BlockSpec((1, 2, LANES, Lc), lambda b, h, i: (b, h, 0, 0)),
            pl.BlockSpec((1, 2, Lc, LANES), lambda b, h, i: (b, h, 0, 0)),
        ],
        out_specs=pl.BlockSpec((1, tq, LANES), lambda b, h, i: (b, i, h)),
        out_shape=jax.ShapeDtypeStruct((B, L, H * MLA_V), BF16),
        scratch_shapes=[pltpu.VMEM((2, tq, 1), F32), pltpu.VMEM((2, tq, LANES), F32),
                        pltpu.VMEM((2, 2, tq, tk), F32)],
        compiler_params=_cparams(("arbitrary", "arbitrary", "arbitrary")),
        name="attention",
    )(q, kx, vx, kc, vc)


def _ret_kernel(lg_ref, lgf_ref, lgb_ref, lgvf_ref, lgvb_ref, gout_ref, rq_ref, rk_ref, rv_ref, rg_ref,
                kc_ref, vc_ref, o_ref, dm_ref, qdf_ref, qdb_ref, kdf_ref, kdb_ref,
                f_ref, r_ref, rs_ref, *, n_steps, C):
    n_sub = rq_ref.shape[1] // C
    Lc = kc_ref.shape[1]
    ps = pl.program_id(1)
    i = pl.program_id(2)
    n_pairs = RET_HEADS // 2
    pw = 2 * RET_DK
    vw = 2 * RET_DV
    tdn = (((0,), (0,)), ((), ()))
    ndn = (((1,), (1,)), ((), ()))
    lgf = lgf_ref[...]
    lgb = lgb_ref[...]

    @pl.when((pl.program_id(0) == 0) & (ps == 0) & (i == 0))
    def _tables():
        a = lax.broadcasted_iota(jnp.int32, (C, C), 0)
        b = lax.broadcasted_iota(jnp.int32, (C, C), 1)
        dab = (a - b).astype(F32)
        for hd in range(RET_HEADS):
            fwd = jnp.where(a >= b, jnp.exp(jnp.where(a >= b, dab, 0.0) * lg_ref[0, hd]), 0.0)
            bwd = jnp.where(b >= a, jnp.exp(jnp.where(b >= a, -dab, 0.0) * lg_ref[1, hd]), 0.0)
            dm_ref[hd] = fwd + bwd
        row = lax.broadcasted_iota(jnp.int32, (C, 1), 0).astype(F32)
        qdf_ref[...] = jnp.exp((row + 1.0) * lgf)
        qdb_ref[...] = jnp.exp((C - row) * lgb)
        kdf_ref[...] = jnp.exp((C - 1.0 - row) * lgf)
        kdb_ref[...] = jnp.exp(row * lgb)

    @pl.when((ps == 0) & (i == 0))
    def _init_states():
        rowc = lax.broadcasted_iota(jnp.int32, (Lc, 1), 0).astype(F32)
        wf = jnp.exp((Lc - 1.0 - rowc) * lgf)
        wb = jnp.exp(rowc * lgb)
        kc = kc_ref[0].astype(F32)
        for pr in range(n_pairs):
            kp = kc[:, pr * pw:(pr + 1) * pw]
            vp = vc_ref[0, :, pr * vw:(pr + 1) * vw]
            f_ref[pr] = lax.dot_general((kp * wf[:, pr * pw:(pr + 1) * pw]).astype(BF16), vp, tdn,
                                        preferred_element_type=F32)
            r_ref[pr] = lax.dot_general((kp * wb[:, pr * pw:(pr + 1) * pw]).astype(BF16), vp, tdn,
                                        preferred_element_type=F32)

    @pl.when(ps == 0)
    def _backward_states():
        cdb = jnp.exp(C * lgvb_ref[...])
        for cc in reversed(range(n_sub)):
            rows = slice(cc * C, (cc + 1) * C)
            c = (n_steps - 1 - i) * n_sub + cc
            k = rk_ref[0, rows, :].astype(F32)
            for pr in range(n_pairs):
                r_old = r_ref[pr]
                rs_ref[c, pr] = r_old.astype(BF16)
                kp = (k[:, pr * pw:(pr + 1) * pw] * kdb_ref[:, pr * pw:(pr + 1) * pw]).astype(BF16)
                vp = rv_ref[0, rows, pr * vw:(pr + 1) * vw]
                upd = lax.dot_general(kp, vp, tdn, preferred_element_type=F32)
                r_ref[pr] = r_old * cdb[:, pr * vw:(pr + 1) * vw] + upd

    @pl.when(ps == 1)
    def _forward_outputs():
        cdf = jnp.exp(C * lgvf_ref[...])
        lane = lax.broadcasted_iota(jnp.int32, (1, pw), 1)
        for cc in range(n_sub):
            rows = slice(cc * C, (cc + 1) * C)
            c = i * n_sub + cc
            q = rq_ref[0, rows, :].astype(F32)
            k = rk_ref[0, rows, :].astype(F32)
            for pr in range(n_pairs):
                sl = slice(pr * pw, (pr + 1) * pw)
                qp = q[:, sl]
                kpb = rk_ref[0, rows, sl]
                qf = qp * qdf_ref[:, sl]
                qb = qp * qdb_ref[:, sl]
                fb = f_ref[pr].astype(BF16)
                rb = rs_ref[c, pr]
                for hh in range(2):
                    hd = 2 * pr + hh
                    hm = (lane // RET_DK) == hh
                    vs = slice(hd * RET_DV, (hd + 1) * RET_DV)
                    fs = slice(hh * RET_DV, (hh + 1) * RET_DV)
                    a = lax.dot_general(jnp.where(hm, qp, 0.0).astype(BF16), kpb, ndn,
                                        preferred_element_type=F32)
                    a = (a * dm_ref[hd]).astype(BF16)
                    o = jnp.dot(a, rv_ref[0, rows, vs], preferred_element_type=F32)
                    o = o + jnp.dot(jnp.where(hm, qf, 0.0).astype(BF16), fb[:, fs],
                                    preferred_element_type=F32)
                    o = o + jnp.dot(jnp.where(hm, qb, 0.0).astype(BF16), rb[:, fs],
                                    preferred_element_type=F32)
                    o = o * lax.rsqrt(jnp.mean(o * o, axis=-1, keepdims=True) + EPS) * gout_ref[:, vs]
                    g = rg_ref[0, rows, vs]
                    o_ref[0, rows, vs] = (o * (g * _sigmoid(g))).astype(BF16)
                kp = (k[:, sl] * kdf_ref[:, sl]).astype(BF16)
                vp = rv_ref[0, rows, pr * vw:(pr + 1) * vw]
                upd = lax.dot_general(kp, vp, tdn, preferred_element_type=F32)
                f_ref[pr] = f_ref[pr] * cdf[:, pr * vw:(pr + 1) * vw] + upd


def _retention(log_gamma, g_ret_out, rq, rk, rv, rg, rk_c, rv_c, C, R):
    B, L, n_qk = rq.shape
    n_v = rv.shape[2]
    Lc = rk_c.shape[1]
    n = L // R
    lgf = jnp.repeat(log_gamma[0], RET_DK)[None, :]
    lgb = jnp.repeat(log_gamma[1], RET_DK)[None, :]
    lgvf = jnp.repeat(log_gamma[0], RET_DV)[None, :]
    lgvb = jnp.repeat(log_gamma[1], RET_DV)[None, :]
    chunk = lambda p, i: jnp.where(p == 0, n - 1 - i, i)
    fwd_only = lambda p, i: jnp.where(p == 0, 0, i)
    const2 = lambda b, p, i: (0, 0)
    n_pairs = RET_HEADS // 2
    return pl.pallas_call(
        functools.partial(_ret_kernel, n_steps=n, C=C),
        grid=(B, 2, n),
        in_specs=[
            pl.BlockSpec(memory_space=pltpu.SMEM),
            pl.BlockSpec((1, n_qk), const2),
            pl.BlockSpec((1, n_qk), const2),
            pl.BlockSpec((1, n_v), const2),
            pl.BlockSpec((1, n_v), const2),
            pl.BlockSpec((1, n_v), const2),
            pl.BlockSpec((1, R, n_qk), lambda b, p, i: (b, fwd_only(p, i), 0)),
            pl.BlockSpec((1, R, n_qk), lambda b, p, i: (b, chunk(p, i), 0)),
            pl.BlockSpec((1, R, n_v), lambda b, p, i: (b, chunk(p, i), 0)),
            pl.BlockSpec((1, R, n_v), lambda b, p, i: (b, fwd_only(p, i), 0)),
            pl.BlockSpec((1, Lc, n_qk), lambda b, p, i: (b, 0, 0)),
            pl.BlockSpec((1, Lc, n_v), lambda b, p, i: (b, 0, 0)),
        ],
        out_specs=pl.BlockSpec((1, R, n_v), lambda b, p, i: (b, fwd_only(p, i), 0)),
        out_shape=jax.ShapeDtypeStruct((B, L, n_v), BF16),
        scratch_shapes=[
            pltpu.VMEM((RET_HEADS, C, C), F32),
            pltpu.VMEM((C, n_qk), F32), pltpu.VMEM((C, n_qk), F32),
            pltpu.VMEM((C, n_qk), F32), pltpu.VMEM((C, n_qk), F32),
            pltpu.VMEM((n_pairs, 2 * RET_DK, 2 * RET_DV), F32),
            pltpu.VMEM((n_pairs, 2 * RET_DK, 2 * RET_DV), F32),
            pltpu.VMEM((L // C, n_pairs, 2 * RET_DK, 2 * RET_DV), BF16),
        ],
        compiler_params=_cparams(("arbitrary", "arbitrary", "arbitrary")),
        name="retention",
    )(log_gamma, lgf, lgb, lgvf, lgvb, g_ret_out, rq, rk, rv, rg, rk_c, rv_c)


def _out_router_kernel(mla_ref, ret_ref, x_ref, mod_ref, wo_ref, gffn_ref, wr_ref, br_ref,
                       x1_ref, hf_ref, idx_ref, rank_ref, gate_ref, cnt_ref):
    tm = x_ref.shape[0]
    n_mla = mla_ref.shape[1]
    y = jnp.dot(mla_ref[...], wo_ref[0:n_mla, :], preferred_element_type=F32)
    y = y + jnp.dot(ret_ref[...], wo_ref[n_mla:, :], preferred_element_type=F32)
    x1 = x_ref[...] + mod_ref[0, 2:3, :] * y
    x1_ref[...] = x1
    ms = jnp.mean(x1 * x1, axis=-1, keepdims=True)
    hf = x1 * lax.rsqrt(ms + EPS) * gffn_ref[...]
    hf = hf * (1.0 + mod_ref[0, 4:5, :]) + mod_ref[0, 3:4, :]
    for c in range(hf.shape[1] // LANES):
        hf_ref[pl.ds(c, tm, stride=SUBLANES), :] = hf[:, c * LANES:(c + 1) * LANES]

    hi = hf.astype(BF16)
    lo = (hf - hi.astype(F32)).astype(BF16)
    both = jnp.dot(hi, wr_ref[...], preferred_element_type=F32)
    logits = (both[:, :LANES] + both[:, LANES:]
              + jnp.dot(lo, wr_ref[:, :LANES], preferred_element_type=F32)) + br_ref[...]
    lt = logits.T[0:N_EXPERTS, :]
    e_iota = lax.broadcasted_iota(jnp.int32, (N_EXPERTS, tm), 0).astype(F32)
    vals, idxs, hits = [], [], []
    for _ in range(TOP_K):
        mx = jnp.max(lt, axis=0, keepdims=True)
        ix = jnp.min(jnp.where(lt == mx, e_iota, float(N_EXPERTS)), axis=0, keepdims=True)
        hit = e_iota == ix
        lt = jnp.where(hit, -jnp.inf, lt)
        vals.append(mx)
        idxs.append(ix)
        hits.append(hit.astype(F32))

    s_iota = lax.broadcasted_iota(jnp.int32, (tm, tm), 0)
    t_iota = lax.broadcasted_iota(jnp.int32, (tm, tm), 1)
    upper = (s_iota <= t_iota).astype(F32).astype(BF16)
    prefix = jnp.dot(jnp.concatenate(hits, axis=0).astype(BF16), upper,
                     preferred_element_type=F32)
    ranks = []
    seen = jnp.zeros((N_EXPERTS, 1), F32)
    for k in range(TOP_K):
        pk = prefix[k * N_EXPERTS:(k + 1) * N_EXPERTS, :]
        rank = jnp.sum(hits[k] * (pk - 1.0 + seen), axis=0, keepdims=True)
        seen = seen + jnp.sum(hits[k], axis=1, keepdims=True)
        ranks.append(rank.astype(jnp.int32))
    ex = [jnp.exp(v - vals[0]) for v in vals]
    den = ex[0] + ex[1] + ex[2] + ex[3]
    idx_ref[...] = jnp.concatenate(idxs, axis=0).astype(jnp.int32)
    rank_ref[...] = jnp.concatenate(ranks, axis=0)
    gate_ref[...] = jnp.concatenate([e / den for e in ex], axis=0)
    cnt_ref[0] = jnp.broadcast_to(seen, (N_EXPERTS, LANES)).astype(jnp.int32)


def _out_router(mla, ret, x2, mod3, w_out_b, g_ffn, w_r, b_r, L, tm):
    T, D = x2.shape
    n_tiles = T // tm
    per_b = L // tm
    const = lambda i: (0, 0)
    return pl.pallas_call(
        _out_router_kernel,
        grid=(n_tiles,),
        in_specs=[
            pl.BlockSpec((tm, mla.shape[1]), lambda i: (i, 0)),
            pl.BlockSpec((tm, ret.shape[1]), lambda i: (i, 0)),
            pl.BlockSpec((tm, D), lambda i: (i, 0)),
            pl.BlockSpec((1, N_MOD, D), lambda i: (i // per_b, 0, 0)),
            pl.BlockSpec(w_out_b.shape, const),
            pl.BlockSpec((1, D), const),
            pl.BlockSpec(w_r.shape, const),
            pl.BlockSpec((1, LANES), const),
        ],
        out_specs=[
            pl.BlockSpec((tm, D), lambda i: (i, 0)),
            pl.BlockSpec((tm * SUBLANES, LANES), lambda i: (i, 0)),
            pl.BlockSpec((TOP_K, tm), lambda i: (0, i)),
            pl.BlockSpec((TOP_K, tm), lambda i: (0, i)),
            pl.BlockSpec((TOP_K, tm), lambda i: (0, i)),
            pl.BlockSpec((1, N_EXPERTS, LANES), lambda i: (i, 0, 0)),
        ],
        out_shape=[
            jax.ShapeDtypeStruct((T, D), F32),
            jax.ShapeDtypeStruct((T * SUBLANES, LANES), F32),
            jax.ShapeDtypeStruct((TOP_K, T), jnp.int32),
            jax.ShapeDtypeStruct((TOP_K, T), jnp.int32),
            jax.ShapeDtypeStruct((TOP_K, T), F32),
            jax.ShapeDtypeStruct((n_tiles, N_EXPERTS, LANES), jnp.int32),
        ],
        compiler_params=_cparams(("arbitrary",)),
        name="out_router",
    )(mla, ret, x2, mod3, w_out_b, g_ffn, w_r, b_r)


def _row_copy(src, dst, sem):
    return pltpu.make_async_copy(src, dst, sem)


def _dispatch_kernel(dest_ref, hf_ref, xs_ref, sem, *, td):
    def issue(t, carry):
        src = hf_ref.at[pl.ds(pl.multiple_of(t * SUBLANES, SUBLANES), SUBLANES), :]
        for k in range(TOP_K):
            d = dest_ref[t * TOP_K + k]
            dst = xs_ref.at[pl.ds(pl.multiple_of(d * SUBLANES, SUBLANES), SUBLANES), :]
            _row_copy(src, dst, sem).start(priority=k % 2)
        return carry

    lax.fori_loop(0, td, issue, 0, unroll=ISSUE_UNROLL)
    for k in range(TOP_K):
        _row_copy(hf_ref, xs_ref.at[pl.ds(0, td * SUBLANES), :], sem).wait()


def _dispatch(dest_flat, hf8, n_pad, td):
    T8, _ = hf8.shape
    T = T8 // SUBLANES
    return pl.pallas_call(
        functools.partial(_dispatch_kernel, td=td),
        grid=(T // td,),
        in_specs=[
            pl.BlockSpec((td * TOP_K,), lambda i: (i,), memory_space=pltpu.SMEM),
            pl.BlockSpec((td * SUBLANES, LANES), lambda i: (i, 0)),
        ],
        out_specs=pl.BlockSpec(memory_space=pl.ANY),
        out_shape=jax.ShapeDtypeStruct((n_pad * SUBLANES, LANES), F32),
        scratch_shapes=[pltpu.SemaphoreType.DMA],
        compiler_params=_cparams(("arbitrary",)),
        name="dispatch",
    )(dest_flat, hf8)


def _split_glu_kernel(w_ref, g_ref, l_ref):
    w = w_ref[0].astype(BF16)
    sub = 2 * LANES
    r = lax.broadcasted_iota(jnp.int32, (sub, sub), 0)
    c = lax.broadcasted_iota(jnp.int32, (sub, sub), 1)
    src_col = jnp.where(c < LANES, 2 * c, 2 * (c - LANES) + 1)
    sel = (r == src_col).astype(F32).astype(BF16)
    for s in range(w.shape[1] // sub):
        t = jnp.dot(w[:, s * sub:(s + 1) * sub], sel, preferred_element_type=F32).astype(BF16)
        g_ref[0, :, s * LANES:(s + 1) * LANES] = t[:, :LANES]
        l_ref[0, :, s * LANES:(s + 1) * LANES] = t[:, LANES:]


def _split_glu(w1):
    E, d, f2 = w1.shape
    tn = 1024
    half = jax.ShapeDtypeStruct((E, d, f2 // 2), BF16)
    return pl.pallas_call(
        _split_glu_kernel,
        grid=(E, f2 // tn),
        in_specs=[pl.BlockSpec((1, d, tn), lambda e, j: (e, 0, j))],
        out_specs=[pl.BlockSpec((1, d, tn // 2), lambda e, j: (e, 0, j)),
                   pl.BlockSpec((1, d, tn // 2), lambda e, j: (e, 0, j))],
        out_shape=[half, half],
        compiler_params=_cparams(("arbitrary", "arbitrary")),
        name="split_glu",
    )(w1)


def _experts_kernel(ie_ref, ib_ref, lo_ref, hi_ref, first_ref, ni_ref, xs_ref, w1g_ref, w1l_ref,
                    w2_ref, b1g_ref, b1l_ref, b2_ref, ys_ref):
    i = pl.program_id(0)
    blk = xs_ref.shape[0] // SUBLANES
    d = w1g_ref.shape[1]

    @pl.when(i < ni_ref[0])
    def _():
        cols = [xs_ref[pl.ds(c, blk, stride=SUBLANES), :] for c in range(d // LANES)]
        x = jnp.concatenate(cols, axis=-1)
        row = lax.broadcasted_iota(jnp.int32, (blk, 1), 0)
        mine = (row >= lo_ref[i]) & (row < hi_ref[i])
        x = jnp.where(mine, x, 0.0).astype(BF16)
        hg = jnp.dot(x, w1g_ref[0], preferred_element_type=F32) + b1g_ref[0]
        hl = jnp.dot(x, w1l_ref[0], preferred_element_type=F32) + b1l_ref[0]
        glu = jnp.minimum(hg, SWIGLU_LIMIT)
        lin = jnp.clip(hl, -SWIGLU_LIMIT, SWIGLU_LIMIT)
        act = glu * _sigmoid(SWIGLU_ALPHA * glu) * (lin + 1.0)
        y = jnp.dot(act.astype(BF16), w2_ref[0], preferred_element_type=F32) + b2_ref[0]
        y = jnp.where(mine, y, 0.0)

        @pl.when(first_ref[i] == 1)
        def _():
            for c in range(d // LANES):
                ys_ref[pl.ds(c, blk, stride=SUBLANES), :] = y[:, c * LANES:(c + 1) * LANES]

        @pl.when(first_ref[i] == 0)
        def _():
            for c in range(d // LANES):
                ys_ref[pl.ds(c, blk, stride=SUBLANES), :] += y[:, c * LANES:(c + 1) * LANES]


def _experts(items, xs, w1g, w1l, w2, b1g, b1l, b2, blk):
    item_e, item_blk, item_lo, item_hi, item_first, n_items = items
    d = w1g.shape[1]
    f = w1g.shape[2]
    row_map = lambda i, ie, ib, lo, hi, fi, ni: (ib[i], 0)
    exp_map = lambda i, ie, ib, lo, hi, fi, ni: (ie[i], 0, 0)
    grid_spec = pltpu.PrefetchScalarGridSpec(
        num_scalar_prefetch=6,
        grid=(item_e.shape[0],),
        in_specs=[
            pl.BlockSpec((blk * SUBLANES, LANES), row_map),
            pl.BlockSpec((1, d, f), exp_map),
            pl.BlockSpec((1, d, f), exp_map),
            pl.BlockSpec((1, f, d), exp_map),
            pl.BlockSpec((1, 1, f), exp_map),
            pl.BlockSpec((1, 1, f), exp_map),
            pl.BlockSpec((1, 1, d), exp_map),
        ],
        out_specs=pl.BlockSpec((blk * SUBLANES, LANES), row_map),
    )
    return pl.pallas_call(
        _experts_kernel,
        grid_spec=grid_spec,
        out_shape=jax.ShapeDtypeStruct(xs.shape, F32),
        compiler_params=_cparams(("arbitrary",)),
        name="experts",
    )(item_e, item_blk, item_lo, item_hi, item_first, n_items, xs, w1g, w1l, w2, b1g, b1l, b2)


def _combine_kernel(dest_ref, dnext_ref, x1_ref, gate_ref, mod_ref, ys_ref, o_ref, buf_ref, sems,
                    *, tc):
    i = pl.program_id(0)
    n = pl.num_programs(0)
    slot = i % 2

    def gather(idx_ref, s):
        def issue(t, carry):
            for k in range(TOP_K):
                d = idx_ref[t * TOP_K + k]
                src = ys_ref.at[pl.ds(pl.multiple_of(d * SUBLANES, SUBLANES), SUBLANES), :]
                dst = buf_ref.at[s, k, pl.ds(pl.multiple_of(t * SUBLANES, SUBLANES), SUBLANES), :]
                _row_copy(src, dst, sems.at[s]).start(priority=k % 2)
            return carry
        lax.fori_loop(0, tc, issue, 0, unroll=ISSUE_UNROLL)

    @pl.when(i == 0)
    def _():
        gather(dest_ref, 0)

    @pl.when(i + 1 < n)
    def _():
        gather(dnext_ref, 1 - slot)

    for k in range(TOP_K):
        _row_copy(ys_ref.at[pl.ds(0, tc * SUBLANES), :], buf_ref.at[slot, k], sems.at[slot]).wait()

    g = gate_ref[...]
    for c in range(o_ref.shape[1] // LANES):
        cs = slice(c * LANES, (c + 1) * LANES)
        acc = g[:, 0:1] * buf_ref[slot, 0, pl.ds(c, tc, stride=SUBLANES), :]
        for k in range(1, TOP_K):
            acc = acc + g[:, k:k + 1] * buf_ref[slot, k, pl.ds(c, tc, stride=SUBLANES), :]
        o_ref[:, cs] = x1_ref[:, cs] + mod_ref[0, 5:6, cs] * acc


def _combine(dest_flat, x1, gates_t, mod3, ys, L, tc):
    T, D = x1.shape
    per_b = L // tc
    n = T // tc
    return pl.pallas_call(
        functools.partial(_combine_kernel, tc=tc),
        grid=(n,),
        in_specs=[
            pl.BlockSpec((tc * TOP_K,), lambda i: (i,), memory_space=pltpu.SMEM),
            pl.BlockSpec((tc * TOP_K,), lambda i: (jnp.minimum(i + 1, n - 1),),
                         memory_space=pltpu.SMEM),
            pl.BlockSpec((tc, D), lambda i: (i, 0)),
            pl.BlockSpec((tc, TOP_K), lambda i: (i, 0)),
            pl.BlockSpec((1, N_MOD, D), lambda i: (i // per_b, 0, 0)),
            pl.BlockSpec(memory_space=pl.ANY),
        ],
        out_specs=pl.BlockSpec((tc, D), lambda i: (i, 0)),
        out_shape=jax.ShapeDtypeStruct((T, D), F32),
        scratch_shapes=[pltpu.VMEM((2, TOP_K, tc * SUBLANES, LANES), F32),
                        pltpu.SemaphoreType.DMA((2,))],
        compiler_params=_cparams(("arbitrary",)),
        name="combine",
    )(dest_flat, dest_flat, x1, gates_t, mod3, ys)


def _rope_tables(L, dim, lane_off, width):
    rows = L // GRID_W
    nf = dim // 4
    inv = jnp.power(ROPE_BASE, -jnp.arange(nf, dtype=F32) / nf)
    row = jnp.repeat(jnp.arange(rows, dtype=F32), GRID_W)
    col = jnp.tile(jnp.arange(GRID_W, dtype=F32), rows)
    pos = jnp.stack([row, col], axis=-1)
    ang = pos[:, :, None] * inv
    ang = jnp.broadcast_to(ang[:, :, None, :], (L, 2, 2, nf)).reshape(L, dim)
    sign = jnp.where((jnp.arange(dim) % (dim // 2)) < nf, -1.0, 1.0).astype(F32)
    cos, sin = jnp.cos(ang), jnp.sin(ang) * sign
    if lane_off is None:
        reps = width // dim
        return jnp.tile(cos, (1, reps)), jnp.tile(sin, (1, reps))
    cfull = jnp.ones((L, width), F32).at[:, lane_off:lane_off + dim].set(cos)
    sfull = jnp.zeros((L, width), F32).at[:, lane_off:lane_off + dim].set(sin)
    return cfull, sfull


def _identity_tables(L):
    return jnp.ones((L, LANES), F32), jnp.zeros((L, LANES), F32)


def _half_rot_src(dim):
    j = jnp.arange(dim)
    return jnp.where((j % (dim // 2)) < dim // 4, j + dim // 4, j - dim // 4)


def _prep_weights(w_in, w_q_up, w_kv_up, g_q_head, g_k_head):
    D = w_in.shape[0]
    o = 0
    wq = w_in[:, o:o + Q_LORA]; o += Q_LORA
    wkv = w_in[:, o:o + KV_LORA]; o += KV_LORA
    wpe = w_in[:, o:o + MLA_ROPE]; o += MLA_ROPE
    n_qk = RET_HEADS * RET_DK
    wrq = w_in[:, o:o + n_qk]; o += n_qk
    wrk = w_in[:, o:o + n_qk]; o += n_qk
    rest = w_in[:, o:]
    src_m = _half_rot_src(MLA_ROPE)
    src_r = _half_rot_src(RET_DK)
    rope_lanes = slice(MLA_NOPE, MLA_NOPE + MLA_ROPE)
    pe_blk = jnp.zeros((D, LANES), w_in.dtype).at[:, rope_lanes].set(wpe)
    pe_perm = jnp.zeros((D, LANES), w_in.dtype).at[:, rope_lanes].set(wpe[:, src_m])
    perm_heads = lambda w: w.reshape(D, RET_HEADS, RET_DK)[:, :, src_r].reshape(D, n_qk)
    w_in_r = jnp.concatenate([wq, wkv, pe_blk, pe_perm, wrq, perm_heads(wrq), wrk, perm_heads(wrk),
                              rest], axis=1).astype(BF16)

    pad_h = LANES - MLA_QK
    wq3 = w_q_up.reshape(Q_LORA, MLA_HEADS, MLA_QK)
    q_main = jnp.pad(wq3, ((0, 0), (0, 0), (0, pad_h)))
    q_perm = jnp.zeros_like(q_main).at[:, :, rope_lanes].set(wq3[:, :, MLA_NOPE:][:, :, src_m])
    w_q_r = jnp.concatenate([q_main.reshape(Q_LORA, -1), q_perm.reshape(Q_LORA, -1)], axis=1).astype(BF16)

    kv = w_kv_up.reshape(KV_LORA, MLA_HEADS, MLA_NOPE + MLA_V)
    kpart = jnp.pad(kv[:, :, :MLA_NOPE], ((0, 0), (0, 0), (0, LANES - MLA_NOPE)))
    vpart = jnp.pad(kv[:, :, MLA_NOPE:], ((0, 0), (0, 0), (0, LANES - MLA_V)))
    w_kv_r = jnp.concatenate([kpart.reshape(KV_LORA, -1), vpart.reshape(KV_LORA, -1)], axis=1).astype(BF16)

    def gains(g):
        main = jnp.pad(g, (0, pad_h))[None, :]
        perm = jnp.zeros((1, LANES), g.dtype).at[0, rope_lanes].set(g[MLA_NOPE:][src_m])
        return main, perm

    gqh, gqr = gains(g_q_head)
    gkh, gkr = gains(g_k_head)
    return w_in_r, w_q_r, w_kv_r, gqh, gqr, gkh, gkr


def _routing_tables(idx, rank, counts, tm, blk):
    T = idx.shape[1]
    i32 = jnp.int32
    tot = jnp.sum(counts, axis=0)
    end = jnp.cumsum(tot)
    start = end - tot
    tile_base = start[None, :] + jnp.cumsum(counts, axis=0) - counts
    base_tok = jnp.repeat(tile_base, tm, axis=0)
    hit = idx[:, :, None] == jnp.arange(N_EXPERTS, dtype=i32)
    dest = jnp.sum(jnp.where(hit, base_tok[None], 0), axis=-1) + rank

    n_work = (T * TOP_K) // blk + N_EXPERTS
    first_blk = start // blk
    last_blk = (end - 1) // blk
    per_e = jnp.where(tot > 0, last_blk - first_blk + 1, 0)
    item_end = jnp.cumsum(per_e)
    item_start = item_end - per_e
    n_items = item_end[-1]
    j = jnp.minimum(jnp.arange(n_work, dtype=i32), n_items - 1)
    item_e = jnp.minimum(jnp.sum(item_end[None, :] <= j[:, None], axis=1), N_EXPERTS - 1).astype(i32)
    item_blk = first_blk[item_e] + j - item_start[item_e]
    item_lo = jnp.clip(start[item_e] - item_blk * blk, 0, blk)
    item_hi = jnp.clip(end[item_e] - item_blk * blk, 0, blk)
    prev_blk = jnp.concatenate([jnp.full((1,), -1, i32), item_blk[:-1].astype(i32)])
    item_first = (item_blk != prev_blk).astype(i32)
    items = (item_e, item_blk.astype(i32), item_lo.astype(i32), item_hi.astype(i32), item_first,
             n_items.astype(i32).reshape(1))
    return dest.astype(i32), items


def kernel(x, c, ctx, c_ctx, g_attn, g_ffn, w_ada, b_ada, w_in, g_q_lora, w_q_up, g_q_head,
           g_kv_lora, w_kv_up, g_k_head, ret_decay_logit, g_ret_out, w_out, w_router, b_router,
           w_mlp1, b_mlp1, w_mlp2, b_mlp2):
    B, L, D = x.shape
    Lc = ctx.shape[1]
    T = B * L
    l = 0
    assert w_ada.shape[0] == 1

    rows = ((B + 1 + SUBLANES - 1) // SUBLANES) * SUBLANES
    cc = jnp.zeros((rows, D), F32).at[:B].set(c).at[B].set(c_ctx)
    mod3 = _adaln(cc, w_ada[l], b_ada[l][None, :]).reshape(rows, N_MOD, D)

    w_in_r, w_q_r, w_kv_r, gqh, gqr, gkh, gkr = _prep_weights(
        w_in[l], w_q_up[l], w_kv_up[l], g_q_head[l], g_k_head[l])
    tabs_x = _rope_tables(L, MLA_ROPE, MLA_NOPE, LANES) + _rope_tables(L, RET_DK, None, LANES)
    tabs_c = _identity_tables(Lc) + _identity_tables(Lc)
    proj = functools.partial(_in_proj, g_attn=g_attn[l][None, :], w_in_r=w_in_r,
                             g_q_lora=g_q_lora[l][None, :], w_q_r=w_q_r, gqh=gqh, gqr=gqr,
                             g_kv_lora=g_kv_lora[l][None, :], w_kv_r=w_kv_r, gkh=gkh, gkr=gkr)
    q, kx, vx, rq, rk, rv, rg = proj(x, mod3, lambda b: b, tabs=tabs_x, tm=min(PROJ_TM, L))
    _, kc, vc, _, rk_c, rv_c, _ = proj(ctx, mod3, lambda b: B, tabs=tabs_c, tm=Lc)

    mla = _attention(q, kx, vx, kc, vc, min(ATT_TQ, L), min(ATT_TK, L // 2))

    log_gamma = jax.nn.log_sigmoid(ret_decay_logit[l].astype(F32))
    ret = _retention(log_gamma, g_ret_out[l][None, :], rq, rk, rv, rg, rk_c, rv_c,
                     min(RET_C, L), min(RET_ROWS, L))

    w_r32 = jnp.pad(w_router[l], ((0, 0), (0, LANES - N_EXPERTS)))
    w_r_hi = w_r32.astype(BF16)
    w_r = jnp.concatenate([w_r_hi, (w_r32 - w_r_hi.astype(F32)).astype(BF16)], axis=1)
    b_r = jnp.pad(b_router[l], (0, LANES - N_EXPERTS))[None, :]
    tm = min(OUT_TM, L)
    x1, hf8, idx, rank, gates, cnt = _out_router(
        mla.reshape(T, -1), ret.reshape(T, -1), x.reshape(T, D), mod3,
        w_out[l].astype(BF16), g_ffn[l][None, :], w_r, b_r, L, tm)

    blk = MOE_BLK
    assert (T * TOP_K) % blk == 0
    dest, items = _routing_tables(idx, rank, cnt[:, :, 0], tm, blk)
    dest_flat = dest.T.reshape(-1)

    xs = _dispatch(dest_flat, hf8, T * TOP_K, min(DISP_T, L))

    w1g, w1l = _split_glu(w_mlp1[l])
    b1g = b_mlp1[l][:, None, 0::2]
    b1l = b_mlp1[l][:, None, 1::2]
    ys = _experts(items, xs, w1g, w1l, w_mlp2[l].astype(BF16),
                  b1g, b1l, b_mlp2[l][:, None, :], blk)

    out = _combine(dest_flat, x1, gates.T, mod3, ys, L, min(COMB_T, L))
    return out.reshape(B, L, D)
```

```python
import functools
import math

import jax
import jax.numpy as jnp
from jax import lax
from jax.experimental import pallas as pl
from jax.experimental.pallas import tpu as pltpu

F32 = jnp.float32
BF16 = jnp.bfloat16

LANES = 128
SUBLANES = 8
VMEM_LIMIT_BYTES = 56 * 1024 * 1024

EPS = 1e-6
ROPE_BASE = 10000.0
GRID_W = 64
N_MOD = 6
MLA_HEADS = 8
MLA_NOPE = 64
MLA_ROPE = 32
MLA_QK = MLA_NOPE + MLA_ROPE
MLA_V = 64
Q_LORA = 256
KV_LORA = 128
RET_HEADS = 4
RET_DK = 64
RET_DV = 128
N_EXPERTS = 32
TOP_K = 4
SWIGLU_LIMIT = 7.0
SWIGLU_ALPHA = 1.702
LOG2E = 1.4426950408889634

PROJ_TM = 512
ATT_TQ = 512
ATT_TK = 1024
RET_C = 256
RET_ROWS = 1024
OUT_TM = 512
MOE_BLK = 512
COMB_T = 512
ISSUE_UNROLL = 8


def _cparams(sem):
    return pltpu.CompilerParams(dimension_semantics=sem, vmem_limit_bytes=VMEM_LIMIT_BYTES)


def _sigmoid(x):
    return 1.0 / (1.0 + jnp.exp(-x))


def _adaln_kernel(c_ref, w_ref, b_ref, o_ref):
    c = c_ref[...]
    s = (c * _sigmoid(c)).astype(BF16)
    o_ref[...] = jnp.dot(s, w_ref[...].astype(BF16), preferred_element_type=F32) + b_ref[...]


def _adaln(cc, w_ada, b_ada):
    rows, d = cc.shape
    n = w_ada.shape[1]
    tn = 1536
    return pl.pallas_call(
        _adaln_kernel,
        grid=(n // tn,),
        in_specs=[pl.BlockSpec((rows, d), lambda j: (0, 0)),
                  pl.BlockSpec((d, tn), lambda j: (0, j)),
                  pl.BlockSpec((1, tn), lambda j: (0, j))],
        out_specs=pl.BlockSpec((rows, tn), lambda j: (0, j)),
        out_shape=jax.ShapeDtypeStruct((rows, n), F32),
        compiler_params=_cparams(("arbitrary",)),
        name="adaln",
    )(cc, w_ada, b_ada)


def _in_proj_kernel(x_ref, mod_ref, gattn_ref, win_ref, gql_ref, wq_ref, gqh_ref, gqr_ref,
                    gkvl_ref, wkv_ref, gkh_ref, gkr_ref, cm_ref, sm_ref, cr_ref, sr_ref,
                    q_ref, k_ref, v_ref, rq_ref, rk_ref, rv_ref, rg_ref, *, q_scale):
    x = x_ref[0]
    shift = mod_ref[0, 0:1, :]
    scale = mod_ref[0, 1:2, :]
    ms = jnp.mean(x * x, axis=-1, keepdims=True)
    h = x * lax.rsqrt(ms + EPS) * gattn_ref[...]
    h = h * (1.0 + scale) + shift
    p = jnp.dot(h.astype(BF16), win_ref[...], preferred_element_type=F32)

    lane = lax.broadcasted_iota(jnp.int32, (1, LANES), 1)
    cm, sm = cm_ref[...], sm_ref[...]
    cr, sr = cr_ref[...], sr_ref[...]
    n_hl = MLA_HEADS * LANES

    cq = p[:, 0:Q_LORA]
    cq = cq * lax.rsqrt(jnp.mean(cq * cq, axis=-1, keepdims=True) + EPS) * gql_ref[...]
    qf = jnp.dot(cq.astype(BF16), wq_ref[...], preferred_element_type=F32)
    gq_cos = gqh_ref[...] * cm
    gq_sin = gqr_ref[...] * sm
    for hd in range(MLA_HEADS):
        blk = qf[:, hd * LANES:(hd + 1) * LANES]
        perm = qf[:, n_hl + hd * LANES:n_hl + (hd + 1) * LANES]
        r = lax.rsqrt(jnp.sum(blk * blk, axis=-1, keepdims=True) * (1.0 / MLA_QK) + EPS)
        q_ref[0, hd] = ((blk * gq_cos + perm * gq_sin) * (r * q_scale)).astype(BF16)

    o_kv = Q_LORA
    ckv = p[:, o_kv:o_kv + KV_LORA]
    ckv = ckv * lax.rsqrt(jnp.mean(ckv * ckv, axis=-1, keepdims=True) + EPS) * gkvl_ref[...]
    kvf = jnp.dot(ckv.astype(BF16), wkv_ref[...], preferred_element_type=F32)
    o_pe = o_kv + KV_LORA
    pe = p[:, o_pe:o_pe + LANES]
    pe_perm = p[:, o_pe + LANES:o_pe + 2 * LANES]
    gk = gkh_ref[...]
    pe_rope = pe * (gk * cm) + pe_perm * (gkr_ref[...] * sm)
    pe_ss = jnp.sum(pe * pe, axis=-1, keepdims=True)
    ones_col = (lane == MLA_V).astype(F32)
    for hd in range(MLA_HEADS):
        kn = kvf[:, hd * LANES:(hd + 1) * LANES]
        ss = jnp.sum(kn * kn, axis=-1, keepdims=True) + pe_ss
        r = lax.rsqrt(ss * (1.0 / MLA_QK) + EPS)
        kk = (kn * gk + pe_rope) * r
        k_ref[0, hd] = kk.T.astype(BF16)
        vv = kvf[:, (MLA_HEADS + hd) * LANES:(MLA_HEADS + hd + 1) * LANES] + ones_col
        v_ref[0, hd] = vv.astype(BF16)

    n_qk = RET_HEADS * RET_DK
    o_rq = o_pe + 2 * LANES
    o_rk = o_rq + 2 * n_qk
    for j in range(n_qk // LANES):
        js = slice(j * LANES, (j + 1) * LANES)
        a = p[:, o_rq + j * LANES:o_rq + (j + 1) * LANES]
        b = p[:, o_rq + n_qk + j * LANES:o_rq + n_qk + (j + 1) * LANES]
        rq_ref[0, :, js] = (a * cr + b * sr).astype(BF16)
        a = p[:, o_rk + j * LANES:o_rk + (j + 1) * LANES]
        b = p[:, o_rk + n_qk + j * LANES:o_rk + n_qk + (j + 1) * LANES]
        rk_ref[0, :, js] = ((a * cr + b * sr) * (RET_DK ** -0.5)).astype(BF16)
    o_rv = o_rk + 2 * n_qk
    n_v = RET_HEADS * RET_DV
    rv_ref[0] = p[:, o_rv:o_rv + n_v].astype(BF16)
    rg_ref[0] = p[:, o_rv + n_v:o_rv + 2 * n_v]


def _in_proj(x, mod3, mod_row_of_batch, g_attn, w_in_r, g_q_lora, w_q_r, gqh, gqr, g_kv_lora,
             w_kv_r, gkh, gkr, tabs, tm):
    B, L, D = x.shape
    cm, sm, cr, sr = tabs
    n_in = w_in_r.shape[1]
    const = lambda b, i: (0, 0)
    tab_spec = pl.BlockSpec((tm, LANES), lambda b, i: (i, 0))
    head_spec = pl.BlockSpec((1, MLA_HEADS, tm, LANES), lambda b, i: (b, 0, i, 0))
    n_qk = RET_HEADS * RET_DK
    n_v = RET_HEADS * RET_DV
    seq_spec = lambda w: pl.BlockSpec((1, tm, w), lambda b, i: (b, i, 0))
    head_shape = jax.ShapeDtypeStruct((B, MLA_HEADS, L, LANES), BF16)
    q_scale = MLA_QK ** -0.5 * LOG2E
    return pl.pallas_call(
        functools.partial(_in_proj_kernel, q_scale=q_scale),
        grid=(B, L // tm),
        in_specs=[
            pl.BlockSpec((1, tm, D), lambda b, i: (b, i, 0)),
            pl.BlockSpec((1, N_MOD, D), lambda b, i: (mod_row_of_batch(b), 0, 0)),
            pl.BlockSpec((1, D), const),
            pl.BlockSpec((D, n_in), const),
            pl.BlockSpec((1, Q_LORA), const),
            pl.BlockSpec(w_q_r.shape, const),
            pl.BlockSpec((1, LANES), const),
            pl.BlockSpec((1, LANES), const),
            pl.BlockSpec((1, KV_LORA), const),
            pl.BlockSpec(w_kv_r.shape, const),
            pl.BlockSpec((1, LANES), const),
            pl.BlockSpec((1, LANES), const),
            tab_spec, tab_spec, tab_spec, tab_spec,
        ],
        out_specs=[head_spec,
                   pl.BlockSpec((1, MLA_HEADS, LANES, tm), lambda b, i: (b, 0, 0, i)),
                   head_spec,
                   seq_spec(n_qk), seq_spec(n_qk), seq_spec(n_v), seq_spec(n_v)],
        out_shape=[head_shape,
                   jax.ShapeDtypeStruct((B, MLA_HEADS, LANES, L), BF16),
                   head_shape,
                   jax.ShapeDtypeStruct((B, L, n_qk), BF16),
                   jax.ShapeDtypeStruct((B, L, n_qk), BF16),
                   jax.ShapeDtypeStruct((B, L, n_v), BF16),
                   jax.ShapeDtypeStruct((B, L, n_v), F32)],
        compiler_params=_cparams(("arbitrary", "arbitrary")),
        name="in_proj",
    )(x, mod3, g_attn, w_in_r, g_q_lora, w_q_r, gqh, gqr, g_kv_lora, w_kv_r, gkh, gkr,
      cm, sm, cr, sr)


def _attn_kernel(q_ref, kx_ref, vx_ref, kc_ref, vc_ref, o_ref, m_ref, acc_ref, s_ref, *, tk):
    n_kv = kx_ref.shape[3] // tk
    n_heads = q_ref.shape[1]

    def scores(hh, start, slot):
        kb = kx_ref[0, hh, :, pl.ds(start, tk)]
        s_ref[hh, slot] = jnp.dot(q_ref[0, hh], kb, preferred_element_type=F32)

    def consume(hh, start, slot):
        s = s_ref[hh, slot]
        m_old = m_ref[hh]
        m_new = jnp.maximum(m_old, jnp.max(s, axis=-1, keepdims=True))
        alpha = jnp.exp2(m_old - m_new)
        pr = jnp.exp2(s - m_new)
        vb = vx_ref[0, hh, pl.ds(start, tk), :]
        acc_ref[hh] = alpha * acc_ref[hh] + jnp.dot(pr.astype(BF16), vb,
                                                    preferred_element_type=F32)
        m_ref[hh] = m_new

    def step(j, slot, prefetch):
        if prefetch:
            nxt = pl.multiple_of((j + 1) * tk, tk)
            for hh in range(n_heads):
                scores(hh, nxt, 1 - slot)
        cur = pl.multiple_of(j * tk, tk)
        for hh in range(n_heads):
            consume(hh, cur, slot)

    for hh in range(n_heads):
        scores(hh, 0, 0)
        s = jnp.dot(q_ref[0, hh], kc_ref[0, hh], preferred_element_type=F32)
        m0 = jnp.max(s, axis=-1, keepdims=True)
        m_ref[hh] = m0
        acc_ref[hh] = jnp.dot(jnp.exp2(s - m0).astype(BF16), vc_ref[0, hh],
                              preferred_element_type=F32)

    def body(jj, carry):
        step(2 * jj, 0, True)
        step(2 * jj + 1, 1, True)
        return carry

    lax.fori_loop(0, n_kv // 2 - 1, body, 0)
    step(n_kv - 2, 0, True)
    step(n_kv - 1, 1, False)
    outs = []
    for hh in range(n_heads):
        acc = acc_ref[hh]
        outs.append(acc[:, :MLA_V] / acc[:, MLA_V:MLA_V + 1])
    o_ref[0] = jnp.concatenate(outs, axis=-1).astype(BF16)


def _attention(q, kx, vx, kc, vc, tq, tk):
    B, H, L, _ = q.shape
    Lc = kc.shape[3]
    return pl.pallas_call(
        functools.partial(_attn_kernel, tk=tk),
        grid=(B, H // 2, L // tq),
        in_specs=[
            pl.BlockSpec((1, 2, tq, LANES), lambda b, h, i: (b, h, i, 0)),
            pl.BlockSpec((1, 2, LANES, L), lambda b, h, i: (b, h, 0, 0)),
            pl.BlockSpec((1, 2, L, LANES), lambda b, h, i: (b, h, 0, 0)),
            pl.BlockSpec((1, 2, LANES, Lc), lambda b, h, i: (b, h, 0, 0)),
            pl.BlockSpec((1, 2, Lc, LANES), lambda b, h, i: (b, h, 0, 0)),
        ],
        out_specs=pl.BlockSpec((1, tq, LANES), lambda b, h, i: (b, i, h)),
        out_shape=jax.ShapeDtypeStruct((B, L, H * MLA_V), BF16),
        scratch_shapes=[pltpu.VMEM((2, tq, 1), F32), pltpu.VMEM((2, tq, LANES), F32),
                        pltpu.VMEM((2, 2, tq, tk), F32)],
        compiler_params=_cparams(("arbitrary", "arbitrary", "arbitrary")),
        name="attention",
    )(q, kx, vx, kc, vc)


def _ret_kernel(lg_ref, lgf_ref, lgb_ref, lgvf_ref, lgvb_ref, gout_ref, rq_ref, rk_ref, rv_ref, rg_ref,
                kc_ref, vc_ref, o_ref, dm_ref, qdf_ref, qdb_ref, kdf_ref, kdb_ref,
                f_ref, r_ref, rs_ref, *, n_steps, C):
    n_sub = rq_ref.shape[1] // C
    Lc = kc_ref.shape[1]
    ps = pl.program_id(1)
    i = pl.program_id(2)
    n_pairs = RET_HEADS // 2
    pw = 2 * RET_DK
    vw = 2 * RET_DV
    tdn = (((0,), (0,)), ((), ()))
    ndn = (((1,), (1,)), ((), ()))
    lgf = lgf_ref[...]
    lgb = lgb_ref[...]

    @pl.when((pl.program_id(0) == 0) & (ps == 0) & (i == 0))
    def _tables():
        a = lax.broadcasted_iota(jnp.int32, (C, C), 0)
        b = lax.broadcasted_iota(jnp.int32, (C, C), 1)
        dab = (a - b).astype(F32)
        for hd in range(RET_HEADS):
            fwd = jnp.where(a >= b, jnp.exp(jnp.where(a >= b, dab, 0.0) * lg_ref[0, hd]), 0.0)
            bwd = jnp.where(b >= a, jnp.exp(jnp.where(b >= a, -dab, 0.0) * lg_ref[1, hd]), 0.0)
            dm_ref[hd] = fwd + bwd
        row = lax.broadcasted_iota(jnp.int32, (C, 1), 0).astype(F32)
        qdf_ref[...] = jnp.exp((row + 1.0) * lgf)
        qdb_ref[...] = jnp.exp((C - row) * lgb)
        kdf_ref[...] = jnp.exp((C - 1.0 - row) * lgf)
        kdb_ref[...] = jnp.exp(row * lgb)

    @pl.when((ps == 0) & (i == 0))
    def _init_states():
        rowc = lax.broadcasted_iota(jnp.int32, (Lc, 1), 0).astype(F32)
        wf = jnp.exp((Lc - 1.0 - rowc) * lgf)
        wb = jnp.exp(rowc * lgb)
        kc = kc_ref[0].astype(F32)
        for pr in range(n_pairs):
            kp = kc[:, pr * pw:(pr + 1) * pw]
            vp = vc_ref[0, :, pr * vw:(pr + 1) * vw]
            f_ref[pr] = lax.dot_general((kp * wf[:, pr * pw:(pr + 1) * pw]).astype(BF16), vp, tdn,
                                        preferred_element_type=F32)
            r_ref[pr] = lax.dot_general((kp * wb[:, pr * pw:(pr + 1) * pw]).astype(BF16), vp, tdn,
                                        preferred_element_type=F32)

    @pl.when(ps == 0)
    def _backward_states():
        cdb = jnp.exp(C * lgvb_ref[...])
        for cc in reversed(range(n_sub)):
            rows = slice(cc * C, (cc + 1) * C)
            c = (n_steps - 1 - i) * n_sub + cc
            k = rk_ref[0, rows, :].astype(F32)
            for pr in range(n_pairs):
                r_old = r_ref[pr]
                rs_ref[c, pr] = r_old.astype(BF16)
                kp = (k[:, pr * pw:(pr + 1) * pw] * kdb_ref[:, pr * pw:(pr + 1) * pw]).astype(BF16)
                vp = rv_ref[0, rows, pr * vw:(pr + 1) * vw]
                upd = lax.dot_general(kp, vp, tdn, preferred_element_type=F32)
                r_ref[pr] = r_old * cdb[:, pr * vw:(pr + 1) * vw] + upd

    @pl.when(ps == 1)
    def _forward_outputs():
        cdf = jnp.exp(C * lgvf_ref[...])
        lane = lax.broadcasted_iota(jnp.int32, (1, pw), 1)
        for cc in range(n_sub):
            rows = slice(cc * C, (cc + 1) * C)
            c = i * n_sub + cc
            q = rq_ref[0, rows, :].astype(F32)
            k = rk_ref[0, rows, :].astype(F32)
            for pr in range(n_pairs):
                sl = slice(pr * pw, (pr + 1) * pw)
                qp = q[:, sl]
                kpb = rk_ref[0, rows, sl]
                qf = qp * qdf_ref[:, sl]
                qb = qp * qdb_ref[:, sl]
                fb = f_ref[pr].astype(BF16)
                rb = rs_ref[c, pr]
                for hh in range(2):
                    hd = 2 * pr + hh
                    hm = (lane // RET_DK) == hh
                    vs = slice(hd * RET_DV, (hd + 1) * RET_DV)
                    fs = slice(hh * RET_DV, (hh + 1) * RET_DV)
                    a = lax.dot_general(jnp.where(hm, qp, 0.0).astype(BF16), kpb, ndn,
                                        preferred_element_type=F32)
                    a = (a * dm_ref[hd]).astype(BF16)
                    o = jnp.dot(a, rv_ref[0, rows, vs], preferred_element_type=F32)
                    o = o + jnp.dot(jnp.where(hm, qf, 0.0).astype(BF16), fb[:, fs],
                                    preferred_element_type=F32)
                    o = o + jnp.dot(jnp.where(hm, qb, 0.0).astype(BF16), rb[:, fs],
                                    preferred_element_type=F32)
                    o = o * lax.rsqrt(jnp.mean(o * o, axis=-1, keepdims=True) + EPS) * gout_ref[:, vs]
                    g = rg_ref[0, rows, vs]
                    o_ref[0, rows, vs] = (o * (g * _sigmoid(g))).astype(BF16)
                kp = (k[:, sl] * kdf_ref[:, sl]).astype(BF16)
                vp = rv_ref[0, rows, pr * vw:(pr + 1) * vw]
                upd = lax.dot_general(kp, vp, tdn, preferred_element_type=F32)
                f_ref[pr] = f_ref[pr] * cdf[:, pr * vw:(pr + 1) * vw] + upd


def _retention(log_gamma, g_ret_out, rq, rk, rv, rg, rk_c, rv_c, C, R):
    B, L, n_qk = rq.shape
    n_v = rv.shape[2]
    Lc = rk_c.shape[1]
    n = L // R
    lgf = jnp.repeat(log_gamma[0], RET_DK)[None, :]
    lgb = jnp.repeat(log_gamma[1], RET_DK)[None, :]
    lgvf = jnp.repeat(log_gamma[0], RET_DV)[None, :]
    lgvb = jnp.repeat(log_gamma[1], RET_DV)[None, :]
    chunk = lambda p, i: jnp.where(p == 0, n - 1 - i, i)
    fwd_only = lambda p, i: jnp.where(p == 0, 0, i)
    const2 = lambda b, p, i: (0, 0)
    n_pairs = RET_HEADS // 2
    return pl.pallas_call(
        functools.partial(_ret_kernel, n_steps=n, C=C),
        grid=(B, 2, n),
        in_specs=[
            pl.BlockSpec(memory_space=pltpu.SMEM),
            pl.BlockSpec((1, n_qk), const2),
            pl.BlockSpec((1, n_qk), const2),
            pl.BlockSpec((1, n_v), const2),
            pl.BlockSpec((1, n_v), const2),
            pl.BlockSpec((1, n_v), const2),
            pl.BlockSpec((1, R, n_qk), lambda b, p, i: (b, fwd_only(p, i), 0)),
            pl.BlockSpec((1, R, n_qk), lambda b, p, i: (b, chunk(p, i), 0)),
            pl.BlockSpec((1, R, n_v), lambda b, p, i: (b, chunk(p, i), 0)),
            pl.BlockSpec((1, R, n_v), lambda b, p, i: (b, fwd_only(p, i), 0)),
            pl.BlockSpec((1, Lc, n_qk), lambda b, p, i: (b, 0, 0)),
            pl.BlockSpec((1, Lc, n_v), lambda b, p, i: (b, 0, 0)),
        ],
        out_specs=pl.BlockSpec((1, R, n_v), lambda b, p, i: (b, fwd_only(p, i), 0)),
        out_shape=jax.ShapeDtypeStruct((B, L, n_v), BF16),
        scratch_shapes=[
            pltpu.VMEM((RET_HEADS, C, C), F32),
            pltpu.VMEM((C, n_qk), F32), pltpu.VMEM((C, n_qk), F32),
            pltpu.VMEM((C, n_qk), F32), pltpu.VMEM((C, n_qk), F32),
            pltpu.VMEM((n_pairs, 2 * RET_DK, 2 * RET_DV), F32),
            pltpu.VMEM((n_pairs, 2 * RET_DK, 2 * RET_DV), F32),
            pltpu.VMEM((L // C, n_pairs, 2 * RET_DK, 2 * RET_DV), BF16),
        ],
        compiler_params=_cparams(("arbitrary", "arbitrary", "arbitrary")),
        name="retention",
    )(log_gamma, lgf, lgb, lgvf, lgvb, g_ret_out, rq, rk, rv, rg, rk_c, rv_c)


def _out_router_kernel(mla_ref, ret_ref, x_ref, mod_ref, wo_ref, gffn_ref, wr_ref, br_ref,
                       x1_ref, hf_ref, idx_ref, rank_ref, gate_ref, cnt_ref):
    tm = x_ref.shape[0]
    n_mla = mla_ref.shape[1]
    y = jnp.dot(mla_ref[...], wo_ref[0:n_mla, :], preferred_element_type=F32)
    y = y + jnp.dot(ret_ref[...], wo_ref[n_mla:, :], preferred_element_type=F32)
    x1 = x_ref[...] + mod_ref[0, 2:3, :] * y
    x1_ref[...] = x1
    ms = jnp.mean(x1 * x1, axis=-1, keepdims=True)
    hf = x1 * lax.rsqrt(ms + EPS) * gffn_ref[...]
    hf = hf * (1.0 + mod_ref[0, 4:5, :]) + mod_ref[0, 3:4, :]
    for c in range(hf.shape[1] // LANES):
        hf_ref[pl.ds(c, tm, stride=SUBLANES), :] = hf[:, c * LANES:(c + 1) * LANES]

    hi = hf.astype(BF16)
    lo = (hf - hi.astype(F32)).astype(BF16)
    both = jnp.dot(hi, wr_ref[...], preferred_element_type=F32)
    logits = (both[:, :LANES] + both[:, LANES:]
              + jnp.dot(lo, wr_ref[:, :LANES], preferred_element_type=F32)) + br_ref[...]
    lt = logits.T[0:N_EXPERTS, :]
    e_iota = lax.broadcasted_iota(jnp.int32, (N_EXPERTS, tm), 0).astype(F32)
    vals, idxs, hits = [], [], []
    for _ in range(TOP_K):
        mx = jnp.max(lt, axis=0, keepdims=True)
        ix = jnp.min(jnp.where(lt == mx, e_iota, float(N_EXPERTS)), axis=0, keepdims=True)
        hit = e_iota == ix
        lt = jnp.where(hit, -jnp.inf, lt)
        vals.append(mx)
        idxs.append(ix)
        hits.append(hit.astype(F32))

    s_iota = lax.broadcasted_iota(jnp.int32, (tm, tm), 0)
    t_iota = lax.broadcasted_iota(jnp.int32, (tm, tm), 1)
    upper = (s_iota <= t_iota).astype(F32).astype(BF16)
    prefix = jnp.dot(jnp.concatenate(hits, axis=0).astype(BF16), upper,
                     preferred_element_type=F32)
    ranks = []
    seen = jnp.zeros((N_EXPERTS, 1), F32)
    for k in range(TOP_K):
        pk = prefix[k * N_EXPERTS:(k + 1) * N_EXPERTS, :]
        rank = jnp.sum(hits[k] * (pk - 1.0 + seen), axis=0, keepdims=True)
        seen = seen + jnp.sum(hits[k], axis=1, keepdims=True)
        ranks.append(rank.astype(jnp.int32))
    ex = [jnp.exp(v - vals[0]) for v in vals]
    den = ex[0] + ex[1] + ex[2] + ex[3]
    idx_ref[...] = jnp.concatenate(idxs, axis=0).astype(jnp.int32)
    rank_ref[...] = jnp.concatenate(ranks, axis=0)
    gate_ref[...] = jnp.concatenate([e / den for e in ex], axis=0)
    cnt_ref[0] = jnp.broadcast_to(seen, (N_EXPERTS, LANES)).astype(jnp.int32)


def _out_router(mla, ret, x2, mod3, w_out_b, g_ffn, w_r, b_r, L, tm):
    T, D = x2.shape
    n_tiles = T // tm
    per_b = L // tm
    const = lambda i: (0, 0)
    return pl.pallas_call(
        _out_router_kernel,
        grid=(n_tiles,),
        in_specs=[
            pl.BlockSpec((tm, mla.shape[1]), lambda i: (i, 0)),
            pl.BlockSpec((tm, ret.shape[1]), lambda i: (i, 0)),
            pl.BlockSpec((tm, D), lambda i: (i, 0)),
            pl.BlockSpec((1, N_MOD, D), lambda i: (i // per_b, 0, 0)),
            pl.BlockSpec(w_out_b.shape, const),
            pl.BlockSpec((1, D), const),
            pl.BlockSpec(w_r.shape, const),
            pl.BlockSpec((1, LANES), const),
        ],
        out_specs=[
            pl.BlockSpec((tm, D), lambda i: (i, 0)),
            pl.BlockSpec((tm * SUBLANES, LANES), lambda i: (i, 0)),
            pl.BlockSpec((TOP_K, tm), lambda i: (0, i)),
            pl.BlockSpec((TOP_K, tm), lambda i: (0, i)),
            pl.BlockSpec((TOP_K, tm), lambda i: (0, i)),
            pl.BlockSpec((1, N_EXPERTS, LANES), lambda i: (i, 0, 0)),
        ],
        out_shape=[
            jax.ShapeDtypeStruct((T, D), F32),
            jax.ShapeDtypeStruct((T * SUBLANES, LANES), F32),
            jax.ShapeDtypeStruct((TOP_K, T), jnp.int32),
            jax.ShapeDtypeStruct((TOP_K, T), jnp.int32),
            jax.ShapeDtypeStruct((TOP_K, T), F32),
            jax.ShapeDtypeStruct((n_tiles, N_EXPERTS, LANES), jnp.int32),
        ],
        compiler_params=_cparams(("arbitrary",)),
        name="out_router",
    )(mla, ret, x2, mod3, w_out_b, g_ffn, w_r, b_r)


def _row_copy(src, dst, sem):
    return pltpu.make_async_copy(src, dst, sem)


def _dispatch_kernel(dest_ref, hf_ref, w1_ref, w2_ref, xs_ref, g_ref, l_ref, w2b_ref, sem, *, td):
    def issue(t, carry):
        src = hf_ref.at[pl.ds(pl.multiple_of(t * SUBLANES, SUBLANES), SUBLANES), :]
        for k in range(TOP_K):
            d = dest_ref[t * TOP_K + k]
            dst = xs_ref.at[pl.ds(pl.multiple_of(d * SUBLANES, SUBLANES), SUBLANES), :]
            _row_copy(src, dst, sem).start(priority=k % 2)
        return carry

    lax.fori_loop(0, td, issue, 0, unroll=ISSUE_UNROLL)

    w = w1_ref[0].astype(BF16)
    sub = 2 * LANES
    r = lax.broadcasted_iota(jnp.int32, (sub, sub), 0)
    c = lax.broadcasted_iota(jnp.int32, (sub, sub), 1)
    src_col = jnp.where(c < LANES, 2 * c, 2 * (c - LANES) + 1)
    sel = (r == src_col).astype(F32).astype(BF16)
    for s in range(w.shape[1] // sub):
        t = jnp.dot(w[:, s * sub:(s + 1) * sub], sel, preferred_element_type=F32).astype(BF16)
        g_ref[0, :, s * LANES:(s + 1) * LANES] = t[:, :LANES]
        l_ref[0, :, s * LANES:(s + 1) * LANES] = t[:, LANES:]
    w2b_ref[0] = w2_ref[0].astype(BF16)

    for k in range(TOP_K):
        _row_copy(hf_ref, xs_ref.at[pl.ds(0, td * SUBLANES), :], sem).wait()


def _dispatch_prep(dest_flat, hf8, w1, w2, n_rows):
    T = hf8.shape[0] // SUBLANES
    E, d, f2 = w1.shape
    f = f2 // 2
    halves = 2
    n = E * halves
    assert T % (n * SUBLANES) == 0
    td = T // n
    half_w1 = jax.ShapeDtypeStruct((E, d, f), BF16)
    return pl.pallas_call(
        functools.partial(_dispatch_kernel, td=td),
        grid=(n,),
        in_specs=[
            pl.BlockSpec((td * TOP_K,), lambda i: (i,), memory_space=pltpu.SMEM),
            pl.BlockSpec((td * SUBLANES, LANES), lambda i: (i, 0)),
            pl.BlockSpec((1, d, f2 // halves), lambda i: (i // halves, 0, i % halves)),
            pl.BlockSpec((1, f // halves, d), lambda i: (i // halves, i % halves, 0)),
        ],
        out_specs=[
            pl.BlockSpec(memory_space=pl.ANY),
            pl.BlockSpec((1, d, f // halves), lambda i: (i // halves, 0, i % halves)),
            pl.BlockSpec((1, d, f // halves), lambda i: (i // halves, 0, i % halves)),
            pl.BlockSpec((1, f // halves, d), lambda i: (i // halves, i % halves, 0)),
        ],
        out_shape=[jax.ShapeDtypeStruct((n_rows * SUBLANES, LANES), F32), half_w1, half_w1,
                   jax.ShapeDtypeStruct(w2.shape, BF16)],
        scratch_shapes=[pltpu.SemaphoreType.DMA],
        compiler_params=_cparams(("arbitrary",)),
        name="dispatch_prep",
    )(dest_flat, hf8, w1, w2)


def _experts_kernel(ie_ref, ib_ref, lo_ref, hi_ref, first_ref, ni_ref, xs_ref, w1g_ref, w1l_ref,
                    w2_ref, b1g_ref, b1l_ref, b2_ref, ys_ref):
    i = pl.program_id(0)
    blk = xs_ref.shape[0] // SUBLANES
    d = w1g_ref.shape[1]

    @pl.when(i < ni_ref[0])
    def _():
        cols = [xs_ref[pl.ds(c, blk, stride=SUBLANES), :] for c in range(d // LANES)]
        x = jnp.concatenate(cols, axis=-1)
        row = lax.broadcasted_iota(jnp.int32, (blk, 1), 0)
        mine = (row >= lo_ref[i]) & (row < hi_ref[i])
        x = jnp.where(mine, x, 0.0).astype(BF16)
        hg = jnp.dot(x, w1g_ref[0], preferred_element_type=F32) + b1g_ref[0]
        hl = jnp.dot(x, w1l_ref[0], preferred_element_type=F32) + b1l_ref[0]
        glu = jnp.minimum(hg, SWIGLU_LIMIT)
        lin = jnp.clip(hl, -SWIGLU_LIMIT, SWIGLU_LIMIT)
        act = glu * _sigmoid(SWIGLU_ALPHA * glu) * (lin + 1.0)
        y = jnp.dot(act.astype(BF16), w2_ref[0], preferred_element_type=F32) + b2_ref[0]
        y = jnp.where(mine, y, 0.0)

        @pl.when(first_ref[i] == 1)
        def _():
            for c in range(d // LANES):
                ys_ref[pl.ds(c, blk, stride=SUBLANES), :] = y[:, c * LANES:(c + 1) * LANES]

        @pl.when(first_ref[i] == 0)
        def _():
            for c in range(d // LANES):
                ys_ref[pl.ds(c, blk, stride=SUBLANES), :] += y[:, c * LANES:(c + 1) * LANES]


def _experts(items, xs, w1g, w1l, w2, b1g, b1l, b2, blk):
    item_e, item_blk, item_lo, item_hi, item_first, n_items = items
    d = w1g.shape[1]
    f = w1g.shape[2]
    row_map = lambda i, ie, ib, lo, hi, fi, ni: (ib[i], 0)
    exp_map = lambda i, ie, ib, lo, hi, fi, ni: (ie[i], 0, 0)
    grid_spec = pltpu.PrefetchScalarGridSpec(
        num_scalar_prefetch=6,
        grid=(item_e.shape[0],),
        in_specs=[
            pl.BlockSpec((blk * SUBLANES, LANES), row_map),
            pl.BlockSpec((1, d, f), exp_map),
            pl.BlockSpec((1, d, f), exp_map),
            pl.BlockSpec((1, f, d), exp_map),
            pl.BlockSpec((1, 1, f), exp_map),
            pl.BlockSpec((1, 1, f), exp_map),
            pl.BlockSpec((1, 1, d), exp_map),
        ],
        out_specs=pl.BlockSpec((blk * SUBLANES, LANES), row_map),
    )
    return pl.pallas_call(
        _experts_kernel,
        grid_spec=grid_spec,
        out_shape=jax.ShapeDtypeStruct(xs.shape, F32),
        compiler_params=_cparams(("arbitrary",)),
        name="experts",
    )(item_e, item_blk, item_lo, item_hi, item_first, n_items, xs, w1g, w1l, w2, b1g, b1l, b2)


def _combine_kernel(dest_ref, dnext_ref, x1_ref, gate_ref, mod_ref, ys_ref, o_ref, buf_ref, sems,
                    *, tc):
    i = pl.program_id(0)
    n = pl.num_programs(0)
    slot = i % 2

    def gather(idx_ref, s):
        def issue(t, carry):
            for k in range(TOP_K):
                d = idx_ref[t * TOP_K + k]
                src = ys_ref.at[pl.ds(pl.multiple_of(d * SUBLANES, SUBLANES), SUBLANES), :]
                dst = buf_ref.at[s, k, pl.ds(pl.multiple_of(t * SUBLANES, SUBLANES), SUBLANES), :]
                _row_copy(src, dst, sems.at[s]).start(priority=k % 2)
            return carry
        lax.fori_loop(0, tc, issue, 0, unroll=ISSUE_UNROLL)

    @pl.when(i == 0)
    def _():
        gather(dest_ref, 0)

    @pl.when(i + 1 < n)
    def _():
        gather(dnext_ref, 1 - slot)

    for k in range(TOP_K):
        _row_copy(ys_ref.at[pl.ds(0, tc * SUBLANES), :], buf_ref.at[slot, k], sems.at[slot]).wait()

    g = gate_ref[...]
    for c in range(o_ref.shape[1] // LANES):
        cs = slice(c * LANES, (c + 1) * LANES)
        acc = g[:, 0:1] * buf_ref[slot, 0, pl.ds(c, tc, stride=SUBLANES), :]
        for k in range(1, TOP_K):
            acc = acc + g[:, k:k + 1] * buf_ref[slot, k, pl.ds(c, tc, stride=SUBLANES), :]
        o_ref[:, cs] = x1_ref[:, cs] + mod_ref[0, 5:6, cs] * acc


def _combine(dest_flat, x1, gates_t, mod3, ys, L, tc):
    T, D = x1.shape
    per_b = L // tc
    n = T // tc
    return pl.pallas_call(
        functools.partial(_combine_kernel, tc=tc),
        grid=(n,),
        in_specs=[
            pl.BlockSpec((tc * TOP_K,), lambda i: (i,), memory_space=pltpu.SMEM),
            pl.BlockSpec((tc * TOP_K,), lambda i: (jnp.minimum(i + 1, n - 1),),
                         memory_space=pltpu.SMEM),
            pl.BlockSpec((tc, D), lambda i: (i, 0)),
            pl.BlockSpec((tc, TOP_K), lambda i: (i, 0)),
            pl.BlockSpec((1, N_MOD, D), lambda i: (i // per_b, 0, 0)),
            pl.BlockSpec(memory_space=pl.ANY),
        ],
        out_specs=pl.BlockSpec((tc, D), lambda i: (i, 0)),
        out_shape=jax.ShapeDtypeStruct((T, D), F32),
        scratch_shapes=[pltpu.VMEM((2, TOP_K, tc * SUBLANES, LANES), F32),
                        pltpu.SemaphoreType.DMA((2,))],
        compiler_params=_cparams(("arbitrary",)),
        name="combine",
    )(dest_flat, dest_flat, x1, gates_t, mod3, ys)


def _rope_tables(L, dim, lane_off, width):
    rows = L // GRID_W
    nf = dim // 4
    inv = jnp.power(ROPE_BASE, -jnp.arange(nf, dtype=F32) / nf)
    row = jnp.repeat(jnp.arange(rows, dtype=F32), GRID_W)
    col = jnp.tile(jnp.arange(GRID_W, dtype=F32), rows)
    pos = jnp.stack([row, col], axis=-1)
    ang = pos[:, :, None] * inv
    ang = jnp.broadcast_to(ang[:, :, None, :], (L, 2, 2, nf)).reshape(L, dim)
    sign = jnp.where((jnp.arange(dim) % (dim // 2)) < nf, -1.0, 1.0).astype(F32)
    cos, sin = jnp.cos(ang), jnp.sin(ang) * sign
    if lane_off is None:
        reps = width // dim
        return jnp.tile(cos, (1, reps)), jnp.tile(sin, (1, reps))
    cfull = jnp.ones((L, width), F32).at[:, lane_off:lane_off + dim].set(cos)
    sfull = jnp.zeros((L, width), F32).at[:, lane_off:lane_off + dim].set(sin)
    return cfull, sfull


def _identity_tables(L):
    return jnp.ones((L, LANES), F32), jnp.zeros((L, LANES), F32)


def _half_rot_src(dim):
    j = jnp.arange(dim)
    return jnp.where((j % (dim // 2)) < dim // 4, j + dim // 4, j - dim // 4)


def _prep_weights(w_in, w_q_up, w_kv_up, g_q_head, g_k_head):
    D = w_in.shape[0]
    o = 0
    wq = w_in[:, o:o + Q_LORA]; o += Q_LORA
    wkv = w_in[:, o:o + KV_LORA]; o += KV_LORA
    wpe = w_in[:, o:o + MLA_ROPE]; o += MLA_ROPE
    n_qk = RET_HEADS * RET_DK
    wrq = w_in[:, o:o + n_qk]; o += n_qk
    wrk = w_in[:, o:o + n_qk]; o += n_qk
    rest = w_in[:, o:]
    src_m = _half_rot_src(MLA_ROPE)
    src_r = _half_rot_src(RET_DK)
    rope_lanes = slice(MLA_NOPE, MLA_NOPE + MLA_ROPE)
    pe_blk = jnp.zeros((D, LANES), w_in.dtype).at[:, rope_lanes].set(wpe)
    pe_perm = jnp.zeros((D, LANES), w_in.dtype).at[:, rope_lanes].set(wpe[:, src_m])
    perm_heads = lambda w: w.reshape(D, RET_HEADS, RET_DK)[:, :, src_r].reshape(D, n_qk)
    w_in_r = jnp.concatenate([wq, wkv, pe_blk, pe_perm, wrq, perm_heads(wrq), wrk, perm_heads(wrk),
                              rest], axis=1).astype(BF16)

    pad_h = LANES - MLA_QK
    wq3 = w_q_up.reshape(Q_LORA, MLA_HEADS, MLA_QK)
    q_main = jnp.pad(wq3, ((0, 0), (0, 0), (0, pad_h)))
    q_perm = jnp.zeros_like(q_main).at[:, :, rope_lanes].set(wq3[:, :, MLA_NOPE:][:, :, src_m])
    w_q_r = jnp.concatenate([q_main.reshape(Q_LORA, -1), q_perm.reshape(Q_LORA, -1)], axis=1).astype(BF16)

    kv = w_kv_up.reshape(KV_LORA, MLA_HEADS, MLA_NOPE + MLA_V)
    kpart = jnp.pad(kv[:, :, :MLA_NOPE], ((0, 0), (0, 0), (0, LANES - MLA_NOPE)))
    vpart = jnp.pad(kv[:, :, MLA_NOPE:], ((0, 0), (0, 0), (0, LANES - MLA_V)))
    w_kv_r = jnp.concatenate([kpart.reshape(KV_LORA, -1), vpart.reshape(KV_LORA, -1)], axis=1).astype(BF16)

    def gains(g):
        main = jnp.pad(g, (0, pad_h))[None, :]
        perm = jnp.zeros((1, LANES), g.dtype).at[0, rope_lanes].set(g[MLA_NOPE:][src_m])
        return main, perm

    gqh, gqr = gains(g_q_head)
    gkh, gkr = gains(g_k_head)
    return w_in_r, w_q_r, w_kv_r, gqh, gqr, gkh, gkr


def _routing_tables(idx, rank, counts, tm, blk):
    T = idx.shape[1]
    i32 = jnp.int32
    tot = jnp.sum(counts, axis=0)
    end = jnp.cumsum(tot)
    start = end - tot
    tile_base = start[None, :] + jnp.cumsum(counts, axis=0) - counts
    base_tok = jnp.repeat(tile_base, tm, axis=0)
    hit = idx[:, :, None] == jnp.arange(N_EXPERTS, dtype=i32)
    dest = jnp.sum(jnp.where(hit, base_tok[None], 0), axis=-1) + rank

    n_work = (T * TOP_K) // blk + N_EXPERTS
    first_blk = start // blk
    last_blk = (end - 1) // blk
    per_e = jnp.where(tot > 0, last_blk - first_blk + 1, 0)
    item_end = jnp.cumsum(per_e)
    item_start = item_end - per_e
    n_items = item_end[-1]
    j = jnp.minimum(jnp.arange(n_work, dtype=i32), n_items - 1)
    item_e = jnp.minimum(jnp.sum(item_end[None, :] <= j[:, None], axis=1), N_EXPERTS - 1).astype(i32)
    item_blk = first_blk[item_e] + j - item_start[item_e]
    item_lo = jnp.clip(start[item_e] - item_blk * blk, 0, blk)
    item_hi = jnp.clip(end[item_e] - item_blk * blk, 0, blk)
    prev_blk = jnp.concatenate([jnp.full((1,), -1, i32), item_blk[:-1].astype(i32)])
    item_first = (item_blk != prev_blk).astype(i32)
    items = (item_e, item_blk.astype(i32), item_lo.astype(i32), item_hi.astype(i32), item_first,
             n_items.astype(i32).reshape(1))
    return dest.astype(i32), items


def kernel(x, c, ctx, c_ctx, g_attn, g_ffn, w_ada, b_ada, w_in, g_q_lora, w_q_up, g_q_head,
           g_kv_lora, w_kv_up, g_k_head, ret_decay_logit, g_ret_out, w_out, w_router, b_router,
           w_mlp1, b_mlp1, w_mlp2, b_mlp2):
    B, L, D = x.shape
    Lc = ctx.shape[1]
    T = B * L
    l = 0
    assert w_ada.shape[0] == 1

    rows = ((B + 1 + SUBLANES - 1) // SUBLANES) * SUBLANES
    cc = jnp.zeros((rows, D), F32).at[:B].set(c).at[B].set(c_ctx)
    mod3 = _adaln(cc, w_ada[l], b_ada[l][None, :]).reshape(rows, N_MOD, D)

    w_in_r, w_q_r, w_kv_r, gqh, gqr, gkh, gkr = _prep_weights(
        w_in[l], w_q_up[l], w_kv_up[l], g_q_head[l], g_k_head[l])
    tabs_x = _rope_tables(L, MLA_ROPE, MLA_NOPE, LANES) + _rope_tables(L, RET_DK, None, LANES)
    tabs_c = _identity_tables(Lc) + _identity_tables(Lc)
    proj = functools.partial(_in_proj, g_attn=g_attn[l][None, :], w_in_r=w_in_r,
                             g_q_lora=g_q_lora[l][None, :], w_q_r=w_q_r, gqh=gqh, gqr=gqr,
                             g_kv_lora=g_kv_lora[l][None, :], w_kv_r=w_kv_r, gkh=gkh, gkr=gkr)
    q, kx, vx, rq, rk, rv, rg = proj(x, mod3, lambda b: b, tabs=tabs_x, tm=min(PROJ_TM, L))
    _, kc, vc, _, rk_c, rv_c, _ = proj(ctx, mod3, lambda b: B, tabs=tabs_c, tm=Lc)

    mla = _attention(q, kx, vx, kc, vc, min(ATT_TQ, L), min(ATT_TK, L // 2))

    log_gamma = jax.nn.log_sigmoid(ret_decay_logit[l].astype(F32))
    ret = _retention(log_gamma, g_ret_out[l][None, :], rq, rk, rv, rg, rk_c, rv_c,
                     min(RET_C, L), min(RET_ROWS, L))

    w_r32 = jnp.pad(w_router[l], ((0, 0), (0, LANES - N_EXPERTS)))
    w_r_hi = w_r32.astype(BF16)
    w_r = jnp.concatenate([w_r_hi, (w_r32 - w_r_hi.astype(F32)).astype(BF16)], axis=1)
    b_r = jnp.pad(b_router[l], (0, LANES - N_EXPERTS))[None, :]
    tm = min(OUT_TM, L)
    x1, hf8, idx, rank, gates, cnt = _out_router(
        mla.reshape(T, -1), ret.reshape(T, -1), x.reshape(T, D), mod3,
        w_out[l].astype(BF16), g_ffn[l][None, :], w_r, b_r, L, tm)

    blk = MOE_BLK
    assert (T * TOP_K) % blk == 0
    dest, items = _routing_tables(idx, rank, cnt[:, :, 0], tm, blk)
    dest_flat = dest.T.reshape(-1)

    xs, w1g, w1l, w2b = _dispatch_prep(dest_flat, hf8, w_mlp1[l], w_mlp2[l], T * TOP_K)
    b1g = b_mlp1[l][:, None, 0::2]
    b1l = b_mlp1[l][:, None, 1::2]
    ys = _experts(items, xs, w1g, w1l, w2b, b1g, b1l, b_mlp2[l][:, None, :], blk)

    out = _combine(dest_flat, x1, gates.T, mod3, ys, L, min(COMB_T, L))
    return out.reshape(B, L, D)
```

```python
import functools
import math

import jax
import jax.numpy as jnp
from jax import lax
from jax.experimental import pallas as pl
from jax.experimental.pallas import tpu as pltpu

F32 = jnp.float32
BF16 = jnp.bfloat16

LANES = 128
SUBLANES = 8
VMEM_LIMIT_BYTES = 56 * 1024 * 1024

EPS = 1e-6
ROPE_BASE = 10000.0
GRID_W = 64
N_MOD = 6
MLA_HEADS = 8
MLA_NOPE = 64
MLA_ROPE = 32
MLA_QK = MLA_NOPE + MLA_ROPE
MLA_V = 64
Q_LORA = 256
KV_LORA = 128
RET_HEADS = 4
RET_DK = 64
RET_DV = 128
N_EXPERTS = 32
TOP_K = 4
SWIGLU_LIMIT = 7.0
SWIGLU_ALPHA = 1.702
LOG2E = 1.4426950408889634

PROJ_TM = 512
ATT_TQ = 512
ATT_TK = 1024
RET_C = 256
RET_ROWS = 1024
OUT_TM = 512
MOE_BLK = 512
COMB_T = 256
ISSUE_UNROLL = 8


def _cparams(sem):
    return pltpu.CompilerParams(dimension_semantics=sem, vmem_limit_bytes=VMEM_LIMIT_BYTES)


def _sigmoid(x):
    return 1.0 / (1.0 + jnp.exp(-x))


def _adaln_kernel(c_ref, w_ref, b_ref, o_ref):
    c = c_ref[...]
    s = (c * _sigmoid(c)).astype(BF16)
    o_ref[...] = jnp.dot(s, w_ref[...].astype(BF16), preferred_element_type=F32) + b_ref[...]


def _adaln(cc, w_ada, b_ada):
    rows, d = cc.shape
    n = w_ada.shape[1]
    tn = 1536
    return pl.pallas_call(
        _adaln_kernel,
        grid=(n // tn,),
        in_specs=[pl.BlockSpec((rows, d), lambda j: (0, 0)),
                  pl.BlockSpec((d, tn), lambda j: (0, j)),
                  pl.BlockSpec((1, tn), lambda j: (0, j))],
        out_specs=pl.BlockSpec((rows, tn), lambda j: (0, j)),
        out_shape=jax.ShapeDtypeStruct((rows, n), F32),
        compiler_params=_cparams(("arbitrary",)),
        name="adaln",
    )(cc, w_ada, b_ada)


def _in_proj_kernel(x_ref, mod_ref, gattn_ref, win_ref, gql_ref, wq_ref, gqh_ref, gqr_ref,
                    gkvl_ref, wkv_ref, gkh_ref, gkr_ref, cm_ref, sm_ref, cr_ref, sr_ref,
                    q_ref, k_ref, v_ref, rq_ref, rk_ref, rv_ref, rg_ref, *, q_scale):
    x = x_ref[0]
    shift = mod_ref[0, 0:1, :]
    scale = mod_ref[0, 1:2, :]
    ms = jnp.mean(x * x, axis=-1, keepdims=True)
    h = x * lax.rsqrt(ms + EPS) * gattn_ref[...]
    h = h * (1.0 + scale) + shift
    p = jnp.dot(h.astype(BF16), win_ref[...], preferred_element_type=F32)

    lane = lax.broadcasted_iota(jnp.int32, (1, LANES), 1)
    cm, sm = cm_ref[...], sm_ref[...]
    cr, sr = cr_ref[...], sr_ref[...]
    n_hl = MLA_HEADS * LANES

    cq = p[:, 0:Q_LORA]
    cq = cq * lax.rsqrt(jnp.mean(cq * cq, axis=-1, keepdims=True) + EPS) * gql_ref[...]
    qf = jnp.dot(cq.astype(BF16), wq_ref[...], preferred_element_type=F32)
    gq_cos = gqh_ref[...] * cm
    gq_sin = gqr_ref[...] * sm
    for hd in range(MLA_HEADS):
        blk = qf[:, hd * LANES:(hd + 1) * LANES]
        perm = qf[:, n_hl + hd * LANES:n_hl + (hd + 1) * LANES]
        r = lax.rsqrt(jnp.sum(blk * blk, axis=-1, keepdims=True) * (1.0 / MLA_QK) + EPS)
        q_ref[0, hd] = ((blk * gq_cos + perm * gq_sin) * (r * q_scale)).astype(BF16)

    o_kv = Q_LORA
    ckv = p[:, o_kv:o_kv + KV_LORA]
    ckv = ckv * lax.rsqrt(jnp.mean(ckv * ckv, axis=-1, keepdims=True) + EPS) * gkvl_ref[...]
    kvf = jnp.dot(ckv.astype(BF16), wkv_ref[...], preferred_element_type=F32)
    o_pe = o_kv + KV_LORA
    pe = p[:, o_pe:o_pe + LANES]
    pe_perm = p[:, o_pe + LANES:o_pe + 2 * LANES]
    gk = gkh_ref[...]
    pe_rope = pe * (gk * cm) + pe_perm * (gkr_ref[...] * sm)
    pe_ss = jnp.sum(pe * pe, axis=-1, keepdims=True)
    ones_col = (lane == MLA_V).astype(F32)
    for hd in range(MLA_HEADS):
        kn = kvf[:, hd * LANES:(hd + 1) * LANES]
        ss = jnp.sum(kn * kn, axis=-1, keepdims=True) + pe_ss
        r = lax.rsqrt(ss * (1.0 / MLA_QK) + EPS)
        kk = (kn * gk + pe_rope) * r
        k_ref[0, hd] = kk.T.astype(BF16)
        vv = kvf[:, (MLA_HEADS + hd) * LANES:(MLA_HEADS + hd + 1) * LANES] + ones_col
        v_ref[0, hd] = vv.astype(BF16)

    n_qk = RET_HEADS * RET_DK
    o_rq = o_pe + 2 * LANES
    o_rk = o_rq + 2 * n_qk
    for j in range(n_qk // LANES):
        js = slice(j * LANES, (j + 1) * LANES)
        a = p[:, o_rq + j * LANES:o_rq + (j + 1) * LANES]
        b = p[:, o_rq + n_qk + j * LANES:o_rq + n_qk + (j + 1) * LANES]
        rq_ref[0, :, js] = (a * cr + b * sr).astype(BF16)
        a = p[:, o_rk + j * LANES:o_rk + (j + 1) * LANES]
        b = p[:, o_rk + n_qk + j * LANES:o_rk + n_qk + (j + 1) * LANES]
        rk_ref[0, :, js] = ((a * cr + b * sr) * (RET_DK ** -0.5)).astype(BF16)
    o_rv = o_rk + 2 * n_qk
    n_v = RET_HEADS * RET_DV
    rv_ref[0] = p[:, o_rv:o_rv + n_v].astype(BF16)
    rg_ref[0] = p[:, o_rv + n_v:o_rv + 2 * n_v]


def _in_proj(x, mod3, mod_row_of_batch, g_attn, w_in_r, g_q_lora, w_q_r, gqh, gqr, g_kv_lora,
             w_kv_r, gkh, gkr, tabs, tm):
    B, L, D = x.shape
    cm, sm, cr, sr = tabs
    n_in = w_in_r.shape[1]
    const = lambda b, i: (0, 0)
    tab_spec = pl.BlockSpec((tm, LANES), lambda b, i: (i, 0))
    head_spec = pl.BlockSpec((1, MLA_HEADS, tm, LANES), lambda b, i: (b, 0, i, 0))
    n_qk = RET_HEADS * RET_DK
    n_v = RET_HEADS * RET_DV
    seq_spec = lambda w: pl.BlockSpec((1, tm, w), lambda b, i: (b, i, 0))
    head_shape = jax.ShapeDtypeStruct((B, MLA_HEADS, L, LANES), BF16)
    q_scale = MLA_QK ** -0.5 * LOG2E
    return pl.pallas_call(
        functools.partial(_in_proj_kernel, q_scale=q_scale),
        grid=(B, L // tm),
        in_specs=[
            pl.BlockSpec((1, tm, D), lambda b, i: (b, i, 0)),
            pl.BlockSpec((1, N_MOD, D), lambda b, i: (mod_row_of_batch(b), 0, 0)),
            pl.BlockSpec((1, D), const),
            pl.BlockSpec((D, n_in), const),
            pl.BlockSpec((1, Q_LORA), const),
            pl.BlockSpec(w_q_r.shape, const),
            pl.BlockSpec((1, LANES), const),
            pl.BlockSpec((1, LANES), const),
            pl.BlockSpec((1, KV_LORA), const),
            pl.BlockSpec(w_kv_r.shape, const),
            pl.BlockSpec((1, LANES), const),
            pl.BlockSpec((1, LANES), const),
            tab_spec, tab_spec, tab_spec, tab_spec,
        ],
        out_specs=[head_spec,
                   pl.BlockSpec((1, MLA_HEADS, LANES, tm), lambda b, i: (b, 0, 0, i)),
                   head_spec,
                   seq_spec(n_qk), seq_spec(n_qk), seq_spec(n_v), seq_spec(n_v)],
        out_shape=[head_shape,
                   jax.ShapeDtypeStruct((B, MLA_HEADS, LANES, L), BF16),
                   head_shape,
                   jax.ShapeDtypeStruct((B, L, n_qk), BF16),
                   jax.ShapeDtypeStruct((B, L, n_qk), BF16),
                   jax.ShapeDtypeStruct((B, L, n_v), BF16),
                   jax.ShapeDtypeStruct((B, L, n_v), F32)],
        compiler_params=_cparams(("arbitrary", "arbitrary")),
        name="in_proj",
    )(x, mod3, g_attn, w_in_r, g_q_lora, w_q_r, gqh, gqr, g_kv_lora, w_kv_r, gkh, gkr,
      cm, sm, cr, sr)


def _attn_kernel(q_ref, kx_ref, vx_ref, kc_ref, vc_ref, o_ref, m_ref, acc_ref, s_ref, *, tk):
    n_kv = kx_ref.shape[3] // tk
    n_heads = q_ref.shape[1]

    def scores(hh, start, slot):
        kb = kx_ref[0, hh, :, pl.ds(start, tk)]
        s_ref[hh, slot] = jnp.dot(q_ref[0, hh], kb, preferred_element_type=F32)

    def consume(hh, start, slot):
        s = s_ref[hh, slot]
        m_old = m_ref[hh]
        m_new = jnp.maximum(m_old, jnp.max(s, axis=-1, keepdims=True))
        alpha = jnp.exp2(m_old - m_new)
        pr = jnp.exp2(s - m_new)
        vb = vx_ref[0, hh, pl.ds(start, tk), :]
        acc_ref[hh] = alpha * acc_ref[hh] + jnp.dot(pr.astype(BF16), vb,
                                                    preferred_element_type=F32)
        m_ref[hh] = m_new

    def step(j, slot, prefetch):
        if prefetch:
            nxt = pl.multiple_of((j + 1) * tk, tk)
            for hh in range(n_heads):
                scores(hh, nxt, 1 - slot)
        cur = pl.multiple_of(j * tk, tk)
        for hh in range(n_heads):
            consume(hh, cur, slot)

    for hh in range(n_heads):
        scores(hh, 0, 0)
        s = jnp.dot(q_ref[0, hh], kc_ref[0, hh], preferred_element_type=F32)
        m0 = jnp.max(s, axis=-1, keepdims=True)
        m_ref[hh] = m0
        acc_ref[hh] = jnp.dot(jnp.exp2(s - m0).astype(BF16), vc_ref[0, hh],
                              preferred_element_type=F32)

    def body(jj, carry):
        step(2 * jj, 0, True)
        step(2 * jj + 1, 1, True)
        return carry

    lax.fori_loop(0, n_kv // 2 - 1, body, 0)
    step(n_kv - 2, 0, True)
    step(n_kv - 1, 1, False)
    outs = []
    for hh in range(n_heads):
        acc = acc_ref[hh]
        outs.append(acc[:, :MLA_V] / acc[:, MLA_V:MLA_V + 1])
    o_ref[0] = jnp.concatenate(outs, axis=-1).astype(BF16)


def _attention(q, kx, vx, kc, vc, tq, tk):
    B, H, L, _ = q.shape
    Lc = kc.shape[3]
    return pl.pallas_call(
        functools.partial(_attn_kernel, tk=tk),
        grid=(B, H // 2, L // tq),
        in_specs=[
            pl.BlockSpec((1, 2, tq, LANES), lambda b, h, i: (b, h, i, 0)),
            pl.BlockSpec((1, 2, LANES, L), lambda b, h, i: (b, h, 0, 0)),
            pl.BlockSpec((1, 2, L, LANES), lambda b, h, i: (b, h, 0, 0)),
            pl.BlockSpec((1, 2, LANES, Lc), lambda b, h, i: (b, h, 0, 0)),
            pl.BlockSpec((1, 2, Lc, LANES), lambda b, h, i: (b, h, 0, 0)),
        ],
        out_specs=pl.BlockSpec((1, tq, LANES), lambda b, h, i: (b, i, h)),
        out_shape=jax.ShapeDtypeStruct((B, L, H * MLA_V), BF16),
        scratch_shapes=[pltpu.VMEM((2, tq, 1), F32), pltpu.VMEM((2, tq, LANES), F32),
                        pltpu.VMEM((2, 2, tq, tk), F32)],
        compiler_params=_cparams(("arbitrary", "arbitrary", "arbitrary")),
        name="attention",
    )(q, kx, vx, kc, vc)


def _ret_kernel(lg_ref, lgf_ref, lgb_ref, lgvf_ref, lgvb_ref, gout_ref, rq_ref, rk_ref, rv_ref, rg_ref,
                kc_ref, vc_ref, o_ref, dm_ref, qdf_ref, qdb_ref, kdf_ref, kdb_ref,
                f_ref, r_ref, rs_ref, *, n_steps, C):
    n_sub = rq_ref.shape[1] // C
    Lc = kc_ref.shape[1]
    ps = pl.program_id(1)
    i = pl.program_id(2)
    n_pairs = RET_HEADS // 2
    pw = 2 * RET_DK
    vw = 2 * RET_DV
    tdn = (((0,), (0,)), ((), ()))
    ndn = (((1,), (1,)), ((), ()))
    lgf = lgf_ref[...]
    lgb = lgb_ref[...]

    @pl.when((pl.program_id(0) == 0) & (ps == 0) & (i == 0))
    def _tables():
        a = lax.broadcasted_iota(jnp.int32, (C, C), 0)
        b = lax.broadcasted_iota(jnp.int32, (C, C), 1)
        dab = (a - b).astype(F32)
        for hd in range(RET_HEADS):
            fwd = jnp.where(a >= b, jnp.exp(jnp.where(a >= b, dab, 0.0) * lg_ref[0, hd]), 0.0)
            bwd = jnp.where(b >= a, jnp.exp(jnp.where(b >= a, -dab, 0.0) * lg_ref[1, hd]), 0.0)
            dm_ref[hd] = fwd + bwd
        row = lax.broadcasted_iota(jnp.int32, (C, 1), 0).astype(F32)
        qdf_ref[...] = jnp.exp((row + 1.0) * lgf)
        qdb_ref[...] = jnp.exp((C - row) * lgb)
        kdf_ref[...] = jnp.exp((C - 1.0 - row) * lgf)
        kdb_ref[...] = jnp.exp(row * lgb)

    @pl.when((ps == 0) & (i == 0))
    def _init_states():
        rowc = lax.broadcasted_iota(jnp.int32, (Lc, 1), 0).astype(F32)
        wf = jnp.exp((Lc - 1.0 - rowc) * lgf)
        wb = jnp.exp(rowc * lgb)
        kc = kc_ref[0].astype(F32)
        for pr in range(n_pairs):
            kp = kc[:, pr * pw:(pr + 1) * pw]
            vp = vc_ref[0, :, pr * vw:(pr + 1) * vw]
            f_ref[pr] = lax.dot_general((kp * wf[:, pr * pw:(pr + 1) * pw]).astype(BF16), vp, tdn,
                                        preferred_element_type=F32)
            r_ref[pr] = lax.dot_general((kp * wb[:, pr * pw:(pr + 1) * pw]).astype(BF16), vp, tdn,
                                        preferred_element_type=F32)

    @pl.when(ps == 0)
    def _backward_states():
        cdb = jnp.exp(C * lgvb_ref[...])
        for cc in reversed(range(n_sub)):
            rows = slice(cc * C, (cc + 1) * C)
            c = (n_steps - 1 - i) * n_sub + cc
            k = rk_ref[0, rows, :].astype(F32)
            for pr in range(n_pairs):
                r_old = r_ref[pr]
                rs_ref[c, pr] = r_old.astype(BF16)
                kp = (k[:, pr * pw:(pr + 1) * pw] * kdb_ref[:, pr * pw:(pr + 1) * pw]).astype(BF16)
                vp = rv_ref[0, rows, pr * vw:(pr + 1) * vw]
                upd = lax.dot_general(kp, vp, tdn, preferred_element_type=F32)
                r_ref[pr] = r_old * cdb[:, pr * vw:(pr + 1) * vw] + upd

    @pl.when(ps == 1)
    def _forward_outputs():
        cdf = jnp.exp(C * lgvf_ref[...])
        lane = lax.broadcasted_iota(jnp.int32, (1, pw), 1)
        for cc in range(n_sub):
            rows = slice(cc * C, (cc + 1) * C)
            c = i * n_sub + cc
            q = rq_ref[0, rows, :].astype(F32)
            k = rk_ref[0, rows, :].astype(F32)
            for pr in range(n_pairs):
                sl = slice(pr * pw, (pr + 1) * pw)
                qp = q[:, sl]
                kpb = rk_ref[0, rows, sl]
                qf = qp * qdf_ref[:, sl]
                qb = qp * qdb_ref[:, sl]
                fb = f_ref[pr].astype(BF16)
                rb = rs_ref[c, pr]
                for hh in range(2):
                    hd = 2 * pr + hh
                    hm = (lane // RET_DK) == hh
                    vs = slice(hd * RET_DV, (hd + 1) * RET_DV)
                    fs = slice(hh * RET_DV, (hh + 1) * RET_DV)
                    a = lax.dot_general(jnp.where(hm, qp, 0.0).astype(BF16), kpb, ndn,
                                        preferred_element_type=F32)
                    a = (a * dm_ref[hd]).astype(BF16)
                    o = jnp.dot(a, rv_ref[0, rows, vs], preferred_element_type=F32)
                    o = o + jnp.dot(jnp.where(hm, qf, 0.0).astype(BF16), fb[:, fs],
                                    preferred_element_type=F32)
                    o = o + jnp.dot(jnp.where(hm, qb, 0.0).astype(BF16), rb[:, fs],
                                    preferred_element_type=F32)
                    o = o * lax.rsqrt(jnp.mean(o * o, axis=-1, keepdims=True) + EPS) * gout_ref[:, vs]
                    g = rg_ref[0, rows, vs]
                    o_ref[0, rows, vs] = (o * (g * _sigmoid(g))).astype(BF16)
                kp = (k[:, sl] * kdf_ref[:, sl]).astype(BF16)
                vp = rv_ref[0, rows, pr * vw:(pr + 1) * vw]
                upd = lax.dot_general(kp, vp, tdn, preferred_element_type=F32)
                f_ref[pr] = f_ref[pr] * cdf[:, pr * vw:(pr + 1) * vw] + upd


def _retention(log_gamma, g_ret_out, rq, rk, rv, rg, rk_c, rv_c, C, R):
    B, L, n_qk = rq.shape
    n_v = rv.shape[2]
    Lc = rk_c.shape[1]
    n = L // R
    lgf = jnp.repeat(log_gamma[0], RET_DK)[None, :]
    lgb = jnp.repeat(log_gamma[1], RET_DK)[None, :]
    lgvf = jnp.repeat(log_gamma[0], RET_DV)[None, :]
    lgvb = jnp.repeat(log_gamma[1], RET_DV)[None, :]
    chunk = lambda p, i: jnp.where(p == 0, n - 1 - i, i)
    fwd_only = lambda p, i: jnp.where(p == 0, 0, i)
    const2 = lambda b, p, i: (0, 0)
    n_pairs = RET_HEADS // 2
    return pl.pallas_call(
        functools.partial(_ret_kernel, n_steps=n, C=C),
        grid=(B, 2, n),
        in_specs=[
            pl.BlockSpec(memory_space=pltpu.SMEM),
            pl.BlockSpec((1, n_qk), const2),
            pl.BlockSpec((1, n_qk), const2),
            pl.BlockSpec((1, n_v), const2),
            pl.BlockSpec((1, n_v), const2),
            pl.BlockSpec((1, n_v), const2),
            pl.BlockSpec((1, R, n_qk), lambda b, p, i: (b, fwd_only(p, i), 0)),
            pl.BlockSpec((1, R, n_qk), lambda b, p, i: (b, chunk(p, i), 0)),
            pl.BlockSpec((1, R, n_v), lambda b, p, i: (b, chunk(p, i), 0)),
            pl.BlockSpec((1, R, n_v), lambda b, p, i: (b, fwd_only(p, i), 0)),
            pl.BlockSpec((1, Lc, n_qk), lambda b, p, i: (b, 0, 0)),
            pl.BlockSpec((1, Lc, n_v), lambda b, p, i: (b, 0, 0)),
        ],
        out_specs=pl.BlockSpec((1, R, n_v), lambda b, p, i: (b, fwd_only(p, i), 0)),
        out_shape=jax.ShapeDtypeStruct((B, L, n_v), BF16),
        scratch_shapes=[
            pltpu.VMEM((RET_HEADS, C, C), F32),
            pltpu.VMEM((C, n_qk), F32), pltpu.VMEM((C, n_qk), F32),
            pltpu.VMEM((C, n_qk), F32), pltpu.VMEM((C, n_qk), F32),
            pltpu.VMEM((n_pairs, 2 * RET_DK, 2 * RET_DV), F32),
            pltpu.VMEM((n_pairs, 2 * RET_DK, 2 * RET_DV), F32),
            pltpu.VMEM((L // C, n_pairs, 2 * RET_DK, 2 * RET_DV), BF16),
        ],
        compiler_params=_cparams(("arbitrary", "arbitrary", "arbitrary")),
        name="retention",
    )(log_gamma, lgf, lgb, lgvf, lgvb, g_ret_out, rq, rk, rv, rg, rk_c, rv_c)


def _out_router_kernel(mla_ref, ret_ref, x_ref, mod_ref, wo_ref, gffn_ref, wr_ref, br_ref,
                       x1_ref, hf_ref, idx_ref, rank_ref, gate_ref, cnt_ref):
    tm = x_ref.shape[0]
    n_mla = mla_ref.shape[1]
    y = jnp.dot(mla_ref[...], wo_ref[0:n_mla, :], preferred_element_type=F32)
    y = y + jnp.dot(ret_ref[...], wo_ref[n_mla:, :], preferred_element_type=F32)
    x1 = x_ref[...] + mod_ref[0, 2:3, :] * y
    x1_ref[...] = x1
    ms = jnp.mean(x1 * x1, axis=-1, keepdims=True)
    hf = x1 * lax.rsqrt(ms + EPS) * gffn_ref[...]
    hf = hf * (1.0 + mod_ref[0, 4:5, :]) + mod_ref[0, 3:4, :]
    for c in range(hf.shape[1] // LANES):
        hf_ref[pl.ds(c, tm, stride=SUBLANES), :] = hf[:, c * LANES:(c + 1) * LANES]

    hi = hf.astype(BF16)
    lo = (hf - hi.astype(F32)).astype(BF16)
    both = jnp.dot(hi, wr_ref[...], preferred_element_type=F32)
    logits = (both[:, :LANES] + both[:, LANES:]
              + jnp.dot(lo, wr_ref[:, :LANES], preferred_element_type=F32)) + br_ref[...]
    lt = logits.T[0:N_EXPERTS, :]
    e_iota = lax.broadcasted_iota(jnp.int32, (N_EXPERTS, tm), 0).astype(F32)
    vals, idxs, hits = [], [], []
    for _ in range(TOP_K):
        mx = jnp.max(lt, axis=0, keepdims=True)
        ix = jnp.min(jnp.where(lt == mx, e_iota, float(N_EXPERTS)), axis=0, keepdims=True)
        hit = e_iota == ix
        lt = jnp.where(hit, -jnp.inf, lt)
        vals.append(mx)
        idxs.append(ix)
        hits.append(hit.astype(F32))

    s_iota = lax.broadcasted_iota(jnp.int32, (tm, tm), 0)
    t_iota = lax.broadcasted_iota(jnp.int32, (tm, tm), 1)
    upper = (s_iota <= t_iota).astype(F32).astype(BF16)
    prefix = jnp.dot(jnp.concatenate(hits, axis=0).astype(BF16), upper,
                     preferred_element_type=F32)
    ranks = []
    seen = jnp.zeros((N_EXPERTS, 1), F32)
    for k in range(TOP_K):
        pk = prefix[k * N_EXPERTS:(k + 1) * N_EXPERTS, :]
        rank = jnp.sum(hits[k] * (pk - 1.0 + seen), axis=0, keepdims=True)
        seen = seen + jnp.sum(hits[k], axis=1, keepdims=True)
        ranks.append(rank.astype(jnp.int32))
    ex = [jnp.exp(v - vals[0]) for v in vals]
    den = ex[0] + ex[1] + ex[2] + ex[3]
    idx_ref[...] = jnp.concatenate(idxs, axis=0).astype(jnp.int32)
    rank_ref[...] = jnp.concatenate(ranks, axis=0)
    gate_ref[...] = jnp.concatenate([e / den for e in ex], axis=0)
    cnt_ref[0] = jnp.broadcast_to(seen, (N_EXPERTS, LANES)).astype(jnp.int32)


def _out_router(mla, ret, x2, mod3, w_out_b, g_ffn, w_r, b_r, L, tm):
    T, D = x2.shape
    n_tiles = T // tm
    per_b = L // tm
    const = lambda i: (0, 0)
    return pl.pallas_call(
        _out_router_kernel,
        grid=(n_tiles,),
        in_specs=[
            pl.BlockSpec((tm, mla.shape[1]), lambda i: (i, 0)),
            pl.BlockSpec((tm, ret.shape[1]), lambda i: (i, 0)),
            pl.BlockSpec((tm, D), lambda i: (i, 0)),
            pl.BlockSpec((1, N_MOD, D), lambda i: (i // per_b, 0, 0)),
            pl.BlockSpec(w_out_b.shape, const),
            pl.BlockSpec((1, D), const),
            pl.BlockSpec(w_r.shape, const),
            pl.BlockSpec((1, LANES), const),
        ],
        out_specs=[
            pl.BlockSpec((tm, D), lambda i: (i, 0)),
            pl.BlockSpec((tm * SUBLANES, LANES), lambda i: (i, 0)),
            pl.BlockSpec((TOP_K, tm), lambda i: (0, i)),
            pl.BlockSpec((TOP_K, tm), lambda i: (0, i)),
            pl.BlockSpec((TOP_K, tm), lambda i: (0, i)),
            pl.BlockSpec((1, N_EXPERTS, LANES), lambda i: (i, 0, 0)),
        ],
        out_shape=[
            jax.ShapeDtypeStruct((T, D), F32),
            jax.ShapeDtypeStruct((T * SUBLANES, LANES), F32),
            jax.ShapeDtypeStruct((TOP_K, T), jnp.int32),
            jax.ShapeDtypeStruct((TOP_K, T), jnp.int32),
            jax.ShapeDtypeStruct((TOP_K, T), F32),
            jax.ShapeDtypeStruct((n_tiles, N_EXPERTS, LANES), jnp.int32),
        ],
        compiler_params=_cparams(("arbitrary",)),
        name="out_router",
    )(mla, ret, x2, mod3, w_out_b, g_ffn, w_r, b_r)


def _row_copy(src, dst, sem):
    return pltpu.make_async_copy(src, dst, sem)


def _dispatch_kernel(dest_ref, hf_ref, w1_ref, w2_ref, xs_ref, g_ref, l_ref, w2b_ref, sem, *, td):
    def issue(t, carry):
        src = hf_ref.at[pl.ds(pl.multiple_of(t * SUBLANES, SUBLANES), SUBLANES), :]
        for k in range(TOP_K):
            d = dest_ref[t * TOP_K + k]
            dst = xs_ref.at[pl.ds(pl.multiple_of(d * SUBLANES, SUBLANES), SUBLANES), :]
            _row_copy(src, dst, sem).start(priority=k % 2)
        return carry

    lax.fori_loop(0, td, issue, 0, unroll=ISSUE_UNROLL)

    w = w1_ref[0].astype(BF16)
    sub = 2 * LANES
    r = lax.broadcasted_iota(jnp.int32, (sub, sub), 0)
    c = lax.broadcasted_iota(jnp.int32, (sub, sub), 1)
    src_col = jnp.where(c < LANES, 2 * c, 2 * (c - LANES) + 1)
    sel = (r == src_col).astype(F32).astype(BF16)
    for s in range(w.shape[1] // sub):
        t = jnp.dot(w[:, s * sub:(s + 1) * sub], sel, preferred_element_type=F32).astype(BF16)
        g_ref[0, :, s * LANES:(s + 1) * LANES] = t[:, :LANES]
        l_ref[0, :, s * LANES:(s + 1) * LANES] = t[:, LANES:]
    w2b_ref[0] = w2_ref[0].astype(BF16)

    for k in range(TOP_K):
        _row_copy(hf_ref, xs_ref.at[pl.ds(0, td * SUBLANES), :], sem).wait()


def _dispatch_prep(dest_flat, hf8, w1, w2, n_rows):
    T = hf8.shape[0] // SUBLANES
    E, d, f2 = w1.shape
    f = f2 // 2
    halves = 2
    n = E * halves
    assert T % (n * SUBLANES) == 0
    td = T // n
    half_w1 = jax.ShapeDtypeStruct((E, d, f), BF16)
    return pl.pallas_call(
        functools.partial(_dispatch_kernel, td=td),
        grid=(n,),
        in_specs=[
            pl.BlockSpec((td * TOP_K,), lambda i: (i,), memory_space=pltpu.SMEM),
            pl.BlockSpec((td * SUBLANES, LANES), lambda i: (i, 0)),
            pl.BlockSpec((1, d, f2 // halves), lambda i: (i // halves, 0, i % halves)),
            pl.BlockSpec((1, f // halves, d), lambda i: (i // halves, i % halves, 0)),
        ],
        out_specs=[
            pl.BlockSpec(memory_space=pl.ANY),
            pl.BlockSpec((1, d, f // halves), lambda i: (i // halves, 0, i % halves)),
            pl.BlockSpec((1, d, f // halves), lambda i: (i // halves, 0, i % halves)),
            pl.BlockSpec((1, f // halves, d), lambda i: (i // halves, i % halves, 0)),
        ],
        out_shape=[jax.ShapeDtypeStruct((n_rows * SUBLANES, LANES), F32), half_w1, half_w1,
                   jax.ShapeDtypeStruct(w2.shape, BF16)],
        scratch_shapes=[pltpu.SemaphoreType.DMA],
        compiler_params=_cparams(("arbitrary",)),
        name="dispatch_prep",
    )(dest_flat, hf8, w1, w2)


def _experts_kernel(ie_ref, ib_ref, lo_ref, hi_ref, first_ref, ni_ref, xs_ref, w1g_ref, w1l_ref,
                    w2_ref, b1g_ref, b1l_ref, b2_ref, ys_ref):
    i = pl.program_id(0)
    blk = xs_ref.shape[0] // SUBLANES
    d = w1g_ref.shape[1]

    @pl.when(i < ni_ref[0])
    def _():
        cols = [xs_ref[pl.ds(c, blk, stride=SUBLANES), :] for c in range(d // LANES)]
        x = jnp.concatenate(cols, axis=-1)
        row = lax.broadcasted_iota(jnp.int32, (blk, 1), 0)
        mine = (row >= lo_ref[i]) & (row < hi_ref[i])
        x = jnp.where(mine, x, 0.0).astype(BF16)
        hg = jnp.dot(x, w1g_ref[0], preferred_element_type=F32) + b1g_ref[0]
        hl = jnp.dot(x, w1l_ref[0], preferred_element_type=F32) + b1l_ref[0]
        glu = jnp.minimum(hg, SWIGLU_LIMIT)
        lin = jnp.clip(hl, -SWIGLU_LIMIT, SWIGLU_LIMIT)
        act = glu * _sigmoid(SWIGLU_ALPHA * glu) * (lin + 1.0)
        y = jnp.dot(act.astype(BF16), w2_ref[0], preferred_element_type=F32) + b2_ref[0]
        y = jnp.where(mine, y, 0.0)

        @pl.when(first_ref[i] == 1)
        def _():
            for c in range(d // LANES):
                ys_ref[pl.ds(c, blk, stride=SUBLANES), :] = y[:, c * LANES:(c + 1) * LANES]

        @pl.when(first_ref[i] == 0)
        def _():
            for c in range(d // LANES):
                ys_ref[pl.ds(c, blk, stride=SUBLANES), :] += y[:, c * LANES:(c + 1) * LANES]


def _experts(items, xs, w1g, w1l, w2, b1g, b1l, b2, blk):
    item_e, item_blk, item_lo, item_hi, item_first, n_items = items
    d = w1g.shape[1]
    f = w1g.shape[2]
    row_map = lambda i, ie, ib, lo, hi, fi, ni: (ib[i], 0)
    exp_map = lambda i, ie, ib, lo, hi, fi, ni: (ie[i], 0, 0)
    grid_spec = pltpu.PrefetchScalarGridSpec(
        num_scalar_prefetch=6,
        grid=(item_e.shape[0],),
        in_specs=[
            pl.BlockSpec((blk * SUBLANES, LANES), row_map),
            pl.BlockSpec((1, d, f), exp_map),
            pl.BlockSpec((1, d, f), exp_map),
            pl.BlockSpec((1, f, d), exp_map),
            pl.BlockSpec((1, 1, f), exp_map),
            pl.BlockSpec((1, 1, f), exp_map),
            pl.BlockSpec((1, 1, d), exp_map),
        ],
        out_specs=pl.BlockSpec((blk * SUBLANES, LANES), row_map),
    )
    return pl.pallas_call(
        _experts_kernel,
        grid_spec=grid_spec,
        out_shape=jax.ShapeDtypeStruct(xs.shape, F32),
        compiler_params=_cparams(("arbitrary",)),
        name="experts",
    )(item_e, item_blk, item_lo, item_hi, item_first, n_items, xs, w1g, w1l, w2, b1g, b1l, b2)


def _combine_kernel(dest_ref, dnext_ref, x1_ref, gate_ref, mod_ref, ys_ref, o_ref, buf_ref, sems,
                    *, tc):
    i = pl.program_id(0)
    n = pl.num_programs(0)
    slot = i % 2

    def gather(idx_ref, s):
        def issue(t, carry):
            for k in range(TOP_K):
                d = idx_ref[t * TOP_K + k]
                src = ys_ref.at[pl.ds(pl.multiple_of(d * SUBLANES, SUBLANES), SUBLANES), :]
                dst = buf_ref.at[s, k, pl.ds(pl.multiple_of(t * SUBLANES, SUBLANES), SUBLANES), :]
                _row_copy(src, dst, sems.at[s]).start(priority=k % 2)
            return carry
        lax.fori_loop(0, tc, issue, 0, unroll=ISSUE_UNROLL)

    @pl.when(i == 0)
    def _():
        gather(dest_ref, 0)

    @pl.when(i + 1 < n)
    def _():
        gather(dnext_ref, 1 - slot)

    for k in range(TOP_K):
        _row_copy(ys_ref.at[pl.ds(0, tc * SUBLANES), :], buf_ref.at[slot, k], sems.at[slot]).wait()

    g = gate_ref[...]
    for c in range(o_ref.shape[1] // LANES):
        cs = slice(c * LANES, (c + 1) * LANES)
        acc = g[:, 0:1] * buf_ref[slot, 0, pl.ds(c, tc, stride=SUBLANES), :]
        for k in range(1, TOP_K):
            acc = acc + g[:, k:k + 1] * buf_ref[slot, k, pl.ds(c, tc, stride=SUBLANES), :]
        o_ref[:, cs] = x1_ref[:, cs] + mod_ref[0, 5:6, cs] * acc


def _combine(dest_flat, x1, gates_t, mod3, ys, L, tc):
    T, D = x1.shape
    per_b = L // tc
    n = T // tc
    return pl.pallas_call(
        functools.partial(_combine_kernel, tc=tc),
        grid=(n,),
        in_specs=[
            pl.BlockSpec((tc * TOP_K,), lambda i: (i,), memory_space=pltpu.SMEM),
            pl.BlockSpec((tc * TOP_K,), lambda i: (jnp.minimum(i + 1, n - 1),),
                         memory_space=pltpu.SMEM),
            pl.BlockSpec((tc, D), lambda i: (i, 0)),
            pl.BlockSpec((tc, TOP_K), lambda i: (i, 0)),
            pl.BlockSpec((1, N_MOD, D), lambda i: (i // per_b, 0, 0)),
            pl.BlockSpec(memory_space=pl.ANY),
        ],
        out_specs=pl.BlockSpec((tc, D), lambda i: (i, 0)),
        out_shape=jax.ShapeDtypeStruct((T, D), F32),
        scratch_shapes=[pltpu.VMEM((2, TOP_K, tc * SUBLANES, LANES), F32),
                        pltpu.SemaphoreType.DMA((2,))],
        compiler_params=_cparams(("arbitrary",)),
        name="combine",
    )(dest_flat, dest_flat, x1, gates_t, mod3, ys)


def _rope_tables(L, dim, lane_off, width):
    rows = L // GRID_W
    nf = dim // 4
    inv = jnp.power(ROPE_BASE, -jnp.arange(nf, dtype=F32) / nf)
    row = jnp.repeat(jnp.arange(rows, dtype=F32), GRID_W)
    col = jnp.tile(jnp.arange(GRID_W, dtype=F32), rows)
    pos = jnp.stack([row, col], axis=-1)
    ang = pos[:, :, None] * inv
    ang = jnp.broadcast_to(ang[:, :, None, :], (L, 2, 2, nf)).reshape(L, dim)
    sign = jnp.where((jnp.arange(dim) % (dim // 2)) < nf, -1.0, 1.0).astype(F32)
    cos, sin = jnp.cos(ang), jnp.sin(ang) * sign
    if lane_off is None:
        reps = width // dim
        return jnp.tile(cos, (1, reps)), jnp.tile(sin, (1, reps))
    cfull = jnp.ones((L, width), F32).at[:, lane_off:lane_off + dim].set(cos)
    sfull = jnp.zeros((L, width), F32).at[:, lane_off:lane_off + dim].set(sin)
    return cfull, sfull


def _identity_tables(L):
    return jnp.ones((L, LANES), F32), jnp.zeros((L, LANES), F32)


def _half_rot_src(dim):
    j = jnp.arange(dim)
    return jnp.where((j % (dim // 2)) < dim // 4, j + dim // 4, j - dim // 4)


def _prep_weights(w_in, w_q_up, w_kv_up, g_q_head, g_k_head):
    D = w_in.shape[0]
    o = 0
    wq = w_in[:, o:o + Q_LORA]; o += Q_LORA
    wkv = w_in[:, o:o + KV_LORA]; o += KV_LORA
    wpe = w_in[:, o:o + MLA_ROPE]; o += MLA_ROPE
    n_qk = RET_HEADS * RET_DK
    wrq = w_in[:, o:o + n_qk]; o += n_qk
    wrk = w_in[:, o:o + n_qk]; o += n_qk
    rest = w_in[:, o:]
    src_m = _half_rot_src(MLA_ROPE)
    src_r = _half_rot_src(RET_DK)
    rope_lanes = slice(MLA_NOPE, MLA_NOPE + MLA_ROPE)
    pe_blk = jnp.zeros((D, LANES), w_in.dtype).at[:, rope_lanes].set(wpe)
    pe_perm = jnp.zeros((D, LANES), w_in.dtype).at[:, rope_lanes].set(wpe[:, src_m])
    perm_heads = lambda w: w.reshape(D, RET_HEADS, RET_DK)[:, :, src_r].reshape(D, n_qk)
    w_in_r = jnp.concatenate([wq, wkv, pe_blk, pe_perm, wrq, perm_heads(wrq), wrk, perm_heads(wrk),
                              rest], axis=1).astype(BF16)

    pad_h = LANES - MLA_QK
    wq3 = w_q_up.reshape(Q_LORA, MLA_HEADS, MLA_QK)
    q_main = jnp.pad(wq3, ((0, 0), (0, 0), (0, pad_h)))
    q_perm = jnp.zeros_like(q_main).at[:, :, rope_lanes].set(wq3[:, :, MLA_NOPE:][:, :, src_m])
    w_q_r = jnp.concatenate([q_main.reshape(Q_LORA, -1), q_perm.reshape(Q_LORA, -1)], axis=1).astype(BF16)

    kv = w_kv_up.reshape(KV_LORA, MLA_HEADS, MLA_NOPE + MLA_V)
    kpart = jnp.pad(kv[:, :, :MLA_NOPE], ((0, 0), (0, 0), (0, LANES - MLA_NOPE)))
    vpart = jnp.pad(kv[:, :, MLA_NOPE:], ((0, 0), (0, 0), (0, LANES - MLA_V)))
    w_kv_r = jnp.concatenate([kpart.reshape(KV_LORA, -1), vpart.reshape(KV_LORA, -1)], axis=1).astype(BF16)

    def gains(g):
        main = jnp.pad(g, (0, pad_h))[None, :]
        perm = jnp.zeros((1, LANES), g.dtype).at[0, rope_lanes].set(g[MLA_NOPE:][src_m])
        return main, perm

    gqh, gqr = gains(g_q_head)
    gkh, gkr = gains(g_k_head)
    return w_in_r, w_q_r, w_kv_r, gqh, gqr, gkh, gkr


def _routing_tables(idx, rank, counts, tm, blk):
    T = idx.shape[1]
    i32 = jnp.int32
    tot = jnp.sum(counts, axis=0)
    end = jnp.cumsum(tot)
    start = end - tot
    tile_base = start[None, :] + jnp.cumsum(counts, axis=0) - counts
    base_tok = jnp.repeat(tile_base, tm, axis=0)
    hit = idx[:, :, None] == jnp.arange(N_EXPERTS, dtype=i32)
    dest = jnp.sum(jnp.where(hit, base_tok[None], 0), axis=-1) + rank

    n_work = (T * TOP_K) // blk + N_EXPERTS
    first_blk = start // blk
    last_blk = (end - 1) // blk
    per_e = jnp.where(tot > 0, last_blk - first_blk + 1, 0)
    item_end = jnp.cumsum(per_e)
    item_start = item_end - per_e
    n_items = item_end[-1]
    j = jnp.minimum(jnp.arange(n_work, dtype=i32), n_items - 1)
    item_e = jnp.minimum(jnp.sum(item_end[None, :] <= j[:, None], axis=1), N_EXPERTS - 1).astype(i32)
    onehot = item_e[:, None] == jnp.arange(N_EXPERTS, dtype=i32)[None, :]
    pick = lambda tab: jnp.sum(jnp.where(onehot, tab[None, :], 0), axis=1)
    item_blk = pick(first_blk) + j - pick(item_start)
    item_lo = jnp.clip(pick(start) - item_blk * blk, 0, blk)
    item_hi = jnp.clip(pick(end) - item_blk * blk, 0, blk)
    prev_blk = jnp.concatenate([jnp.full((1,), -1, i32), item_blk[:-1].astype(i32)])
    item_first = (item_blk != prev_blk).astype(i32)
    items = (item_e, item_blk.astype(i32), item_lo.astype(i32), item_hi.astype(i32), item_first,
             n_items.astype(i32).reshape(1))
    return dest.astype(i32), items


def kernel(x, c, ctx, c_ctx, g_attn, g_ffn, w_ada, b_ada, w_in, g_q_lora, w_q_up, g_q_head,
           g_kv_lora, w_kv_up, g_k_head, ret_decay_logit, g_ret_out, w_out, w_router, b_router,
           w_mlp1, b_mlp1, w_mlp2, b_mlp2):
    B, L, D = x.shape
    Lc = ctx.shape[1]
    T = B * L
    l = 0
    assert w_ada.shape[0] == 1

    rows = ((B + 1 + SUBLANES - 1) // SUBLANES) * SUBLANES
    cc = jnp.zeros((rows, D), F32).at[:B].set(c).at[B].set(c_ctx)
    mod3 = _adaln(cc, w_ada[l], b_ada[l][None, :]).reshape(rows, N_MOD, D)

    w_in_r, w_q_r, w_kv_r, gqh, gqr, gkh, gkr = _prep_weights(
        w_in[l], w_q_up[l], w_kv_up[l], g_q_head[l], g_k_head[l])
    tabs_x = _rope_tables(L, MLA_ROPE, MLA_NOPE, LANES) + _rope_tables(L, RET_DK, None, LANES)
    tabs_c = _identity_tables(Lc) + _identity_tables(Lc)
    proj = functools.partial(_in_proj, g_attn=g_attn[l][None, :], w_in_r=w_in_r,
                             g_q_lora=g_q_lora[l][None, :], w_q_r=w_q_r, gqh=gqh, gqr=gqr,
                             g_kv_lora=g_kv_lora[l][None, :], w_kv_r=w_kv_r, gkh=gkh, gkr=gkr)
    q, kx, vx, rq, rk, rv, rg = proj(x, mod3, lambda b: b, tabs=tabs_x, tm=min(PROJ_TM, L))
    _, kc, vc, _, rk_c, rv_c, _ = proj(ctx, mod3, lambda b: B, tabs=tabs_c, tm=Lc)

    mla = _attention(q, kx, vx, kc, vc, min(ATT_TQ, L), min(ATT_TK, L // 2))

    log_gamma = jax.nn.log_sigmoid(ret_decay_logit[l].astype(F32))
    ret = _retention(log_gamma, g_ret_out[l][None, :], rq, rk, rv, rg, rk_c, rv_c,
                     min(RET_C, L), min(RET_ROWS, L))

    w_r32 = jnp.pad(w_router[l], ((0, 0), (0, LANES - N_EXPERTS)))
    w_r_hi = w_r32.astype(BF16)
    w_r = jnp.concatenate([w_r_hi, (w_r32 - w_r_hi.astype(F32)).astype(BF16)], axis=1)
    b_r = jnp.pad(b_router[l], (0, LANES - N_EXPERTS))[None, :]
    tm = min(OUT_TM, L)
    x1, hf8, idx, rank, gates, cnt = _out_router(
        mla.reshape(T, -1), ret.reshape(T, -1), x.reshape(T, D), mod3,
        w_out[l].astype(BF16), g_ffn[l][None, :], w_r, b_r, L, tm)

    blk = MOE_BLK
    assert (T * TOP_K) % blk == 0
    dest, items = _routing_tables(idx, rank, cnt[:, :, 0], tm, blk)
    dest_flat = dest.T.reshape(-1)

    xs, w1g, w1l, w2b = _dispatch_prep(dest_flat, hf8, w_mlp1[l], w_mlp2[l], T * TOP_K)
    b1g = b_mlp1[l][:, None, 0::2]
    b1l = b_mlp1[l][:, None, 1::2]
    ys = _experts(items, xs, w1g, w1l, w2b, b1g, b1l, b_mlp2[l][:, None, :], blk)

    out = _combine(dest_flat, x1, gates.T, mod3, ys, L, min(COMB_T, L))
    return out.reshape(B, L, D)
```

```python
import functools
import math

import jax
import jax.numpy as jnp
from jax import lax
from jax.experimental import pallas as pl
from jax.experimental.pallas import tpu as pltpu

F32 = jnp.float32
BF16 = jnp.bfloat16

LANES = 128
SUBLANES = 8
VMEM_LIMIT_BYTES = 56 * 1024 * 1024

EPS = 1e-6
ROPE_BASE = 10000.0
GRID_W = 64
N_MOD = 6
MLA_HEADS = 8
MLA_NOPE = 64
MLA_ROPE = 32
MLA_QK = MLA_NOPE + MLA_ROPE
MLA_V = 64
Q_LORA = 256
KV_LORA = 128
RET_HEADS = 4
RET_DK = 64
RET_DV = 128
N_EXPERTS = 32
TOP_K = 4
SWIGLU_LIMIT = 7.0
SWIGLU_ALPHA = 1.702
LOG2E = 1.4426950408889634

PROJ_TM = 512
ATT_TQ = 512
ATT_TK = 1024
RET_C = 256
RET_ROWS = 1024
OUT_TM = 512
MOE_BLK = 512
COMB_T = 256
COMB_SLOTS = 3
COMB_GROUPS = 4
ISSUE_UNROLL = 8


def _cparams(sem):
    return pltpu.CompilerParams(dimension_semantics=sem, vmem_limit_bytes=VMEM_LIMIT_BYTES)


def _sigmoid(x):
    return 1.0 / (1.0 + jnp.exp(-x))


def _adaln_kernel(c_ref, w_ref, b_ref, o_ref):
    c = c_ref[...]
    s = (c * _sigmoid(c)).astype(BF16)
    o_ref[...] = jnp.dot(s, w_ref[...].astype(BF16), preferred_element_type=F32) + b_ref[...]


def _adaln(cc, w_ada, b_ada):
    rows, d = cc.shape
    n = w_ada.shape[1]
    tn = 1536
    return pl.pallas_call(
        _adaln_kernel,
        grid=(n // tn,),
        in_specs=[pl.BlockSpec((rows, d), lambda j: (0, 0)),
                  pl.BlockSpec((d, tn), lambda j: (0, j)),
                  pl.BlockSpec((1, tn), lambda j: (0, j))],
        out_specs=pl.BlockSpec((rows, tn), lambda j: (0, j)),
        out_shape=jax.ShapeDtypeStruct((rows, n), F32),
        compiler_params=_cparams(("arbitrary",)),
        name="adaln",
    )(cc, w_ada, b_ada)


def _in_proj_kernel(x_ref, mod_ref, gattn_ref, win_ref, gql_ref, wq_ref, gqh_ref, gqr_ref,
                    gkvl_ref, wkv_ref, gkh_ref, gkr_ref, cm_ref, sm_ref, cr_ref, sr_ref,
                    q_ref, k_ref, v_ref, rq_ref, rk_ref, rv_ref, rg_ref, *, q_scale):
    x = x_ref[0]
    shift = mod_ref[0, 0:1, :]
    scale = mod_ref[0, 1:2, :]
    ms = jnp.mean(x * x, axis=-1, keepdims=True)
    h = x * lax.rsqrt(ms + EPS) * gattn_ref[...]
    h = h * (1.0 + scale) + shift
    p = jnp.dot(h.astype(BF16), win_ref[...], preferred_element_type=F32)

    lane = lax.broadcasted_iota(jnp.int32, (1, LANES), 1)
    cm, sm = cm_ref[...], sm_ref[...]
    cr, sr = cr_ref[...], sr_ref[...]
    n_hl = MLA_HEADS * LANES

    cq = p[:, 0:Q_LORA]
    cq = cq * lax.rsqrt(jnp.mean(cq * cq, axis=-1, keepdims=True) + EPS) * gql_ref[...]
    qf = jnp.dot(cq.astype(BF16), wq_ref[...], preferred_element_type=F32)
    gq_cos = gqh_ref[...] * cm
    gq_sin = gqr_ref[...] * sm
    for hd in range(MLA_HEADS):
        blk = qf[:, hd * LANES:(hd + 1) * LANES]
        perm = qf[:, n_hl + hd * LANES:n_hl + (hd + 1) * LANES]
        r = lax.rsqrt(jnp.sum(blk * blk, axis=-1, keepdims=True) * (1.0 / MLA_QK) + EPS)
        q_ref[0, hd] = ((blk * gq_cos + perm * gq_sin) * (r * q_scale)).astype(BF16)

    o_kv = Q_LORA
    ckv = p[:, o_kv:o_kv + KV_LORA]
    ckv = ckv * lax.rsqrt(jnp.mean(ckv * ckv, axis=-1, keepdims=True) + EPS) * gkvl_ref[...]
    kvf = jnp.dot(ckv.astype(BF16), wkv_ref[...], preferred_element_type=F32)
    o_pe = o_kv + KV_LORA
    pe = p[:, o_pe:o_pe + LANES]
    pe_perm = p[:, o_pe + LANES:o_pe + 2 * LANES]
    gk = gkh_ref[...]
    pe_rope = pe * (gk * cm) + pe_perm * (gkr_ref[...] * sm)
    pe_ss = jnp.sum(pe * pe, axis=-1, keepdims=True)
    ones_col = (lane == MLA_V).astype(F32)
    for hd in range(MLA_HEADS):
        kn = kvf[:, hd * LANES:(hd + 1) * LANES]
        ss = jnp.sum(kn * kn, axis=-1, keepdims=True) + pe_ss
        r = lax.rsqrt(ss * (1.0 / MLA_QK) + EPS)
        kk = (kn * gk + pe_rope) * r
        k_ref[0, hd] = kk.T.astype(BF16)
        vv = kvf[:, (MLA_HEADS + hd) * LANES:(MLA_HEADS + hd + 1) * LANES] + ones_col
        v_ref[0, hd] = vv.astype(BF16)

    n_qk = RET_HEADS * RET_DK
    o_rq = o_pe + 2 * LANES
    o_rk = o_rq + 2 * n_qk
    for j in range(n_qk // LANES):
        js = slice(j * LANES, (j + 1) * LANES)
        a = p[:, o_rq + j * LANES:o_rq + (j + 1) * LANES]
        b = p[:, o_rq + n_qk + j * LANES:o_rq + n_qk + (j + 1) * LANES]
        rq_ref[0, :, js] = (a * cr + b * sr).astype(BF16)
        a = p[:, o_rk + j * LANES:o_rk + (j + 1) * LANES]
        b = p[:, o_rk + n_qk + j * LANES:o_rk + n_qk + (j + 1) * LANES]
        rk_ref[0, :, js] = ((a * cr + b * sr) * (RET_DK ** -0.5)).astype(BF16)
    o_rv = o_rk + 2 * n_qk
    n_v = RET_HEADS * RET_DV
    rv_ref[0] = p[:, o_rv:o_rv + n_v].astype(BF16)
    rg_ref[0] = p[:, o_rv + n_v:o_rv + 2 * n_v]


def _in_proj(x, mod3, mod_row_of_batch, g_attn, w_in_r, g_q_lora, w_q_r, gqh, gqr, g_kv_lora,
             w_kv_r, gkh, gkr, tabs, tm):
    B, L, D = x.shape
    cm, sm, cr, sr = tabs
    n_in = w_in_r.shape[1]
    const = lambda b, i: (0, 0)
    tab_spec = pl.BlockSpec((tm, LANES), lambda b, i: (i, 0))
    head_spec = pl.BlockSpec((1, MLA_HEADS, tm, LANES), lambda b, i: (b, 0, i, 0))
    n_qk = RET_HEADS * RET_DK
    n_v = RET_HEADS * RET_DV
    seq_spec = lambda w: pl.BlockSpec((1, tm, w), lambda b, i: (b, i, 0))
    head_shape = jax.ShapeDtypeStruct((B, MLA_HEADS, L, LANES), BF16)
    q_scale = MLA_QK ** -0.5 * LOG2E
    return pl.pallas_call(
        functools.partial(_in_proj_kernel, q_scale=q_scale),
        grid=(B, L // tm),
        in_specs=[
            pl.BlockSpec((1, tm, D), lambda b, i: (b, i, 0)),
            pl.BlockSpec((1, N_MOD, D), lambda b, i: (mod_row_of_batch(b), 0, 0)),
            pl.BlockSpec((1, D), const),
            pl.BlockSpec((D, n_in), const),
            pl.BlockSpec((1, Q_LORA), const),
            pl.BlockSpec(w_q_r.shape, const),
            pl.BlockSpec((1, LANES), const),
            pl.BlockSpec((1, LANES), const),
            pl.BlockSpec((1, KV_LORA), const),
            pl.BlockSpec(w_kv_r.shape, const),
            pl.BlockSpec((1, LANES), const),
            pl.BlockSpec((1, LANES), const),
            tab_spec, tab_spec, tab_spec, tab_spec,
        ],
        out_specs=[head_spec,
                   pl.BlockSpec((1, MLA_HEADS, LANES, tm), lambda b, i: (b, 0, 0, i)),
                   head_spec,
                   seq_spec(n_qk), seq_spec(n_qk), seq_spec(n_v), seq_spec(n_v)],
        out_shape=[head_shape,
                   jax.ShapeDtypeStruct((B, MLA_HEADS, LANES, L), BF16),
                   head_shape,
                   jax.ShapeDtypeStruct((B, L, n_qk), BF16),
                   jax.ShapeDtypeStruct((B, L, n_qk), BF16),
                   jax.ShapeDtypeStruct((B, L, n_v), BF16),
                   jax.ShapeDtypeStruct((B, L, n_v), F32)],
        compiler_params=_cparams(("arbitrary", "arbitrary")),
        name="in_proj",
    )(x, mod3, g_attn, w_in_r, g_q_lora, w_q_r, gqh, gqr, g_kv_lora, w_kv_r, gkh, gkr,
      cm, sm, cr, sr)


def _attn_kernel(q_ref, kx_ref, vx_ref, kc_ref, vc_ref, o_ref, m_ref, acc_ref, s_ref, *, tk):
    n_kv = kx_ref.shape[3] // tk
    n_heads = q_ref.shape[1]

    def scores(hh, start, slot):
        kb = kx_ref[0, hh, :, pl.ds(start, tk)]
        s_ref[hh, slot] = jnp.dot(q_ref[0, hh], kb, preferred_element_type=F32)

    def consume(hh, start, slot):
        s = s_ref[hh, slot]
        m_old = m_ref[hh]
        m_new = jnp.maximum(m_old, jnp.max(s, axis=-1, keepdims=True))
        alpha = jnp.exp2(m_old - m_new)
        pr = jnp.exp2(s - m_new)
        vb = vx_ref[0, hh, pl.ds(start, tk), :]
        acc_ref[hh] = alpha * acc_ref[hh] + jnp.dot(pr.astype(BF16), vb,
                                                    preferred_element_type=F32)
        m_ref[hh] = m_new

    def step(j, slot, prefetch):
        if prefetch:
            nxt = pl.multiple_of((j + 1) * tk, tk)
            for hh in range(n_heads):
                scores(hh, nxt, 1 - slot)
        cur = pl.multiple_of(j * tk, tk)
        for hh in range(n_heads):
            consume(hh, cur, slot)

    for hh in range(n_heads):
        scores(hh, 0, 0)
        s = jnp.dot(q_ref[0, hh], kc_ref[0, hh], preferred_element_type=F32)
        m0 = jnp.max(s, axis=-1, keepdims=True)
        m_ref[hh] = m0
        acc_ref[hh] = jnp.dot(jnp.exp2(s - m0).astype(BF16), vc_ref[0, hh],
                              preferred_element_type=F32)

    def body(jj, carry):
        step(2 * jj, 0, True)
        step(2 * jj + 1, 1, True)
        return carry

    lax.fori_loop(0, n_kv // 2 - 1, body, 0)
    step(n_kv - 2, 0, True)
    step(n_kv - 1, 1, False)
    outs = []
    for hh in range(n_heads):
        acc = acc_ref[hh]
        outs.append(acc[:, :MLA_V] / acc[:, MLA_V:MLA_V + 1])
    o_ref[0] = jnp.concatenate(outs, axis=-1).astype(BF16)


def _attention(q, kx, vx, kc, vc, tq, tk):
    B, H, L, _ = q.shape
    Lc = kc.shape[3]
    return pl.pallas_call(
        functools.partial(_attn_kernel, tk=tk),
        grid=(B, H // 2, L // tq),
        in_specs=[
            pl.BlockSpec((1, 2, tq, LANES), lambda b, h, i: (b, h, i, 0)),
            pl.BlockSpec((1, 2, LANES, L), lambda b, h, i: (b, h, 0, 0)),
            pl.BlockSpec((1, 2, L, LANES), lambda b, h, i: (b, h, 0, 0)),
            pl.BlockSpec((1, 2, LANES, Lc), lambda b, h, i: (b, h, 0, 0)),
            pl.BlockSpec((1, 2, Lc, LANES), lambda b, h, i: (b, h, 0, 0)),
        ],
        out_specs=pl.BlockSpec((1, tq, LANES), lambda b, h, i: (b, i, h)),
        out_shape=jax.ShapeDtypeStruct((B, L, H * MLA_V), BF16),
        scratch_shapes=[pltpu.VMEM((2, tq, 1), F32), pltpu.VMEM((2, tq, LANES), F32),
                        pltpu.VMEM((2, 2, tq, tk), F32)],
        compiler_params=_cparams(("arbitrary", "arbitrary", "arbitrary")),
        name="attention",
    )(q, kx, vx, kc, vc)


def _ret_kernel(lg_ref, lgf_ref, lgb_ref, lgvf_ref, lgvb_ref, gout_ref, rq_ref, rk_ref, rv_ref, rg_ref,
                kc_ref, vc_ref, o_ref, dm_ref, qdf_ref, qdb_ref, kdf_ref, kdb_ref,
                f_ref, r_ref, rs_ref, *, n_steps, C):
    n_sub = rq_ref.shape[1] // C
    Lc = kc_ref.shape[1]
    ps = pl.program_id(1)
    i = pl.program_id(2)
    n_pairs = RET_HEADS // 2
    pw = 2 * RET_DK
    vw = 2 * RET_DV
    tdn = (((0,), (0,)), ((), ()))
    ndn = (((1,), (1,)), ((), ()))
    lgf = lgf_ref[...]
    lgb = lgb_ref[...]

    @pl.when((pl.program_id(0) == 0) & (ps == 0) & (i == 0))
    def _tables():
        a = lax.broadcasted_iota(jnp.int32, (C, C), 0)
        b = lax.broadcasted_iota(jnp.int32, (C, C), 1)
        dab = (a - b).astype(F32)
        for hd in range(RET_HEADS):
            fwd = jnp.where(a >= b, jnp.exp(jnp.where(a >= b, dab, 0.0) * lg_ref[0, hd]), 0.0)
            bwd = jnp.where(b >= a, jnp.exp(jnp.where(b >= a, -dab, 0.0) * lg_ref[1, hd]), 0.0)
            dm_ref[hd] = fwd + bwd
        row = lax.broadcasted_iota(jnp.int32, (C, 1), 0).astype(F32)
        qdf_ref[...] = jnp.exp((row + 1.0) * lgf)
        qdb_ref[...] = jnp.exp((C - row) * lgb)
        kdf_ref[...] = jnp.exp((C - 1.0 - row) * lgf)
        kdb_ref[...] = jnp.exp(row * lgb)

    @pl.when((ps == 0) & (i == 0))
    def _init_states():
        rowc = lax.broadcasted_iota(jnp.int32, (Lc, 1), 0).astype(F32)
        wf = jnp.exp((Lc - 1.0 - rowc) * lgf)
        wb = jnp.exp(rowc * lgb)
        kc = kc_ref[0].astype(F32)
        for pr in range(n_pairs):
            kp = kc[:, pr * pw:(pr + 1) * pw]
            vp = vc_ref[0, :, pr * vw:(pr + 1) * vw]
            f_ref[pr] = lax.dot_general((kp * wf[:, pr * pw:(pr + 1) * pw]).astype(BF16), vp, tdn,
                                        preferred_element_type=F32)
            r_ref[pr] = lax.dot_general((kp * wb[:, pr * pw:(pr + 1) * pw]).astype(BF16), vp, tdn,
                                        preferred_element_type=F32)

    @pl.when(ps == 0)
    def _backward_states():
        cdb = jnp.exp(C * lgvb_ref[...])
        for cc in reversed(range(n_sub)):
            rows = slice(cc * C, (cc + 1) * C)
            c = (n_steps - 1 - i) * n_sub + cc
            k = rk_ref[0, rows, :].astype(F32)
            for pr in range(n_pairs):
                r_old = r_ref[pr]
                rs_ref[c, pr] = r_old.astype(BF16)
                kp = (k[:, pr * pw:(pr + 1) * pw] * kdb_ref[:, pr * pw:(pr + 1) * pw]).astype(BF16)
                vp = rv_ref[0, rows, pr * vw:(pr + 1) * vw]
                upd = lax.dot_general(kp, vp, tdn, preferred_element_type=F32)
                r_ref[pr] = r_old * cdb[:, pr * vw:(pr + 1) * vw] + upd

    @pl.when(ps == 1)
    def _forward_outputs():
        cdf = jnp.exp(C * lgvf_ref[...])
        lane = lax.broadcasted_iota(jnp.int32, (1, pw), 1)
        for cc in range(n_sub):
            rows = slice(cc * C, (cc + 1) * C)
            c = i * n_sub + cc
            q = rq_ref[0, rows, :].astype(F32)
            k = rk_ref[0, rows, :].astype(F32)
            for pr in range(n_pairs):
                sl = slice(pr * pw, (pr + 1) * pw)
                qp = q[:, sl]
                kpb = rk_ref[0, rows, sl]
                qf = qp * qdf_ref[:, sl]
                qb = qp * qdb_ref[:, sl]
                fb = f_ref[pr].astype(BF16)
                rb = rs_ref[c, pr]
                for hh in range(2):
                    hd = 2 * pr + hh
                    hm = (lane // RET_DK) == hh
                    vs = slice(hd * RET_DV, (hd + 1) * RET_DV)
                    fs = slice(hh * RET_DV, (hh + 1) * RET_DV)
                    a = lax.dot_general(jnp.where(hm, qp, 0.0).astype(BF16), kpb, ndn,
                                        preferred_element_type=F32)
                    a = (a * dm_ref[hd]).astype(BF16)
                    o = jnp.dot(a, rv_ref[0, rows, vs], preferred_element_type=F32)
                    o = o + jnp.dot(jnp.where(hm, qf, 0.0).astype(BF16), fb[:, fs],
                                    preferred_element_type=F32)
                    o = o + jnp.dot(jnp.where(hm, qb, 0.0).astype(BF16), rb[:, fs],
                                    preferred_element_type=F32)
                    o = o * lax.rsqrt(jnp.mean(o * o, axis=-1, keepdims=True) + EPS) * gout_ref[:, vs]
                    g = rg_ref[0, rows, vs]
                    o_ref[0, rows, vs] = (o * (g * _sigmoid(g))).astype(BF16)
                kp = (k[:, sl] * kdf_ref[:, sl]).astype(BF16)
                vp = rv_ref[0, rows, pr * vw:(pr + 1) * vw]
                upd = lax.dot_general(kp, vp, tdn, preferred_element_type=F32)
                f_ref[pr] = f_ref[pr] * cdf[:, pr * vw:(pr + 1) * vw] + upd


def _retention(log_gamma, g_ret_out, rq, rk, rv, rg, rk_c, rv_c, C, R):
    B, L, n_qk = rq.shape
    n_v = rv.shape[2]
    Lc = rk_c.shape[1]
    n = L // R
    lgf = jnp.repeat(log_gamma[0], RET_DK)[None, :]
    lgb = jnp.repeat(log_gamma[1], RET_DK)[None, :]
    lgvf = jnp.repeat(log_gamma[0], RET_DV)[None, :]
    lgvb = jnp.repeat(log_gamma[1], RET_DV)[None, :]
    chunk = lambda p, i: jnp.where(p == 0, n - 1 - i, i)
    fwd_only = lambda p, i: jnp.where(p == 0, 0, i)
    const2 = lambda b, p, i: (0, 0)
    n_pairs = RET_HEADS // 2
    return pl.pallas_call(
        functools.partial(_ret_kernel, n_steps=n, C=C),
        grid=(B, 2, n),
        in_specs=[
            pl.BlockSpec(memory_space=pltpu.SMEM),
            pl.BlockSpec((1, n_qk), const2),
            pl.BlockSpec((1, n_qk), const2),
            pl.BlockSpec((1, n_v), const2),
            pl.BlockSpec((1, n_v), const2),
            pl.BlockSpec((1, n_v), const2),
            pl.BlockSpec((1, R, n_qk), lambda b, p, i: (b, fwd_only(p, i), 0)),
            pl.BlockSpec((1, R, n_qk), lambda b, p, i: (b, chunk(p, i), 0)),
            pl.BlockSpec((1, R, n_v), lambda b, p, i: (b, chunk(p, i), 0)),
            pl.BlockSpec((1, R, n_v), lambda b, p, i: (b, fwd_only(p, i), 0)),
            pl.BlockSpec((1, Lc, n_qk), lambda b, p, i: (b, 0, 0)),
            pl.BlockSpec((1, Lc, n_v), lambda b, p, i: (b, 0, 0)),
        ],
        out_specs=pl.BlockSpec((1, R, n_v), lambda b, p, i: (b, fwd_only(p, i), 0)),
        out_shape=jax.ShapeDtypeStruct((B, L, n_v), BF16),
        scratch_shapes=[
            pltpu.VMEM((RET_HEADS, C, C), F32),
            pltpu.VMEM((C, n_qk), F32), pltpu.VMEM((C, n_qk), F32),
            pltpu.VMEM((C, n_qk), F32), pltpu.VMEM((C, n_qk), F32),
            pltpu.VMEM((n_pairs, 2 * RET_DK, 2 * RET_DV), F32),
            pltpu.VMEM((n_pairs, 2 * RET_DK, 2 * RET_DV), F32),
            pltpu.VMEM((L // C, n_pairs, 2 * RET_DK, 2 * RET_DV), BF16),
        ],
        compiler_params=_cparams(("arbitrary", "arbitrary", "arbitrary")),
        name="retention",
    )(log_gamma, lgf, lgb, lgvf, lgvb, g_ret_out, rq, rk, rv, rg, rk_c, rv_c)


def _out_router_kernel(mla_ref, ret_ref, x_ref, mod_ref, wo_ref, gffn_ref, wr_ref, br_ref,
                       x1_ref, hf_ref, idx_ref, rank_ref, gate_ref, cnt_ref):
    tm = x_ref.shape[0]
    n_mla = mla_ref.shape[1]
    y = jnp.dot(mla_ref[...], wo_ref[0:n_mla, :], preferred_element_type=F32)
    y = y + jnp.dot(ret_ref[...], wo_ref[n_mla:, :], preferred_element_type=F32)
    x1 = x_ref[...] + mod_ref[0, 2:3, :] * y
    x1_ref[...] = x1
    ms = jnp.mean(x1 * x1, axis=-1, keepdims=True)
    hf = x1 * lax.rsqrt(ms + EPS) * gffn_ref[...]
    hf = hf * (1.0 + mod_ref[0, 4:5, :]) + mod_ref[0, 3:4, :]
    for c in range(hf.shape[1] // LANES):
        hf_ref[pl.ds(c, tm, stride=SUBLANES), :] = hf[:, c * LANES:(c + 1) * LANES]

    hi = hf.astype(BF16)
    lo = (hf - hi.astype(F32)).astype(BF16)
    both = jnp.dot(hi, wr_ref[...], preferred_element_type=F32)
    logits = (both[:, :LANES] + both[:, LANES:]
              + jnp.dot(lo, wr_ref[:, :LANES], preferred_element_type=F32)) + br_ref[...]
    lt = logits.T[0:N_EXPERTS, :]
    e_iota = lax.broadcasted_iota(jnp.int32, (N_EXPERTS, tm), 0).astype(F32)
    vals, idxs, hits = [], [], []
    for _ in range(TOP_K):
        mx = jnp.max(lt, axis=0, keepdims=True)
        ix = jnp.min(jnp.where(lt == mx, e_iota, float(N_EXPERTS)), axis=0, keepdims=True)
        hit = e_iota == ix
        lt = jnp.where(hit, -jnp.inf, lt)
        vals.append(mx)
        idxs.append(ix)
        hits.append(hit.astype(F32))

    s_iota = lax.broadcasted_iota(jnp.int32, (tm, tm), 0)
    t_iota = lax.broadcasted_iota(jnp.int32, (tm, tm), 1)
    upper = (s_iota <= t_iota).astype(F32).astype(BF16)
    prefix = jnp.dot(jnp.concatenate(hits, axis=0).astype(BF16), upper,
                     preferred_element_type=F32)
    ranks = []
    seen = jnp.zeros((N_EXPERTS, 1), F32)
    for k in range(TOP_K):
        pk = prefix[k * N_EXPERTS:(k + 1) * N_EXPERTS, :]
        rank = jnp.sum(hits[k] * (pk - 1.0 + seen), axis=0, keepdims=True)
        seen = seen + jnp.sum(hits[k], axis=1, keepdims=True)
        ranks.append(rank.astype(jnp.int32))
    ex = [jnp.exp(v - vals[0]) for v in vals]
    den = ex[0] + ex[1] + ex[2] + ex[3]
    idx_ref[...] = jnp.concatenate(idxs, axis=0).astype(jnp.int32)
    rank_ref[...] = jnp.concatenate(ranks, axis=0)
    gate_ref[...] = jnp.concatenate([e / den for e in ex], axis=0)
    cnt_ref[0] = jnp.broadcast_to(seen, (N_EXPERTS, LANES)).astype(jnp.int32)


def _out_router(mla, ret, x2, mod3, w_out_b, g_ffn, w_r, b_r, L, tm):
    T, D = x2.shape
    n_tiles = T // tm
    per_b = L // tm
    const = lambda i: (0, 0)
    return pl.pallas_call(
        _out_router_kernel,
        grid=(n_tiles,),
        in_specs=[
            pl.BlockSpec((tm, mla.shape[1]), lambda i: (i, 0)),
            pl.BlockSpec((tm, ret.shape[1]), lambda i: (i, 0)),
            pl.BlockSpec((tm, D), lambda i: (i, 0)),
            pl.BlockSpec((1, N_MOD, D), lambda i: (i // per_b, 0, 0)),
            pl.BlockSpec(w_out_b.shape, const),
            pl.BlockSpec((1, D), const),
            pl.BlockSpec(w_r.shape, const),
            pl.BlockSpec((1, LANES), const),
        ],
        out_specs=[
            pl.BlockSpec((tm, D), lambda i: (i, 0)),
            pl.BlockSpec((tm * SUBLANES, LANES), lambda i: (i, 0)),
            pl.BlockSpec((TOP_K, tm), lambda i: (0, i)),
            pl.BlockSpec((TOP_K, tm), lambda i: (0, i)),
            pl.BlockSpec((TOP_K, tm), lambda i: (0, i)),
            pl.BlockSpec((1, N_EXPERTS, LANES), lambda i: (i, 0, 0)),
        ],
        out_shape=[
            jax.ShapeDtypeStruct((T, D), F32),
            jax.ShapeDtypeStruct((T * SUBLANES, LANES), F32),
            jax.ShapeDtypeStruct((TOP_K, T), jnp.int32),
            jax.ShapeDtypeStruct((TOP_K, T), jnp.int32),
            jax.ShapeDtypeStruct((TOP_K, T), F32),
            jax.ShapeDtypeStruct((n_tiles, N_EXPERTS, LANES), jnp.int32),
        ],
        compiler_params=_cparams(("arbitrary",)),
        name="out_router",
    )(mla, ret, x2, mod3, w_out_b, g_ffn, w_r, b_r)


def _row_copy(src, dst, sem):
    return pltpu.make_async_copy(src, dst, sem)


def _dispatch_kernel(dest_ref, hf_ref, w1_ref, w2_ref, xs_ref, g_ref, l_ref, w2b_ref, sem, *, td):
    def issue(t, carry):
        src = hf_ref.at[pl.ds(pl.multiple_of(t * SUBLANES, SUBLANES), SUBLANES), :]
        for k in range(TOP_K):
            d = dest_ref[t * TOP_K + k]
            dst = xs_ref.at[pl.ds(pl.multiple_of(d * SUBLANES, SUBLANES), SUBLANES), :]
            _row_copy(src, dst, sem).start(priority=k % 2)
        return carry

    lax.fori_loop(0, td, issue, 0, unroll=ISSUE_UNROLL)

    w = w1_ref[0].astype(BF16)
    sub = 2 * LANES
    r = lax.broadcasted_iota(jnp.int32, (sub, sub), 0)
    c = lax.broadcasted_iota(jnp.int32, (sub, sub), 1)
    src_col = jnp.where(c < LANES, 2 * c, 2 * (c - LANES) + 1)
    sel = (r == src_col).astype(F32).astype(BF16)
    for s in range(w.shape[1] // sub):
        t = jnp.dot(w[:, s * sub:(s + 1) * sub], sel, preferred_element_type=F32).astype(BF16)
        g_ref[0, :, s * LANES:(s + 1) * LANES] = t[:, :LANES]
        l_ref[0, :, s * LANES:(s + 1) * LANES] = t[:, LANES:]
    w2b_ref[0] = w2_ref[0].astype(BF16)

    for k in range(TOP_K):
        _row_copy(hf_ref, xs_ref.at[pl.ds(0, td * SUBLANES), :], sem).wait()


def _dispatch_prep(dest_flat, hf8, w1, w2, n_rows):
    T = hf8.shape[0] // SUBLANES
    E, d, f2 = w1.shape
    f = f2 // 2
    halves = 2
    n = E * halves
    assert T % (n * SUBLANES) == 0
    td = T // n
    half_w1 = jax.ShapeDtypeStruct((E, d, f), BF16)
    return pl.pallas_call(
        functools.partial(_dispatch_kernel, td=td),
        grid=(n,),
        in_specs=[
            pl.BlockSpec((td * TOP_K,), lambda i: (i,), memory_space=pltpu.SMEM),
            pl.BlockSpec((td * SUBLANES, LANES), lambda i: (i, 0)),
            pl.BlockSpec((1, d, f2 // halves), lambda i: (i // halves, 0, i % halves)),
            pl.BlockSpec((1, f // halves, d), lambda i: (i // halves, i % halves, 0)),
        ],
        out_specs=[
            pl.BlockSpec(memory_space=pl.ANY),
            pl.BlockSpec((1, d, f // halves), lambda i: (i // halves, 0, i % halves)),
            pl.BlockSpec((1, d, f // halves), lambda i: (i // halves, 0, i % halves)),
            pl.BlockSpec((1, f // halves, d), lambda i: (i // halves, i % halves, 0)),
        ],
        out_shape=[jax.ShapeDtypeStruct((n_rows * SUBLANES, LANES), F32), half_w1, half_w1,
                   jax.ShapeDtypeStruct(w2.shape, BF16)],
        scratch_shapes=[pltpu.SemaphoreType.DMA],
        compiler_params=_cparams(("arbitrary",)),
        name="dispatch_prep",
    )(dest_flat, hf8, w1, w2)


def _experts_kernel(ie_ref, ib_ref, lo_ref, hi_ref, first_ref, ni_ref, xs_ref, w1g_ref, w1l_ref,
                    w2_ref, b1g_ref, b1l_ref, b2_ref, ys_ref):
    i = pl.program_id(0)
    blk = xs_ref.shape[0] // SUBLANES
    d = w1g_ref.shape[1]

    @pl.when(i < ni_ref[0])
    def _():
        cols = [xs_ref[pl.ds(c, blk, stride=SUBLANES), :] for c in range(d // LANES)]
        x = jnp.concatenate(cols, axis=-1)
        row = lax.broadcasted_iota(jnp.int32, (blk, 1), 0)
        mine = (row >= lo_ref[i]) & (row < hi_ref[i])
        x = jnp.where(mine, x, 0.0).astype(BF16)
        hg = jnp.dot(x, w1g_ref[0], preferred_element_type=F32) + b1g_ref[0]
        hl = jnp.dot(x, w1l_ref[0], preferred_element_type=F32) + b1l_ref[0]
        glu = jnp.minimum(hg, SWIGLU_LIMIT)
        lin = jnp.clip(hl, -SWIGLU_LIMIT, SWIGLU_LIMIT)
        act = glu * _sigmoid(SWIGLU_ALPHA * glu) * (lin + 1.0)
        y = jnp.dot(act.astype(BF16), w2_ref[0], preferred_element_type=F32) + b2_ref[0]
        y = jnp.where(mine, y, 0.0)

        @pl.when(first_ref[i] == 1)
        def _():
            for c in range(d // LANES):
                ys_ref[pl.ds(c, blk, stride=SUBLANES), :] = y[:, c * LANES:(c + 1) * LANES]

        @pl.when(first_ref[i] == 0)
        def _():
            for c in range(d // LANES):
                ys_ref[pl.ds(c, blk, stride=SUBLANES), :] += y[:, c * LANES:(c + 1) * LANES]


def _experts(items, xs, w1g, w1l, w2, b1g, b1l, b2, blk):
    item_e, item_blk, item_lo, item_hi, item_first, n_items = items
    d = w1g.shape[1]
    f = w1g.shape[2]
    row_map = lambda i, ie, ib, lo, hi, fi, ni: (ib[i], 0)
    exp_map = lambda i, ie, ib, lo, hi, fi, ni: (ie[i], 0, 0)
    grid_spec = pltpu.PrefetchScalarGridSpec(
        num_scalar_prefetch=6,
        grid=(item_e.shape[0],),
        in_specs=[
            pl.BlockSpec((blk * SUBLANES, LANES), row_map),
            pl.BlockSpec((1, d, f), exp_map),
            pl.BlockSpec((1, d, f), exp_map),
            pl.BlockSpec((1, f, d), exp_map),
            pl.BlockSpec((1, 1, f), exp_map),
            pl.BlockSpec((1, 1, f), exp_map),
            pl.BlockSpec((1, 1, d), exp_map),
        ],
        out_specs=pl.BlockSpec((blk * SUBLANES, LANES), row_map),
    )
    return pl.pallas_call(
        _experts_kernel,
        grid_spec=grid_spec,
        out_shape=jax.ShapeDtypeStruct(xs.shape, F32),
        compiler_params=_cparams(("arbitrary",)),
        name="experts",
    )(item_e, item_blk, item_lo, item_hi, item_first, n_items, xs, w1g, w1l, w2, b1g, b1l, b2)


def _combine_kernel(dest_ref, dnext_ref, dnext2_ref, x1_ref, gate_ref, mod_ref, ys_ref, o_ref,
                    buf_ref, sems, *, tc):
    i = pl.program_id(0)
    n = pl.num_programs(0)
    slot = i % COMB_SLOTS
    n_groups = tc // SUBLANES

    def issue_token(idx_ref, s, t):
        for k in range(TOP_K):
            d = idx_ref[t * TOP_K + k]
            src = ys_ref.at[pl.ds(pl.multiple_of(d * SUBLANES, SUBLANES), SUBLANES), :]
            dst = buf_ref.at[s, k, pl.ds(pl.multiple_of(t * SUBLANES, SUBLANES), SUBLANES), :]
            _row_copy(src, dst, sems.at[s]).start(priority=k % 2)

    def gather_all(idx_ref, s):
        def body(t, carry):
            issue_token(idx_ref, s, t)
            return carry
        lax.fori_loop(0, tc, body, 0, unroll=ISSUE_UNROLL)

    def reduce_group(g):
        t0 = pl.multiple_of(g * SUBLANES, SUBLANES)
        gates = gate_ref[pl.ds(t0, SUBLANES), :]
        for c in range(o_ref.shape[1] // LANES):
            cs = slice(c * LANES, (c + 1) * LANES)
            rows = pl.ds(t0 * SUBLANES + c, SUBLANES, stride=SUBLANES)
            acc = gates[:, 0:1] * buf_ref[slot, 0, rows, :]
            for k in range(1, TOP_K):
                acc = acc + gates[:, k:k + 1] * buf_ref[slot, k, rows, :]
            o_ref[pl.ds(t0, SUBLANES), cs] = (x1_ref[pl.ds(t0, SUBLANES), cs]
                                              + mod_ref[0, 5:6, cs] * acc)

    @pl.when(i == 0)
    def _():
        gather_all(dest_ref, 0)

        @pl.when(n > 1)
        def _():
            gather_all(dnext_ref, 1)

    for k in range(TOP_K):
        _row_copy(ys_ref.at[pl.ds(0, tc * SUBLANES), :], buf_ref.at[slot, k], sems.at[slot]).wait()

    @pl.when(i + 2 < n)
    def _():
        nxt = (i + 2) % COMB_SLOTS

        def body(gg, carry):
            for u in range(COMB_GROUPS * SUBLANES):
                issue_token(dnext2_ref, nxt, gg * (COMB_GROUPS * SUBLANES) + u)
            for v in range(COMB_GROUPS):
                reduce_group(gg * COMB_GROUPS + v)
            return carry
        lax.fori_loop(0, n_groups // COMB_GROUPS, body, 0)

    @pl.when(i + 2 >= n)
    def _():
        def body(gg, carry):
            for v in range(COMB_GROUPS):
                reduce_group(gg * COMB_GROUPS + v)
            return carry
        lax.fori_loop(0, n_groups // COMB_GROUPS, body, 0)


def _combine(dest_flat, x1, gates_t, mod3, ys, L, tc):
    T, D = x1.shape
    per_b = L // tc
    n = T // tc
    ahead = lambda a: pl.BlockSpec((tc * TOP_K,), lambda i: (jnp.minimum(i + a, n - 1),),
                                   memory_space=pltpu.SMEM)
    return pl.pallas_call(
        functools.partial(_combine_kernel, tc=tc),
        grid=(n,),
        in_specs=[
            ahead(0), ahead(1), ahead(2),
            pl.BlockSpec((tc, D), lambda i: (i, 0)),
            pl.BlockSpec((tc, TOP_K), lambda i: (i, 0)),
            pl.BlockSpec((1, N_MOD, D), lambda i: (i // per_b, 0, 0)),
            pl.BlockSpec(memory_space=pl.ANY),
        ],
        out_specs=pl.BlockSpec((tc, D), lambda i: (i, 0)),
        out_shape=jax.ShapeDtypeStruct((T, D), F32),
        scratch_shapes=[pltpu.VMEM((COMB_SLOTS, TOP_K, tc * SUBLANES, LANES), F32),
                        pltpu.SemaphoreType.DMA((COMB_SLOTS,))],
        compiler_params=_cparams(("arbitrary",)),
        name="combine",
    )(dest_flat, dest_flat, dest_flat, x1, gates_t, mod3, ys)


def _rope_tables(L, dim, lane_off, width):
    rows = L // GRID_W
    nf = dim // 4
    inv = jnp.power(ROPE_BASE, -jnp.arange(nf, dtype=F32) / nf)
    row = jnp.repeat(jnp.arange(rows, dtype=F32), GRID_W)
    col = jnp.tile(jnp.arange(GRID_W, dtype=F32), rows)
    pos = jnp.stack([row, col], axis=-1)
    ang = pos[:, :, None] * inv
    ang = jnp.broadcast_to(ang[:, :, None, :], (L, 2, 2, nf)).reshape(L, dim)
    sign = jnp.where((jnp.arange(dim) % (dim // 2)) < nf, -1.0, 1.0).astype(F32)
    cos, sin = jnp.cos(ang), jnp.sin(ang) * sign
    if lane_off is None:
        reps = width // dim
        return jnp.tile(cos, (1, reps)), jnp.tile(sin, (1, reps))
    cfull = jnp.ones((L, width), F32).at[:, lane_off:lane_off + dim].set(cos)
    sfull = jnp.zeros((L, width), F32).at[:, lane_off:lane_off + dim].set(sin)
    return cfull, sfull


def _identity_tables(L):
    return jnp.ones((L, LANES), F32), jnp.zeros((L, LANES), F32)


def _half_rot_src(dim):
    j = jnp.arange(dim)
    return jnp.where((j % (dim // 2)) < dim // 4, j + dim // 4, j - dim // 4)


def _prep_weights(w_in, w_q_up, w_kv_up, g_q_head, g_k_head):
    D = w_in.shape[0]
    o = 0
    wq = w_in[:, o:o + Q_LORA]; o += Q_LORA
    wkv = w_in[:, o:o + KV_LORA]; o += KV_LORA
    wpe = w_in[:, o:o + MLA_ROPE]; o += MLA_ROPE
    n_qk = RET_HEADS * RET_DK
    wrq = w_in[:, o:o + n_qk]; o += n_qk
    wrk = w_in[:, o:o + n_qk]; o += n_qk
    rest = w_in[:, o:]
    src_m = _half_rot_src(MLA_ROPE)
    src_r = _half_rot_src(RET_DK)
    rope_lanes = slice(MLA_NOPE, MLA_NOPE + MLA_ROPE)
    pe_blk = jnp.zeros((D, LANES), w_in.dtype).at[:, rope_lanes].set(wpe)
    pe_perm = jnp.zeros((D, LANES), w_in.dtype).at[:, rope_lanes].set(wpe[:, src_m])
    perm_heads = lambda w: w.reshape(D, RET_HEADS, RET_DK)[:, :, src_r].reshape(D, n_qk)
    w_in_r = jnp.concatenate([wq, wkv, pe_blk, pe_perm, wrq, perm_heads(wrq), wrk, perm_heads(wrk),
                              rest], axis=1).astype(BF16)

    pad_h = LANES - MLA_QK
    wq3 = w_q_up.reshape(Q_LORA, MLA_HEADS, MLA_QK)
    q_main = jnp.pad(wq3, ((0, 0), (0, 0), (0, pad_h)))
    q_perm = jnp.zeros_like(q_main).at[:, :, rope_lanes].set(wq3[:, :, MLA_NOPE:][:, :, src_m])
    w_q_r = jnp.concatenate([q_main.reshape(Q_LORA, -1), q_perm.reshape(Q_LORA, -1)], axis=1).astype(BF16)

    kv = w_kv_up.reshape(KV_LORA, MLA_HEADS, MLA_NOPE + MLA_V)
    kpart = jnp.pad(kv[:, :, :MLA_NOPE], ((0, 0), (0, 0), (0, LANES - MLA_NOPE)))
    vpart = jnp.pad(kv[:, :, MLA_NOPE:], ((0, 0), (0, 0), (0, LANES - MLA_V)))
    w_kv_r = jnp.concatenate([kpart.reshape(KV_LORA, -1), vpart.reshape(KV_LORA, -1)], axis=1).astype(BF16)

    def gains(g):
        main = jnp.pad(g, (0, pad_h))[None, :]
        perm = jnp.zeros((1, LANES), g.dtype).at[0, rope_lanes].set(g[MLA_NOPE:][src_m])
        return main, perm

    gqh, gqr = gains(g_q_head)
    gkh, gkr = gains(g_k_head)
    return w_in_r, w_q_r, w_kv_r, gqh, gqr, gkh, gkr


def _routing_tables(idx, rank, counts, tm, blk):
    T = idx.shape[1]
    i32 = jnp.int32
    tot = jnp.sum(counts, axis=0)
    end = jnp.cumsum(tot)
    start = end - tot
    tile_base = start[None, :] + jnp.cumsum(counts, axis=0) - counts
    base_tok = jnp.repeat(tile_base, tm, axis=0)
    hit = idx[:, :, None] == jnp.arange(N_EXPERTS, dtype=i32)
    dest = jnp.sum(jnp.where(hit, base_tok[None], 0), axis=-1) + rank

    n_work = (T * TOP_K) // blk + N_EXPERTS
    first_blk = start // blk
    last_blk = (end - 1) // blk
    per_e = jnp.where(tot > 0, last_blk - first_blk + 1, 0)
    item_end = jnp.cumsum(per_e)
    item_start = item_end - per_e
    n_items = item_end[-1]
    j = jnp.minimum(jnp.arange(n_work, dtype=i32), n_items - 1)
    item_e = jnp.minimum(jnp.sum(item_end[None, :] <= j[:, None], axis=1), N_EXPERTS - 1).astype(i32)
    onehot = item_e[:, None] == jnp.arange(N_EXPERTS, dtype=i32)[None, :]
    pick = lambda tab: jnp.sum(jnp.where(onehot, tab[None, :], 0), axis=1)
    item_blk = pick(first_blk) + j - pick(item_start)
    item_lo = jnp.clip(pick(start) - item_blk * blk, 0, blk)
    item_hi = jnp.clip(pick(end) - item_blk * blk, 0, blk)
    prev_blk = jnp.concatenate([jnp.full((1,), -1, i32), item_blk[:-1].astype(i32)])
    item_first = (item_blk != prev_blk).astype(i32)
    items = (item_e, item_blk.astype(i32), item_lo.astype(i32), item_hi.astype(i32), item_first,
             n_items.astype(i32).reshape(1))
    return dest.astype(i32), items


def kernel(x, c, ctx, c_ctx, g_attn, g_ffn, w_ada, b_ada, w_in, g_q_lora, w_q_up, g_q_head,
           g_kv_lora, w_kv_up, g_k_head, ret_decay_logit, g_ret_out, w_out, w_router, b_router,
           w_mlp1, b_mlp1, w_mlp2, b_mlp2):
    B, L, D = x.shape
    Lc = ctx.shape[1]
    T = B * L
    l = 0
    assert w_ada.shape[0] == 1

    rows = ((B + 1 + SUBLANES - 1) // SUBLANES) * SUBLANES
    cc = jnp.zeros((rows, D), F32).at[:B].set(c).at[B].set(c_ctx)
    mod3 = _adaln(cc, w_ada[l], b_ada[l][None, :]).reshape(rows, N_MOD, D)

    w_in_r, w_q_r, w_kv_r, gqh, gqr, gkh, gkr = _prep_weights(
        w_in[l], w_q_up[l], w_kv_up[l], g_q_head[l], g_k_head[l])
    tabs_x = _rope_tables(L, MLA_ROPE, MLA_NOPE, LANES) + _rope_tables(L, RET_DK, None, LANES)
    tabs_c = _identity_tables(Lc) + _identity_tables(Lc)
    proj = functools.partial(_in_proj, g_attn=g_attn[l][None, :], w_in_r=w_in_r,
                             g_q_lora=g_q_lora[l][None, :], w_q_r=w_q_r, gqh=gqh, gqr=gqr,
                             g_kv_lora=g_kv_lora[l][None, :], w_kv_r=w_kv_r, gkh=gkh, gkr=gkr)
    q, kx, vx, rq, rk, rv, rg = proj(x, mod3, lambda b: b, tabs=tabs_x, tm=min(PROJ_TM, L))
    _, kc, vc, _, rk_c, rv_c, _ = proj(ctx, mod3, lambda b: B, tabs=tabs_c, tm=Lc)

    mla = _attention(q, kx, vx, kc, vc, min(ATT_TQ, L), min(ATT_TK, L // 2))

    log_gamma = jax.nn.log_sigmoid(ret_decay_logit[l].astype(F32))
    ret = _retention(log_gamma, g_ret_out[l][None, :], rq, rk, rv, rg, rk_c, rv_c,
                     min(RET_C, L), min(RET_ROWS, L))

    w_r32 = jnp.pad(w_router[l], ((0, 0), (0, LANES - N_EXPERTS)))
    w_r_hi = w_r32.astype(BF16)
    w_r = jnp.concatenate([w_r_hi, (w_r32 - w_r_hi.astype(F32)).astype(BF16)], axis=1)
    b_r = jnp.pad(b_router[l], (0, LANES - N_EXPERTS))[None, :]
    tm = min(OUT_TM, L)
    x1, hf8, idx, rank, gates, cnt = _out_router(
        mla.reshape(T, -1), ret.reshape(T, -1), x.reshape(T, D), mod3,
        w_out[l].astype(BF16), g_ffn[l][None, :], w_r, b_r, L, tm)

    blk = MOE_BLK
    assert (T * TOP_K) % blk == 0
    dest, items = _routing_tables(idx, rank, cnt[:, :, 0], tm, blk)
    dest_flat = dest.T.reshape(-1)

    xs, w1g, w1l, w2b = _dispatch_prep(dest_flat, hf8, w_mlp1[l], w_mlp2[l], T * TOP_K)
    b1g = b_mlp1[l][:, None, 0::2]
    b1l = b_mlp1[l][:, None, 1::2]
    ys = _experts(items, xs, w1g, w1l, w2b, b1g, b1l, b_mlp2[l][:, None, :], blk)

    out = _combine(dest_flat, x1, gates.T, mod3, ys, L, min(COMB_T, L))
    return out.reshape(B, L, D)
```

```python
import functools

import jax
import jax.numpy as jnp
from jax import lax
from jax.experimental import pallas as pl
from jax.experimental.pallas import tpu as pltpu

F32 = jnp.float32
BF16 = jnp.bfloat16

LANES = 128
SUBLANES = 8
VMEM_LIMIT_BYTES = 56 * 1024 * 1024

EPS = 1e-6
ROPE_BASE = 10000.0
GRID_W = 64
N_MOD = 6
MLA_HEADS = 8
MLA_NOPE = 64
MLA_ROPE = 32
MLA_QK = MLA_NOPE + MLA_ROPE
MLA_V = 64
Q_LORA = 256
KV_LORA = 128
RET_HEADS = 4
RET_DK = 64
RET_DV = 128
N_EXPERTS = 32
TOP_K = 4
SWIGLU_LIMIT = 7.0
SWIGLU_ALPHA = 1.702
LOG2E = 1.4426950408889634

ADALN_TN = 1536
PROJ_TM = 512
ATT_TQ = 512
ATT_TK = 1024
RET_C = 256
RET_ROWS = 2048
OUT_TM = 512
MOE_BLK = 512
COMB_T = 256
COMB_SLOTS = 3
COMB_GROUPS = 4
ISSUE_UNROLL = 8


def _cparams(sem):
    return pltpu.CompilerParams(dimension_semantics=sem, vmem_limit_bytes=VMEM_LIMIT_BYTES)


def _sigmoid(x):
    return 1.0 / (1.0 + jnp.exp(-x))


def _adaln_kernel(c_ref, w_ref, b_ref, o_ref):
    c = c_ref[...]
    s = (c * _sigmoid(c)).astype(BF16)
    o_ref[...] = jnp.dot(s, w_ref[...].astype(BF16), preferred_element_type=F32) + b_ref[...]


def _adaln(cc, w_ada, b_ada):
    rows, d = cc.shape
    n = w_ada.shape[1]
    tn = ADALN_TN
    return pl.pallas_call(
        _adaln_kernel,
        grid=(n // tn,),
        in_specs=[pl.BlockSpec((rows, d), lambda j: (0, 0)),
                  pl.BlockSpec((d, tn), lambda j: (0, j)),
                  pl.BlockSpec((1, tn), lambda j: (0, j))],
        out_specs=pl.BlockSpec((rows, tn), lambda j: (0, j)),
        out_shape=jax.ShapeDtypeStruct((rows, n), F32),
        compiler_params=_cparams(("arbitrary",)),
        name="adaln",
    )(cc, w_ada, b_ada)


def _in_proj_kernel(x_ref, mod_ref, gattn_ref, win_ref, gql_ref, wq_ref, gqh_ref, gqr_ref,
                    gkvl_ref, wkv_ref, gkh_ref, gkr_ref, cm_ref, sm_ref, cr_ref, sr_ref,
                    q_ref, k_ref, v_ref, rq_ref, rk_ref, rv_ref, rg_ref, *, q_scale):
    x = x_ref[0]
    shift = mod_ref[0, 0:1, :]
    scale = mod_ref[0, 1:2, :]
    ms = jnp.mean(x * x, axis=-1, keepdims=True)
    h = x * lax.rsqrt(ms + EPS) * gattn_ref[...]
    h = h * (1.0 + scale) + shift
    p = jnp.dot(h.astype(BF16), win_ref[...], preferred_element_type=F32)

    lane = lax.broadcasted_iota(jnp.int32, (1, LANES), 1)
    cm, sm = cm_ref[...], sm_ref[...]
    cr, sr = cr_ref[...], sr_ref[...]
    n_hl = MLA_HEADS * LANES

    cq = p[:, 0:Q_LORA]
    cq = cq * lax.rsqrt(jnp.mean(cq * cq, axis=-1, keepdims=True) + EPS) * gql_ref[...]
    qf = jnp.dot(cq.astype(BF16), wq_ref[...], preferred_element_type=F32)
    gq_cos = gqh_ref[...] * cm
    gq_sin = gqr_ref[...] * sm
    for hd in range(MLA_HEADS):
        blk = qf[:, hd * LANES:(hd + 1) * LANES]
        perm = qf[:, n_hl + hd * LANES:n_hl + (hd + 1) * LANES]
        r = lax.rsqrt(jnp.sum(blk * blk, axis=-1, keepdims=True) * (1.0 / MLA_QK) + EPS)
        q_ref[0, hd] = ((blk * gq_cos + perm * gq_sin) * (r * q_scale)).astype(BF16)

    o_kv = Q_LORA
    ckv = p[:, o_kv:o_kv + KV_LORA]
    ckv = ckv * lax.rsqrt(jnp.mean(ckv * ckv, axis=-1, keepdims=True) + EPS) * gkvl_ref[...]
    kvf = jnp.dot(ckv.astype(BF16), wkv_ref[...], preferred_element_type=F32)
    o_pe = o_kv + KV_LORA
    pe = p[:, o_pe:o_pe + LANES]
    pe_perm = p[:, o_pe + LANES:o_pe + 2 * LANES]
    gk = gkh_ref[...]
    pe_rope = pe * (gk * cm) + pe_perm * (gkr_ref[...] * sm)
    pe_ss = jnp.sum(pe * pe, axis=-1, keepdims=True)
    ones_col = (lane == MLA_V).astype(F32)
    for hd in range(MLA_HEADS):
        kn = kvf[:, hd * LANES:(hd + 1) * LANES]
        ss = jnp.sum(kn * kn, axis=-1, keepdims=True) + pe_ss
        r = lax.rsqrt(ss * (1.0 / MLA_QK) + EPS)
        kk = (kn * gk + pe_rope) * r
        k_ref[0, hd] = kk.T.astype(BF16)
        vv = kvf[:, (MLA_HEADS + hd) * LANES:(MLA_HEADS + hd + 1) * LANES] + ones_col
        v_ref[0, hd] = vv.astype(BF16)

    n_qk = RET_HEADS * RET_DK
    o_rq = o_pe + 2 * LANES
    o_rk = o_rq + 2 * n_qk
    for j in range(n_qk // LANES):
        js = slice(j * LANES, (j + 1) * LANES)
        a = p[:, o_rq + j * LANES:o_rq + (j + 1) * LANES]
        b = p[:, o_rq + n_qk + j * LANES:o_rq + n_qk + (j + 1) * LANES]
        rq_ref[0, :, js] = (a * cr + b * sr).astype(BF16)
        a = p[:, o_rk + j * LANES:o_rk + (j + 1) * LANES]
        b = p[:, o_rk + n_qk + j * LANES:o_rk + n_qk + (j + 1) * LANES]
        rk_ref[0, :, js] = ((a * cr + b * sr) * (RET_DK ** -0.5)).astype(BF16)
    o_rv = o_rk + 2 * n_qk
    n_v = RET_HEADS * RET_DV
    rv_ref[0] = p[:, o_rv:o_rv + n_v].astype(BF16)
    rg_ref[0] = p[:, o_rv + n_v:o_rv + 2 * n_v]


def _in_proj(x, mod3, mod_row_of_batch, g_attn, w_in_r, g_q_lora, w_q_r, gqh, gqr, g_kv_lora,
             w_kv_r, gkh, gkr, tabs, tm):
    B, L, D = x.shape
    cm, sm, cr, sr = tabs
    n_in = w_in_r.shape[1]
    const = lambda b, i: (0, 0)
    tab_spec = pl.BlockSpec((tm, LANES), lambda b, i: (i, 0))
    head_spec = pl.BlockSpec((1, MLA_HEADS, tm, LANES), lambda b, i: (b, 0, i, 0))
    n_qk = RET_HEADS * RET_DK
    n_v = RET_HEADS * RET_DV
    seq_spec = lambda w: pl.BlockSpec((1, tm, w), lambda b, i: (b, i, 0))
    head_shape = jax.ShapeDtypeStruct((B, MLA_HEADS, L, LANES), BF16)
    q_scale = MLA_QK ** -0.5 * LOG2E
    return pl.pallas_call(
        functools.partial(_in_proj_kernel, q_scale=q_scale),
        grid=(B, L // tm),
        in_specs=[
            pl.BlockSpec((1, tm, D), lambda b, i: (b, i, 0)),
            pl.BlockSpec((1, N_MOD, D), lambda b, i: (mod_row_of_batch(b), 0, 0)),
            pl.BlockSpec((1, D), const),
            pl.BlockSpec((D, n_in), const),
            pl.BlockSpec((1, Q_LORA), const),
            pl.BlockSpec(w_q_r.shape, const),
            pl.BlockSpec((1, LANES), const),
            pl.BlockSpec((1, LANES), const),
            pl.BlockSpec((1, KV_LORA), const),
            pl.BlockSpec(w_kv_r.shape, const),
            pl.BlockSpec((1, LANES), const),
            pl.BlockSpec((1, LANES), const),
            tab_spec, tab_spec, tab_spec, tab_spec,
        ],
        out_specs=[head_spec,
                   pl.BlockSpec((1, MLA_HEADS, LANES, tm), lambda b, i: (b, 0, 0, i)),
                   head_spec,
                   seq_spec(n_qk), seq_spec(n_qk), seq_spec(n_v), seq_spec(n_v)],
        out_shape=[head_shape,
                   jax.ShapeDtypeStruct((B, MLA_HEADS, LANES, L), BF16),
                   head_shape,
                   jax.ShapeDtypeStruct((B, L, n_qk), BF16),
                   jax.ShapeDtypeStruct((B, L, n_qk), BF16),
                   jax.ShapeDtypeStruct((B, L, n_v), BF16),
                   jax.ShapeDtypeStruct((B, L, n_v), F32)],
        compiler_params=_cparams(("arbitrary", "arbitrary")),
        name="in_proj",
    )(x, mod3, g_attn, w_in_r, g_q_lora, w_q_r, gqh, gqr, g_kv_lora, w_kv_r, gkh, gkr,
      cm, sm, cr, sr)


def _attn_kernel(q_ref, kx_ref, vx_ref, kc_ref, vc_ref, o_ref, m_ref, acc_ref, s_ref, *, tk):
    n_kv = kx_ref.shape[3] // tk
    n_heads = q_ref.shape[1]

    def scores(hh, start, slot):
        kb = kx_ref[0, hh, :, pl.ds(start, tk)]
        s_ref[hh, slot] = jnp.dot(q_ref[0, hh], kb, preferred_element_type=F32)

    def consume(hh, start, slot):
        s = s_ref[hh, slot]
        m_old = m_ref[hh]
        m_new = jnp.maximum(m_old, jnp.max(s, axis=-1, keepdims=True))
        alpha = jnp.exp2(m_old - m_new)
        pr = jnp.exp2(s - m_new)
        vb = vx_ref[0, hh, pl.ds(start, tk), :]
        acc_ref[hh] = alpha * acc_ref[hh] + jnp.dot(pr.astype(BF16), vb,
                                                    preferred_element_type=F32)
        m_ref[hh] = m_new

    def step(j, slot, prefetch):
        if prefetch:
            nxt = pl.multiple_of((j + 1) * tk, tk)
            for hh in range(n_heads):
                scores(hh, nxt, 1 - slot)
        cur = pl.multiple_of(j * tk, tk)
        for hh in range(n_heads):
            consume(hh, cur, slot)

    for hh in range(n_heads):
        scores(hh, 0, 0)
        s = jnp.dot(q_ref[0, hh], kc_ref[0, hh], preferred_element_type=F32)
        m0 = jnp.max(s, axis=-1, keepdims=True)
        m_ref[hh] = m0
        acc_ref[hh] = jnp.dot(jnp.exp2(s - m0).astype(BF16), vc_ref[0, hh],
                              preferred_element_type=F32)

    def body(jj, carry):
        step(2 * jj, 0, True)
        step(2 * jj + 1, 1, True)
        return carry

    lax.fori_loop(0, n_kv // 2 - 1, body, 0)
    step(n_kv - 2, 0, True)
    step(n_kv - 1, 1, False)
    outs = []
    for hh in range(n_heads):
        acc = acc_ref[hh]
        outs.append(acc[:, :MLA_V] / acc[:, MLA_V:MLA_V + 1])
    o_ref[0] = jnp.concatenate(outs, axis=-1).astype(BF16)


def _attention(q, kx, vx, kc, vc, tq, tk):
    B, H, L, _ = q.shape
    Lc = kc.shape[3]
    return pl.pallas_call(
        functools.partial(_attn_kernel, tk=tk),
        grid=(B, H // 2, L // tq),
        in_specs=[
            pl.BlockSpec((1, 2, tq, LANES), lambda b, h, i: (b, h, i, 0)),
            pl.BlockSpec((1, 2, LANES, L), lambda b, h, i: (b, h, 0, 0)),
            pl.BlockSpec((1, 2, L, LANES), lambda b, h, i: (b, h, 0, 0)),
            pl.BlockSpec((1, 2, LANES, Lc), lambda b, h, i: (b, h, 0, 0)),
            pl.BlockSpec((1, 2, Lc, LANES), lambda b, h, i: (b, h, 0, 0)),
        ],
        out_specs=pl.BlockSpec((1, tq, LANES), lambda b, h, i: (b, i, h)),
        out_shape=jax.ShapeDtypeStruct((B, L, H * MLA_V), BF16),
        scratch_shapes=[pltpu.VMEM((2, tq, 1), F32), pltpu.VMEM((2, tq, LANES), F32),
                        pltpu.VMEM((2, 2, tq, tk), F32)],
        compiler_params=_cparams(("arbitrary", "arbitrary", "arbitrary")),
        name="attention",
    )(q, kx, vx, kc, vc)


def _ret_kernel(lg_ref, lgf_ref, lgb_ref, lgvf_ref, lgvb_ref, gout_ref, rq_ref, rk_ref, rv_ref, rg_ref,
                kc_ref, vc_ref, o_ref, dm_ref, qdf_ref, qdb_ref, kdf_ref, kdb_ref,
                f_ref, r_ref, rs_ref, *, n_steps, C):
    n_sub = rq_ref.shape[1] // C
    Lc = kc_ref.shape[1]
    ps = pl.program_id(1)
    i = pl.program_id(2)
    n_pairs = RET_HEADS // 2
    pw = 2 * RET_DK
    vw = 2 * RET_DV
    tdn = (((0,), (0,)), ((), ()))
    ndn = (((1,), (1,)), ((), ()))
    lgf = lgf_ref[...]
    lgb = lgb_ref[...]

    @pl.when((pl.program_id(0) == 0) & (ps == 0) & (i == 0))
    def _tables():
        a = lax.broadcasted_iota(jnp.int32, (C, C), 0)
        b = lax.broadcasted_iota(jnp.int32, (C, C), 1)
        dab = (a - b).astype(F32)
        for hd in range(RET_HEADS):
            fwd = jnp.where(a >= b, jnp.exp(jnp.where(a >= b, dab, 0.0) * lg_ref[0, hd]), 0.0)
            bwd = jnp.where(b >= a, jnp.exp(jnp.where(b >= a, -dab, 0.0) * lg_ref[1, hd]), 0.0)
            dm_ref[hd] = fwd + bwd
        row = lax.broadcasted_iota(jnp.int32, (C, 1), 0).astype(F32)
        qdf_ref[...] = jnp.exp((row + 1.0) * lgf)
        qdb_ref[...] = jnp.exp((C - row) * lgb)
        kdf_ref[...] = jnp.exp((C - 1.0 - row) * lgf)
        kdb_ref[...] = jnp.exp(row * lgb)

    @pl.when((ps == 0) & (i == 0))
    def _init_states():
        rowc = lax.broadcasted_iota(jnp.int32, (Lc, 1), 0).astype(F32)
        wf = jnp.exp((Lc - 1.0 - rowc) * lgf)
        wb = jnp.exp(rowc * lgb)
        kc = kc_ref[0].astype(F32)
        for pr in range(n_pairs):
            kp = kc[:, pr * pw:(pr + 1) * pw]
            vp = vc_ref[0, :, pr * vw:(pr + 1) * vw]
            f_ref[pr] = lax.dot_general((kp * wf[:, pr * pw:(pr + 1) * pw]).astype(BF16), vp, tdn,
                                        preferred_element_type=F32)
            r_ref[pr] = lax.dot_general((kp * wb[:, pr * pw:(pr + 1) * pw]).astype(BF16), vp, tdn,
                                        preferred_element_type=F32)

    @pl.when(ps == 0)
    def _backward_states():
        cdb = jnp.exp(C * lgvb_ref[...])
        for cc in reversed(range(n_sub)):
            rows = slice(cc * C, (cc + 1) * C)
            c = (n_steps - 1 - i) * n_sub + cc
            k = rk_ref[0, rows, :].astype(F32)
            for pr in range(n_pairs):
                r_old = r_ref[pr]
                rs_ref[c, pr] = r_old.astype(BF16)
                kp = (k[:, pr * pw:(pr + 1) * pw] * kdb_ref[:, pr * pw:(pr + 1) * pw]).astype(BF16)
                vp = rv_ref[0, rows, pr * vw:(pr + 1) * vw]
                upd = lax.dot_general(kp, vp, tdn, preferred_element_type=F32)
                r_ref[pr] = r_old * cdb[:, pr * vw:(pr + 1) * vw] + upd

    @pl.when(ps == 1)
    def _forward_outputs():
        cdf = jnp.exp(C * lgvf_ref[...])
        lane = lax.broadcasted_iota(jnp.int32, (1, pw), 1)
        for cc in range(n_sub):
            rows = slice(cc * C, (cc + 1) * C)
            c = i * n_sub + cc
            q = rq_ref[0, rows, :].astype(F32)
            k = rk_ref[0, rows, :].astype(F32)
            for pr in range(n_pairs):
                sl = slice(pr * pw, (pr + 1) * pw)
                qp = q[:, sl]
                kpb = rk_ref[0, rows, sl]
                qf = qp * qdf_ref[:, sl]
                qb = qp * qdb_ref[:, sl]
                fb = f_ref[pr].astype(BF16)
                rb = rs_ref[c, pr]
                for hh in range(2):
                    hd = 2 * pr + hh
                    hm = (lane // RET_DK) == hh
                    vs = slice(hd * RET_DV, (hd + 1) * RET_DV)
                    fs = slice(hh * RET_DV, (hh + 1) * RET_DV)
                    a = lax.dot_general(jnp.where(hm, qp, 0.0).astype(BF16), kpb, ndn,
                                        preferred_element_type=F32)
                    a = (a * dm_ref[hd]).astype(BF16)
                    o = jnp.dot(a, rv_ref[0, rows, vs], preferred_element_type=F32)
                    o = o + jnp.dot(jnp.where(hm, qf, 0.0).astype(BF16), fb[:, fs],
                                    preferred_element_type=F32)
                    o = o + jnp.dot(jnp.where(hm, qb, 0.0).astype(BF16), rb[:, fs],
                                    preferred_element_type=F32)
                    o = o * lax.rsqrt(jnp.mean(o * o, axis=-1, keepdims=True) + EPS) * gout_ref[:, vs]
                    g = rg_ref[0, rows, vs]
                    o_ref[0, rows, vs] = (o * (g * _sigmoid(g))).astype(BF16)
                kp = (k[:, sl] * kdf_ref[:, sl]).astype(BF16)
                vp = rv_ref[0, rows, pr * vw:(pr + 1) * vw]
                upd = lax.dot_general(kp, vp, tdn, preferred_element_type=F32)
                f_ref[pr] = f_ref[pr] * cdf[:, pr * vw:(pr + 1) * vw] + upd


def _retention(log_gamma, g_ret_out, rq, rk, rv, rg, rk_c, rv_c, C, R):
    B, L, n_qk = rq.shape
    n_v = rv.shape[2]
    Lc = rk_c.shape[1]
    n = L // R
    lgf = jnp.repeat(log_gamma[0], RET_DK)[None, :]
    lgb = jnp.repeat(log_gamma[1], RET_DK)[None, :]
    lgvf = jnp.repeat(log_gamma[0], RET_DV)[None, :]
    lgvb = jnp.repeat(log_gamma[1], RET_DV)[None, :]
    chunk = lambda p, i: jnp.where(p == 0, n - 1 - i, i)
    fwd_only = lambda p, i: jnp.where(p == 0, 0, i)
    const2 = lambda b, p, i: (0, 0)
    n_pairs = RET_HEADS // 2
    return pl.pallas_call(
        functools.partial(_ret_kernel, n_steps=n, C=C),
        grid=(B, 2, n),
        in_specs=[
            pl.BlockSpec(memory_space=pltpu.SMEM),
            pl.BlockSpec((1, n_qk), const2),
            pl.BlockSpec((1, n_qk), const2),
            pl.BlockSpec((1, n_v), const2),
            pl.BlockSpec((1, n_v), const2),
            pl.BlockSpec((1, n_v), const2),
            pl.BlockSpec((1, R, n_qk), lambda b, p, i: (b, fwd_only(p, i), 0)),
            pl.BlockSpec((1, R, n_qk), lambda b, p, i: (b, chunk(p, i), 0)),
            pl.BlockSpec((1, R, n_v), lambda b, p, i: (b, chunk(p, i), 0)),
            pl.BlockSpec((1, R, n_v), lambda b, p, i: (b, fwd_only(p, i), 0)),
            pl.BlockSpec((1, Lc, n_qk), lambda b, p, i: (b, 0, 0)),
            pl.BlockSpec((1, Lc, n_v), lambda b, p, i: (b, 0, 0)),
        ],
        out_specs=pl.BlockSpec((1, R, n_v), lambda b, p, i: (b, fwd_only(p, i), 0)),
        out_shape=jax.ShapeDtypeStruct((B, L, n_v), BF16),
        scratch_shapes=[
            pltpu.VMEM((RET_HEADS, C, C), F32),
            pltpu.VMEM((C, n_qk), F32), pltpu.VMEM((C, n_qk), F32),
            pltpu.VMEM((C, n_qk), F32), pltpu.VMEM((C, n_qk), F32),
            pltpu.VMEM((n_pairs, 2 * RET_DK, 2 * RET_DV), F32),
            pltpu.VMEM((n_pairs, 2 * RET_DK, 2 * RET_DV), F32),
            pltpu.VMEM((L // C, n_pairs, 2 * RET_DK, 2 * RET_DV), BF16),
        ],
        compiler_params=_cparams(("arbitrary", "arbitrary", "arbitrary")),
        name="retention",
    )(log_gamma, lgf, lgb, lgvf, lgvb, g_ret_out, rq, rk, rv, rg, rk_c, rv_c)


def _out_router_kernel(mla_ref, ret_ref, x_ref, mod_ref, wo_ref, gffn_ref, wr_ref, br_ref,
                       x1_ref, hf_ref, idx_ref, rank_ref, gate_ref, cnt_ref):
    tm = x_ref.shape[0]
    n_mla = mla_ref.shape[1]
    y = jnp.dot(mla_ref[...], wo_ref[0:n_mla, :], preferred_element_type=F32)
    y = y + jnp.dot(ret_ref[...], wo_ref[n_mla:, :], preferred_element_type=F32)
    x1 = x_ref[...] + mod_ref[0, 2:3, :] * y
    x1_ref[...] = x1
    ms = jnp.mean(x1 * x1, axis=-1, keepdims=True)
    hf = x1 * lax.rsqrt(ms + EPS) * gffn_ref[...]
    hf = hf * (1.0 + mod_ref[0, 4:5, :]) + mod_ref[0, 3:4, :]
    for c in range(hf.shape[1] // LANES):
        hf_ref[pl.ds(c, tm, stride=SUBLANES), :] = hf[:, c * LANES:(c + 1) * LANES]

    hi = hf.astype(BF16)
    lo = (hf - hi.astype(F32)).astype(BF16)
    both = jnp.dot(hi, wr_ref[...], preferred_element_type=F32)
    logits = (both[:, :LANES] + both[:, LANES:]
              + jnp.dot(lo, wr_ref[:, :LANES], preferred_element_type=F32)) + br_ref[...]
    lt = logits.T[0:N_EXPERTS, :]
    e_iota = lax.broadcasted_iota(jnp.int32, (N_EXPERTS, tm), 0).astype(F32)
    vals, idxs, hits = [], [], []
    for _ in range(TOP_K):
        mx = jnp.max(lt, axis=0, keepdims=True)
        ix = jnp.min(jnp.where(lt == mx, e_iota, float(N_EXPERTS)), axis=0, keepdims=True)
        hit = e_iota == ix
        lt = jnp.where(hit, -jnp.inf, lt)
        vals.append(mx)
        idxs.append(ix)
        hits.append(hit.astype(F32))

    s_iota = lax.broadcasted_iota(jnp.int32, (tm, tm), 0)
    t_iota = lax.broadcasted_iota(jnp.int32, (tm, tm), 1)
    upper = (s_iota <= t_iota).astype(F32).astype(BF16)
    prefix = jnp.dot(jnp.concatenate(hits, axis=0).astype(BF16), upper,
                     preferred_element_type=F32)
    ranks = []
    seen = jnp.zeros((N_EXPERTS, 1), F32)
    for k in range(TOP_K):
        pk = prefix[k * N_EXPERTS:(k + 1) * N_EXPERTS, :]
        rank = jnp.sum(hits[k] * (pk - 1.0 + seen), axis=0, keepdims=True)
        seen = seen + jnp.sum(hits[k], axis=1, keepdims=True)
        ranks.append(rank.astype(jnp.int32))
    ex = [jnp.exp(v - vals[0]) for v in vals]
    den = ex[0] + ex[1] + ex[2] + ex[3]
    idx_ref[...] = jnp.concatenate(idxs, axis=0).astype(jnp.int32)
    rank_ref[...] = jnp.concatenate(ranks, axis=0)
    gate_ref[...] = jnp.concatenate([e / den for e in ex], axis=0)
    cnt_ref[0] = jnp.broadcast_to(seen, (N_EXPERTS, LANES)).astype(jnp.int32)


def _out_router(mla, ret, x2, mod3, w_out_b, g_ffn, w_r, b_r, L, tm):
    T, D = x2.shape
    n_tiles = T // tm
    per_b = L // tm
    const = lambda i: (0, 0)
    return pl.pallas_call(
        _out_router_kernel,
        grid=(n_tiles,),
        in_specs=[
            pl.BlockSpec((tm, mla.shape[1]), lambda i: (i, 0)),
            pl.BlockSpec((tm, ret.shape[1]), lambda i: (i, 0)),
            pl.BlockSpec((tm, D), lambda i: (i, 0)),
            pl.BlockSpec((1, N_MOD, D), lambda i: (i // per_b, 0, 0)),
            pl.BlockSpec(w_out_b.shape, const),
            pl.BlockSpec((1, D), const),
            pl.BlockSpec(w_r.shape, const),
            pl.BlockSpec((1, LANES), const),
        ],
        out_specs=[
            pl.BlockSpec((tm, D), lambda i: (i, 0)),
            pl.BlockSpec((tm * SUBLANES, LANES), lambda i: (i, 0)),
            pl.BlockSpec((TOP_K, tm), lambda i: (0, i)),
            pl.BlockSpec((TOP_K, tm), lambda i: (0, i)),
            pl.BlockSpec((TOP_K, tm), lambda i: (0, i)),
            pl.BlockSpec((1, N_EXPERTS, LANES), lambda i: (i, 0, 0)),
        ],
        out_shape=[
            jax.ShapeDtypeStruct((T, D), F32),
            jax.ShapeDtypeStruct((T * SUBLANES, LANES), F32),
            jax.ShapeDtypeStruct((TOP_K, T), jnp.int32),
            jax.ShapeDtypeStruct((TOP_K, T), jnp.int32),
            jax.ShapeDtypeStruct((TOP_K, T), F32),
            jax.ShapeDtypeStruct((n_tiles, N_EXPERTS, LANES), jnp.int32),
        ],
        compiler_params=_cparams(("arbitrary",)),
        name="out_router",
    )(mla, ret, x2, mod3, w_out_b, g_ffn, w_r, b_r)


def _row_copy(src, dst, sem):
    return pltpu.make_async_copy(src, dst, sem)


def _dispatch_kernel(dest_ref, hf_ref, w1_ref, w2_ref, xs_ref, g_ref, l_ref, w2b_ref, sem, *, td):
    def issue(t, carry):
        src = hf_ref.at[pl.ds(pl.multiple_of(t * SUBLANES, SUBLANES), SUBLANES), :]
        for k in range(TOP_K):
            d = dest_ref[t * TOP_K + k]
            dst = xs_ref.at[pl.ds(pl.multiple_of(d * SUBLANES, SUBLANES), SUBLANES), :]
            _row_copy(src, dst, sem).start(priority=k % 2)
        return carry

    lax.fori_loop(0, td, issue, 0, unroll=ISSUE_UNROLL)

    w = w1_ref[0].astype(BF16)
    sub = 2 * LANES
    r = lax.broadcasted_iota(jnp.int32, (sub, sub), 0)
    c = lax.broadcasted_iota(jnp.int32, (sub, sub), 1)
    src_col = jnp.where(c < LANES, 2 * c, 2 * (c - LANES) + 1)
    sel = (r == src_col).astype(F32).astype(BF16)
    for s in range(w.shape[1] // sub):
        t = jnp.dot(w[:, s * sub:(s + 1) * sub], sel, preferred_element_type=F32).astype(BF16)
        g_ref[0, :, s * LANES:(s + 1) * LANES] = t[:, :LANES]
        l_ref[0, :, s * LANES:(s + 1) * LANES] = t[:, LANES:]
    w2b_ref[0] = w2_ref[0].astype(BF16)

    for k in range(TOP_K):
        _row_copy(hf_ref, xs_ref.at[pl.ds(0, td * SUBLANES), :], sem).wait()


def _dispatch_prep(dest_flat, hf8, w1, w2, n_rows):
    T = hf8.shape[0] // SUBLANES
    E, d, f2 = w1.shape
    f = f2 // 2
    halves = 2
    n = E * halves
    assert T % (n * SUBLANES) == 0
    td = T // n
    half_w1 = jax.ShapeDtypeStruct((E, d, f), BF16)
    return pl.pallas_call(
        functools.partial(_dispatch_kernel, td=td),
        grid=(n,),
        in_specs=[
            pl.BlockSpec((td * TOP_K,), lambda i: (i,), memory_space=pltpu.SMEM),
            pl.BlockSpec((td * SUBLANES, LANES), lambda i: (i, 0)),
            pl.BlockSpec((1, d, f2 // halves), lambda i: (i // halves, 0, i % halves)),
            pl.BlockSpec((1, f // halves, d), lambda i: (i // halves, i % halves, 0)),
        ],
        out_specs=[
            pl.BlockSpec(memory_space=pl.ANY),
            pl.BlockSpec((1, d, f // halves), lambda i: (i // halves, 0, i % halves)),
            pl.BlockSpec((1, d, f // halves), lambda i: (i // halves, 0, i % halves)),
            pl.BlockSpec((1, f // halves, d), lambda i: (i // halves, i % halves, 0)),
        ],
        out_shape=[jax.ShapeDtypeStruct((n_rows * SUBLANES, LANES), F32), half_w1, half_w1,
                   jax.ShapeDtypeStruct(w2.shape, BF16)],
        scratch_shapes=[pltpu.SemaphoreType.DMA],
        compiler_params=_cparams(("arbitrary",)),
        name="dispatch_prep",
    )(dest_flat, hf8, w1, w2)


def _experts_kernel(ie_ref, ib_ref, lo_ref, hi_ref, first_ref, ni_ref, xs_ref, w1g_ref, w1l_ref,
                    w2_ref, b1g_ref, b1l_ref, b2_ref, ys_ref):
    i = pl.program_id(0)
    blk = xs_ref.shape[0] // SUBLANES
    d = w1g_ref.shape[1]

    @pl.when(i < ni_ref[0])
    def _():
        cols = [xs_ref[pl.ds(c, blk, stride=SUBLANES), :] for c in range(d // LANES)]
        x = jnp.concatenate(cols, axis=-1)
        row = lax.broadcasted_iota(jnp.int32, (blk, 1), 0)
        mine = (row >= lo_ref[i]) & (row < hi_ref[i])
        x = jnp.where(mine, x, 0.0).astype(BF16)
        hg = jnp.dot(x, w1g_ref[0], preferred_element_type=F32) + b1g_ref[0]
        hl = jnp.dot(x, w1l_ref[0], preferred_element_type=F32) + b1l_ref[0]
        glu = jnp.minimum(hg, SWIGLU_LIMIT)
        lin = jnp.clip(hl, -SWIGLU_LIMIT, SWIGLU_LIMIT)
        act = glu * _sigmoid(SWIGLU_ALPHA * glu) * (lin + 1.0)
        y = jnp.dot(act.astype(BF16), w2_ref[0], preferred_element_type=F32) + b2_ref[0]
        y = jnp.where(mine, y, 0.0)

        @pl.when(first_ref[i] == 1)
        def _():
            for c in range(d // LANES):
                ys_ref[pl.ds(c, blk, stride=SUBLANES), :] = y[:, c * LANES:(c + 1) * LANES]

        @pl.when(first_ref[i] == 0)
        def _():
            for c in range(d // LANES):
                ys_ref[pl.ds(c, blk, stride=SUBLANES), :] += y[:, c * LANES:(c + 1) * LANES]


def _experts(items, xs, w1g, w1l, w2, b1g, b1l, b2, blk):
    item_e, item_blk, item_lo, item_hi, item_first, n_items = items
    d = w1g.shape[1]
    f = w1g.shape[2]
    row_map = lambda i, ie, ib, lo, hi, fi, ni: (ib[i], 0)
    exp_map = lambda i, ie, ib, lo, hi, fi, ni: (ie[i], 0, 0)
    grid_spec = pltpu.PrefetchScalarGridSpec(
        num_scalar_prefetch=6,
        grid=(item_e.shape[0],),
        in_specs=[
            pl.BlockSpec((blk * SUBLANES, LANES), row_map),
            pl.BlockSpec((1, d, f), exp_map),
            pl.BlockSpec((1, d, f), exp_map),
            pl.BlockSpec((1, f, d), exp_map),
            pl.BlockSpec((1, 1, f), exp_map),
            pl.BlockSpec((1, 1, f), exp_map),
            pl.BlockSpec((1, 1, d), exp_map),
        ],
        out_specs=pl.BlockSpec((blk * SUBLANES, LANES), row_map),
    )
    return pl.pallas_call(
        _experts_kernel,
        grid_spec=grid_spec,
        out_shape=jax.ShapeDtypeStruct(xs.shape, F32),
        compiler_params=_cparams(("arbitrary",)),
        name="experts",
    )(item_e, item_blk, item_lo, item_hi, item_first, n_items, xs, w1g, w1l, w2, b1g, b1l, b2)


def _combine_kernel(dest_ref, dnext_ref, dnext2_ref, x1_ref, gate_ref, mod_ref, ys_ref, o_ref,
                    buf_ref, sems, *, tc):
    i = pl.program_id(0)
    n = pl.num_programs(0)
    slot = i % COMB_SLOTS
    n_groups = tc // SUBLANES

    def issue_token(idx_ref, s, t):
        for k in range(TOP_K):
            d = idx_ref[t * TOP_K + k]
            src = ys_ref.at[pl.ds(pl.multiple_of(d * SUBLANES, SUBLANES), SUBLANES), :]
            dst = buf_ref.at[s, k, pl.ds(pl.multiple_of(t * SUBLANES, SUBLANES), SUBLANES), :]
            _row_copy(src, dst, sems.at[s]).start(priority=k % 2)

    def gather_all(idx_ref, s):
        def body(t, carry):
            issue_token(idx_ref, s, t)
            return carry
        lax.fori_loop(0, tc, body, 0, unroll=ISSUE_UNROLL)

    def reduce_group(g):
        t0 = pl.multiple_of(g * SUBLANES, SUBLANES)
        gates = gate_ref[pl.ds(t0, SUBLANES), :]
        for c in range(o_ref.shape[1] // LANES):
            cs = slice(c * LANES, (c + 1) * LANES)
            rows = pl.ds(t0 * SUBLANES + c, SUBLANES, stride=SUBLANES)
            acc = gates[:, 0:1] * buf_ref[slot, 0, rows, :]
            for k in range(1, TOP_K):
                acc = acc + gates[:, k:k + 1] * buf_ref[slot, k, rows, :]
            o_ref[pl.ds(t0, SUBLANES), cs] = (x1_ref[pl.ds(t0, SUBLANES), cs]
                                              + mod_ref[0, 5:6, cs] * acc)

    @pl.when(i == 0)
    def _():
        gather_all(dest_ref, 0)

        @pl.when(n > 1)
        def _():
            gather_all(dnext_ref, 1)

    for k in range(TOP_K):
        _row_copy(ys_ref.at[pl.ds(0, tc * SUBLANES), :], buf_ref.at[slot, k], sems.at[slot]).wait()

    @pl.when(i + 2 < n)
    def _():
        nxt = (i + 2) % COMB_SLOTS

        def body(gg, carry):
            for u in range(COMB_GROUPS * SUBLANES):
                issue_token(dnext2_ref, nxt, gg * (COMB_GROUPS * SUBLANES) + u)
            for v in range(COMB_GROUPS):
                reduce_group(gg * COMB_GROUPS + v)
            return carry
        lax.fori_loop(0, n_groups // COMB_GROUPS, body, 0)

    @pl.when(i + 2 >= n)
    def _():
        def body(gg, carry):
            for v in range(COMB_GROUPS):
                reduce_group(gg * COMB_GROUPS + v)
            return carry
        lax.fori_loop(0, n_groups // COMB_GROUPS, body, 0)


def _combine(dest_flat, x1, gates_t, mod3, ys, L, tc):
    T, D = x1.shape
    per_b = L // tc
    n = T // tc
    ahead = lambda a: pl.BlockSpec((tc * TOP_K,), lambda i: (jnp.minimum(i + a, n - 1),),
                                   memory_space=pltpu.SMEM)
    return pl.pallas_call(
        functools.partial(_combine_kernel, tc=tc),
        grid=(n,),
        in_specs=[
            ahead(0), ahead(1), ahead(2),
            pl.BlockSpec((tc, D), lambda i: (i, 0)),
            pl.BlockSpec((tc, TOP_K), lambda i: (i, 0)),
            pl.BlockSpec((1, N_MOD, D), lambda i: (i // per_b, 0, 0)),
            pl.BlockSpec(memory_space=pl.ANY),
        ],
        out_specs=pl.BlockSpec((tc, D), lambda i: (i, 0)),
        out_shape=jax.ShapeDtypeStruct((T, D), F32),
        scratch_shapes=[pltpu.VMEM((COMB_SLOTS, TOP_K, tc * SUBLANES, LANES), F32),
                        pltpu.SemaphoreType.DMA((COMB_SLOTS,))],
        compiler_params=_cparams(("arbitrary",)),
        name="combine",
    )(dest_flat, dest_flat, dest_flat, x1, gates_t, mod3, ys)


def _rope_tables(L, dim, lane_off, width):
    rows = L // GRID_W
    nf = dim // 4
    inv = jnp.power(ROPE_BASE, -jnp.arange(nf, dtype=F32) / nf)
    row = jnp.repeat(jnp.arange(rows, dtype=F32), GRID_W)
    col = jnp.tile(jnp.arange(GRID_W, dtype=F32), rows)
    pos = jnp.stack([row, col], axis=-1)
    ang = pos[:, :, None] * inv
    ang = jnp.broadcast_to(ang[:, :, None, :], (L, 2, 2, nf)).reshape(L, dim)
    sign = jnp.where((jnp.arange(dim) % (dim // 2)) < nf, -1.0, 1.0).astype(F32)
    cos, sin = jnp.cos(ang), jnp.sin(ang) * sign
    if lane_off is None:
        reps = width // dim
        return jnp.tile(cos, (1, reps)), jnp.tile(sin, (1, reps))
    cfull = jnp.ones((L, width), F32).at[:, lane_off:lane_off + dim].set(cos)
    sfull = jnp.zeros((L, width), F32).at[:, lane_off:lane_off + dim].set(sin)
    return cfull, sfull


def _identity_tables(L):
    return jnp.ones((L, LANES), F32), jnp.zeros((L, LANES), F32)


def _half_rot_src(dim):
    j = jnp.arange(dim)
    return jnp.where((j % (dim // 2)) < dim // 4, j + dim // 4, j - dim // 4)


def _prep_weights(w_in, w_q_up, w_kv_up, g_q_head, g_k_head):
    D = w_in.shape[0]
    o = 0
    wq = w_in[:, o:o + Q_LORA]; o += Q_LORA
    wkv = w_in[:, o:o + KV_LORA]; o += KV_LORA
    wpe = w_in[:, o:o + MLA_ROPE]; o += MLA_ROPE
    n_qk = RET_HEADS * RET_DK
    wrq = w_in[:, o:o + n_qk]; o += n_qk
    wrk = w_in[:, o:o + n_qk]; o += n_qk
    rest = w_in[:, o:]
    src_m = _half_rot_src(MLA_ROPE)
    src_r = _half_rot_src(RET_DK)
    rope_lanes = slice(MLA_NOPE, MLA_NOPE + MLA_ROPE)
    pe_blk = jnp.zeros((D, LANES), w_in.dtype).at[:, rope_lanes].set(wpe)
    pe_perm = jnp.zeros((D, LANES), w_in.dtype).at[:, rope_lanes].set(wpe[:, src_m])
    perm_heads = lambda w: w.reshape(D, RET_HEADS, RET_DK)[:, :, src_r].reshape(D, n_qk)
    w_in_r = jnp.concatenate([wq, wkv, pe_blk, pe_perm, wrq, perm_heads(wrq), wrk, perm_heads(wrk),
                              rest], axis=1).astype(BF16)

    pad_h = LANES - MLA_QK
    wq3 = w_q_up.reshape(Q_LORA, MLA_HEADS, MLA_QK)
    q_main = jnp.pad(wq3, ((0, 0), (0, 0), (0, pad_h)))
    q_perm = jnp.zeros_like(q_main).at[:, :, rope_lanes].set(wq3[:, :, MLA_NOPE:][:, :, src_m])
    w_q_r = jnp.concatenate([q_main.reshape(Q_LORA, -1), q_perm.reshape(Q_LORA, -1)], axis=1).astype(BF16)

    kv = w_kv_up.reshape(KV_LORA, MLA_HEADS, MLA_NOPE + MLA_V)
    kpart = jnp.pad(kv[:, :, :MLA_NOPE], ((0, 0), (0, 0), (0, LANES - MLA_NOPE)))
    vpart = jnp.pad(kv[:, :, MLA_NOPE:], ((0, 0), (0, 0), (0, LANES - MLA_V)))
    w_kv_r = jnp.concatenate([kpart.reshape(KV_LORA, -1), vpart.reshape(KV_LORA, -1)], axis=1).astype(BF16)

    def gains(g):
        main = jnp.pad(g, (0, pad_h))[None, :]
        perm = jnp.zeros((1, LANES), g.dtype).at[0, rope_lanes].set(g[MLA_NOPE:][src_m])
        return main, perm

    gqh, gqr = gains(g_q_head)
    gkh, gkr = gains(g_k_head)
    return w_in_r, w_q_r, w_kv_r, gqh, gqr, gkh, gkr


def _routing_tables(idx, rank, counts, tm, blk):
    T = idx.shape[1]
    i32 = jnp.int32
    tot = jnp.sum(counts, axis=0)
    end = jnp.cumsum(tot)
    start = end - tot
    tile_base = start[None, :] + jnp.cumsum(counts, axis=0) - counts
    base_tok = jnp.repeat(tile_base, tm, axis=0)
    hit = idx[:, :, None] == jnp.arange(N_EXPERTS, dtype=i32)
    dest = jnp.sum(jnp.where(hit, base_tok[None], 0), axis=-1) + rank

    n_work = (T * TOP_K) // blk + N_EXPERTS
    first_blk = start // blk
    last_blk = (end - 1) // blk
    per_e = jnp.where(tot > 0, last_blk - first_blk + 1, 0)
    item_end = jnp.cumsum(per_e)
    item_start = item_end - per_e
    n_items = item_end[-1]
    j = jnp.minimum(jnp.arange(n_work, dtype=i32), n_items - 1)
    item_e = jnp.minimum(jnp.sum(item_end[None, :] <= j[:, None], axis=1), N_EXPERTS - 1).astype(i32)
    onehot = item_e[:, None] == jnp.arange(N_EXPERTS, dtype=i32)[None, :]
    pick = lambda tab: jnp.sum(jnp.where(onehot, tab[None, :], 0), axis=1)
    item_blk = pick(first_blk) + j - pick(item_start)
    item_lo = jnp.clip(pick(start) - item_blk * blk, 0, blk)
    item_hi = jnp.clip(pick(end) - item_blk * blk, 0, blk)
    prev_blk = jnp.concatenate([jnp.full((1,), -1, i32), item_blk[:-1].astype(i32)])
    item_first = (item_blk != prev_blk).astype(i32)
    items = (item_e, item_blk.astype(i32), item_lo.astype(i32), item_hi.astype(i32), item_first,
             n_items.astype(i32).reshape(1))
    return dest.astype(i32), items


def kernel(x, c, ctx, c_ctx, g_attn, g_ffn, w_ada, b_ada, w_in, g_q_lora, w_q_up, g_q_head,
           g_kv_lora, w_kv_up, g_k_head, ret_decay_logit, g_ret_out, w_out, w_router, b_router,
           w_mlp1, b_mlp1, w_mlp2, b_mlp2):
    B, L, D = x.shape
    Lc = ctx.shape[1]
    T = B * L
    l = 0
    assert w_ada.shape[0] == 1

    rows = ((B + 1 + SUBLANES - 1) // SUBLANES) * SUBLANES
    cc = jnp.zeros((rows, D), F32).at[:B].set(c).at[B].set(c_ctx)
    mod3 = _adaln(cc, w_ada[l], b_ada[l][None, :]).reshape(rows, N_MOD, D)

    w_in_r, w_q_r, w_kv_r, gqh, gqr, gkh, gkr = _prep_weights(
        w_in[l], w_q_up[l], w_kv_up[l], g_q_head[l], g_k_head[l])
    tabs_x = _rope_tables(L, MLA_ROPE, MLA_NOPE, LANES) + _rope_tables(L, RET_DK, None, LANES)
    tabs_c = _identity_tables(Lc) + _identity_tables(Lc)
    proj = functools.partial(_in_proj, g_attn=g_attn[l][None, :], w_in_r=w_in_r,
                             g_q_lora=g_q_lora[l][None, :], w_q_r=w_q_r, gqh=gqh, gqr=gqr,
                             g_kv_lora=g_kv_lora[l][None, :], w_kv_r=w_kv_r, gkh=gkh, gkr=gkr)
    q, kx, vx, rq, rk, rv, rg = proj(x, mod3, lambda b: b, tabs=tabs_x, tm=min(PROJ_TM, L))
    _, kc, vc, _, rk_c, rv_c, _ = proj(ctx, mod3, lambda b: B, tabs=tabs_c, tm=Lc)

    mla = _attention(q, kx, vx, kc, vc, min(ATT_TQ, L), min(ATT_TK, L // 2))

    log_gamma = jax.nn.log_sigmoid(ret_decay_logit[l].astype(F32))
    ret = _retention(log_gamma, g_ret_out[l][None, :], rq, rk, rv, rg, rk_c, rv_c,
                     min(RET_C, L), min(RET_ROWS, L))

    w_r32 = jnp.pad(w_router[l], ((0, 0), (0, LANES - N_EXPERTS)))
    w_r_hi = w_r32.astype(BF16)
    w_r = jnp.concatenate([w_r_hi, (w_r32 - w_r_hi.astype(F32)).astype(BF16)], axis=1)
    b_r = jnp.pad(b_router[l], (0, LANES - N_EXPERTS))[None, :]
    tm = min(OUT_TM, L)
    x1, hf8, idx, rank, gates, cnt = _out_router(
        mla.reshape(T, -1), ret.reshape(T, -1), x.reshape(T, D), mod3,
        w_out[l].astype(BF16), g_ffn[l][None, :], w_r, b_r, L, tm)

    blk = MOE_BLK
    assert (T * TOP_K) % blk == 0
    dest, items = _routing_tables(idx, rank, cnt[:, :, 0], tm, blk)
    dest_flat = dest.T.reshape(-1)

    xs, w1g, w1l, w2b = _dispatch_prep(dest_flat, hf8, w_mlp1[l], w_mlp2[l], T * TOP_K)
    b1g = b_mlp1[l][:, None, 0::2]
    b1l = b_mlp1[l][:, None, 1::2]
    ys = _experts(items, xs, w1g, w1l, w2b, b1g, b1l, b_mlp2[l][:, None, :], blk)

    out = _combine(dest_flat, x1, gates.T, mod3, ys, L, min(COMB_T, L))
    return out.reshape(B, L, D)
```

```python
import functools

import jax
import jax.numpy as jnp
from jax import lax
from jax.experimental import pallas as pl
from jax.experimental.pallas import tpu as pltpu

F32 = jnp.float32
BF16 = jnp.bfloat16

LANES = 128
SUBLANES = 8
VMEM_LIMIT_BYTES = 56 * 1024 * 1024

EPS = 1e-6
ROPE_BASE = 10000.0
GRID_W = 64
N_MOD = 6
MLA_HEADS = 8
MLA_NOPE = 64
MLA_ROPE = 32
MLA_QK = MLA_NOPE + MLA_ROPE
MLA_V = 64
Q_LORA = 256
KV_LORA = 128
RET_HEADS = 4
RET_DK = 64
RET_DV = 128
N_EXPERTS = 32
TOP_K = 4
SWIGLU_LIMIT = 7.0
SWIGLU_ALPHA = 1.702
LOG2E = 1.4426950408889634

ADALN_TN = 1536
PROJ_TM = 512
ATT_TQ = 512
ATT_TK = 1024
RET_C = 256
RET_ROWS = 2048
OUT_TM = 512
MOE_BLK = 512
COMB_T = 256
COMB_SLOTS = 3
COMB_GROUPS = 4
ISSUE_UNROLL = 8


def _cparams(sem):
    return pltpu.CompilerParams(dimension_semantics=sem, vmem_limit_bytes=VMEM_LIMIT_BYTES)


def _sigmoid(x):
    return 1.0 / (1.0 + jnp.exp(-x))


def _adaln_kernel(c_ref, w_ref, b_ref, o_ref):
    c = c_ref[...]
    s = (c * _sigmoid(c)).astype(BF16)
    o_ref[...] = jnp.dot(s, w_ref[...].astype(BF16), preferred_element_type=F32) + b_ref[...]


def _adaln(cc, w_ada, b_ada):
    rows, d = cc.shape
    n = w_ada.shape[1]
    tn = ADALN_TN
    return pl.pallas_call(
        _adaln_kernel,
        grid=(n // tn,),
        in_specs=[pl.BlockSpec((rows, d), lambda j: (0, 0)),
                  pl.BlockSpec((d, tn), lambda j: (0, j)),
                  pl.BlockSpec((1, tn), lambda j: (0, j))],
        out_specs=pl.BlockSpec((rows, tn), lambda j: (0, j)),
        out_shape=jax.ShapeDtypeStruct((rows, n), F32),
        compiler_params=_cparams(("arbitrary",)),
        name="adaln",
    )(cc, w_ada, b_ada)


def _in_proj_kernel(x_ref, mod_ref, gattn_ref, win_ref, gql_ref, wq_ref, gqh_ref, gqr_ref,
                    gkvl_ref, wkv_ref, gkh_ref, gkr_ref, cm_ref, sm_ref, cr_ref, sr_ref,
                    q_ref, k_ref, v_ref, rq_ref, rk_ref, rv_ref, rg_ref, *, q_scale):
    x = x_ref[0]
    shift = mod_ref[0, 0:1, :]
    scale = mod_ref[0, 1:2, :]
    ms = jnp.mean(x * x, axis=-1, keepdims=True)
    h = x * lax.rsqrt(ms + EPS) * gattn_ref[...]
    h = h * (1.0 + scale) + shift
    p = jnp.dot(h.astype(BF16), win_ref[...], preferred_element_type=F32)

    lane = lax.broadcasted_iota(jnp.int32, (1, LANES), 1)
    cm, sm = cm_ref[...], sm_ref[...]
    cr, sr = cr_ref[...], sr_ref[...]
    n_hl = MLA_HEADS * LANES

    cq = p[:, 0:Q_LORA]
    cq = cq * lax.rsqrt(jnp.mean(cq * cq, axis=-1, keepdims=True) + EPS) * gql_ref[...]
    qf = jnp.dot(cq.astype(BF16), wq_ref[...], preferred_element_type=F32)
    gq_cos = gqh_ref[...] * cm
    gq_sin = gqr_ref[...] * sm
    for hd in range(MLA_HEADS):
        blk = qf[:, hd * LANES:(hd + 1) * LANES]
        perm = qf[:, n_hl + hd * LANES:n_hl + (hd + 1) * LANES]
        r = lax.rsqrt(jnp.sum(blk * blk, axis=-1, keepdims=True) * (1.0 / MLA_QK) + EPS)
        q_ref[0, hd] = ((blk * gq_cos + perm * gq_sin) * (r * q_scale)).astype(BF16)

    o_kv = Q_LORA
    ckv = p[:, o_kv:o_kv + KV_LORA]
    ckv = ckv * lax.rsqrt(jnp.mean(ckv * ckv, axis=-1, keepdims=True) + EPS) * gkvl_ref[...]
    kvf = jnp.dot(ckv.astype(BF16), wkv_ref[...], preferred_element_type=F32)
    o_pe = o_kv + KV_LORA
    pe = p[:, o_pe:o_pe + LANES]
    pe_perm = p[:, o_pe + LANES:o_pe + 2 * LANES]
    gk = gkh_ref[...]
    pe_rope = pe * (gk * cm) + pe_perm * (gkr_ref[...] * sm)
    pe_ss = jnp.sum(pe * pe, axis=-1, keepdims=True)
    ones_col = (lane == MLA_V).astype(F32)
    for hd in range(MLA_HEADS):
        kn = kvf[:, hd * LANES:(hd + 1) * LANES]
        ss = jnp.sum(kn * kn, axis=-1, keepdims=True) + pe_ss
        r = lax.rsqrt(ss * (1.0 / MLA_QK) + EPS)
        kk = (kn * gk + pe_rope) * r
        k_ref[0, hd] = kk.T.astype(BF16)
        vv = kvf[:, (MLA_HEADS + hd) * LANES:(MLA_HEADS + hd + 1) * LANES] + ones_col
        v_ref[0, hd] = vv.astype(BF16)

    n_qk = RET_HEADS * RET_DK
    o_rq = o_pe + 2 * LANES
    o_rk = o_rq + 2 * n_qk
    for j in range(n_qk // LANES):
        js = slice(j * LANES, (j + 1) * LANES)
        a = p[:, o_rq + j * LANES:o_rq + (j + 1) * LANES]
        b = p[:, o_rq + n_qk + j * LANES:o_rq + n_qk + (j + 1) * LANES]
        rq_ref[0, :, js] = (a * cr + b * sr).astype(BF16)
        a = p[:, o_rk + j * LANES:o_rk + (j + 1) * LANES]
        b = p[:, o_rk + n_qk + j * LANES:o_rk + n_qk + (j + 1) * LANES]
        rk_ref[0, :, js] = ((a * cr + b * sr) * (RET_DK ** -0.5)).astype(BF16)
    o_rv = o_rk + 2 * n_qk
    n_v = RET_HEADS * RET_DV
    rv_ref[0] = p[:, o_rv:o_rv + n_v].astype(BF16)
    rg_ref[0] = p[:, o_rv + n_v:o_rv + 2 * n_v]


def _in_proj(x, mod3, mod_row_of_batch, g_attn, w_in_r, g_q_lora, w_q_r, gqh, gqr, g_kv_lora,
             w_kv_r, gkh, gkr, tabs, tm):
    B, L, D = x.shape
    cm, sm, cr, sr = tabs
    n_in = w_in_r.shape[1]
    const = lambda b, i: (0, 0)
    tab_spec = pl.BlockSpec((tm, LANES), lambda b, i: (i, 0))
    head_spec = pl.BlockSpec((1, MLA_HEADS, tm, LANES), lambda b, i: (b, 0, i, 0))
    n_qk = RET_HEADS * RET_DK
    n_v = RET_HEADS * RET_DV
    seq_spec = lambda w: pl.BlockSpec((1, tm, w), lambda b, i: (b, i, 0))
    head_shape = jax.ShapeDtypeStruct((B, MLA_HEADS, L, LANES), BF16)
    q_scale = MLA_QK ** -0.5 * LOG2E
    return pl.pallas_call(
        functools.partial(_in_proj_kernel, q_scale=q_scale),
        grid=(B, L // tm),
        in_specs=[
            pl.BlockSpec((1, tm, D), lambda b, i: (b, i, 0)),
            pl.BlockSpec((1, N_MOD, D), lambda b, i: (mod_row_of_batch(b), 0, 0)),
            pl.BlockSpec((1, D), const),
            pl.BlockSpec((D, n_in), const),
            pl.BlockSpec((1, Q_LORA), const),
            pl.BlockSpec(w_q_r.shape, const),
            pl.BlockSpec((1, LANES), const),
            pl.BlockSpec((1, LANES), const),
            pl.BlockSpec((1, KV_LORA), const),
            pl.BlockSpec(w_kv_r.shape, const),
            pl.BlockSpec((1, LANES), const),
            pl.BlockSpec((1, LANES), const),
            tab_spec, tab_spec, tab_spec, tab_spec,
        ],
        out_specs=[head_spec,
                   pl.BlockSpec((1, MLA_HEADS, LANES, tm), lambda b, i: (b, 0, 0, i)),
                   head_spec,
                   seq_spec(n_qk), seq_spec(n_qk), seq_spec(n_v), seq_spec(n_v)],
        out_shape=[head_shape,
                   jax.ShapeDtypeStruct((B, MLA_HEADS, LANES, L), BF16),
                   head_shape,
                   jax.ShapeDtypeStruct((B, L, n_qk), BF16),
                   jax.ShapeDtypeStruct((B, L, n_qk), BF16),
                   jax.ShapeDtypeStruct((B, L, n_v), BF16),
                   jax.ShapeDtypeStruct((B, L, n_v), F32)],
        compiler_params=_cparams(("arbitrary", "arbitrary")),
        name="in_proj",
    )(x, mod3, g_attn, w_in_r, g_q_lora, w_q_r, gqh, gqr, g_kv_lora, w_kv_r, gkh, gkr,
      cm, sm, cr, sr)


def _attn_kernel(q_ref, kx_ref, vx_ref, kc_ref, vc_ref, o_ref, m_ref, acc_ref, s_ref, *, tk):
    n_kv = kx_ref.shape[3] // tk
    n_heads = q_ref.shape[1]

    def scores(hh, start, slot):
        kb = kx_ref[0, hh, :, pl.ds(start, tk)]
        s_ref[hh, slot] = jnp.dot(q_ref[0, hh], kb, preferred_element_type=F32)

    def consume(hh, start, slot):
        s = s_ref[hh, slot]
        m_old = m_ref[hh]
        m_new = jnp.maximum(m_old, jnp.max(s, axis=-1, keepdims=True))
        alpha = jnp.exp2(m_old - m_new)
        pr = jnp.exp2(s - m_new)
        vb = vx_ref[0, hh, pl.ds(start, tk), :]
        acc_ref[hh] = alpha * acc_ref[hh] + jnp.dot(pr.astype(BF16), vb,
                                                    preferred_element_type=F32)
        m_ref[hh] = m_new

    def step(j, slot, prefetch):
        if prefetch:
            nxt = pl.multiple_of((j + 1) * tk, tk)
            for hh in range(n_heads):
                scores(hh, nxt, 1 - slot)
        cur = pl.multiple_of(j * tk, tk)
        for hh in range(n_heads):
            consume(hh, cur, slot)

    for hh in range(n_heads):
        scores(hh, 0, 0)
        s = jnp.dot(q_ref[0, hh], kc_ref[0, hh], preferred_element_type=F32)
        m0 = jnp.max(s, axis=-1, keepdims=True)
        m_ref[hh] = m0
        acc_ref[hh] = jnp.dot(jnp.exp2(s - m0).astype(BF16), vc_ref[0, hh],
                              preferred_element_type=F32)

    def body(jj, carry):
        step(2 * jj, 0, True)
        step(2 * jj + 1, 1, True)
        return carry

    lax.fori_loop(0, n_kv // 2 - 1, body, 0)
    step(n_kv - 2, 0, True)
    step(n_kv - 1, 1, False)
    outs = []
    for hh in range(n_heads):
        acc = acc_ref[hh]
        outs.append(acc[:, :MLA_V] / acc[:, MLA_V:MLA_V + 1])
    o_ref[0] = jnp.concatenate(outs, axis=-1).astype(BF16)


def _attention(q, kx, vx, kc, vc, tq, tk):
    B, H, L, _ = q.shape
    Lc = kc.shape[3]
    return pl.pallas_call(
        functools.partial(_attn_kernel, tk=tk),
        grid=(B, H // 2, L // tq),
        in_specs=[
            pl.BlockSpec((1, 2, tq, LANES), lambda b, h, i: (b, h, i, 0)),
            pl.BlockSpec((1, 2, LANES, L), lambda b, h, i: (b, h, 0, 0)),
            pl.BlockSpec((1, 2, L, LANES), lambda b, h, i: (b, h, 0, 0)),
            pl.BlockSpec((1, 2, LANES, Lc), lambda b, h, i: (b, h, 0, 0)),
            pl.BlockSpec((1, 2, Lc, LANES), lambda b, h, i: (b, h, 0, 0)),
        ],
        out_specs=pl.BlockSpec((1, tq, LANES), lambda b, h, i: (b, i, h)),
        out_shape=jax.ShapeDtypeStruct((B, L, H * MLA_V), BF16),
        scratch_shapes=[pltpu.VMEM((2, tq, 1), F32), pltpu.VMEM((2, tq, LANES), F32),
                        pltpu.VMEM((2, 2, tq, tk), F32)],
        compiler_params=_cparams(("arbitrary", "arbitrary", "arbitrary")),
        name="attention",
    )(q, kx, vx, kc, vc)


def _ret_kernel(lg_ref, lgf_ref, lgb_ref, lgvf_ref, lgvb_ref, gout_ref, rq_ref, rk_ref, rv_ref, rg_ref,
                kc_ref, vc_ref, o_ref, dm_ref, qdf_ref, qdb_ref, kdf_ref, kdb_ref,
                f_ref, r_ref, rs_ref, *, n_steps, C):
    n_sub = rq_ref.shape[1] // C
    Lc = kc_ref.shape[1]
    ps = pl.program_id(1)
    i = pl.program_id(2)
    n_pairs = RET_HEADS // 2
    pw = 2 * RET_DK
    vw = 2 * RET_DV
    tdn = (((0,), (0,)), ((), ()))
    ndn = (((1,), (1,)), ((), ()))
    lgf = lgf_ref[...]
    lgb = lgb_ref[...]

    @pl.when((pl.program_id(0) == 0) & (ps == 0) & (i == 0))
    def _tables():
        a = lax.broadcasted_iota(jnp.int32, (C, C), 0)
        b = lax.broadcasted_iota(jnp.int32, (C, C), 1)
        dab = (a - b).astype(F32)
        for hd in range(RET_HEADS):
            fwd = jnp.where(a >= b, jnp.exp(jnp.where(a >= b, dab, 0.0) * lg_ref[0, hd]), 0.0)
            bwd = jnp.where(b >= a, jnp.exp(jnp.where(b >= a, -dab, 0.0) * lg_ref[1, hd]), 0.0)
            dm_ref[hd] = fwd + bwd
        row = lax.broadcasted_iota(jnp.int32, (C, 1), 0).astype(F32)
        qdf_ref[...] = jnp.exp((row + 1.0) * lgf)
        qdb_ref[...] = jnp.exp((C - row) * lgb)
        kdf_ref[...] = jnp.exp((C - 1.0 - row) * lgf)
        kdb_ref[...] = jnp.exp(row * lgb)

    @pl.when((ps == 0) & (i == 0))
    def _init_states():
        rowc = lax.broadcasted_iota(jnp.int32, (Lc, 1), 0).astype(F32)
        wf = jnp.exp((Lc - 1.0 - rowc) * lgf)
        wb = jnp.exp(rowc * lgb)
        kc = kc_ref[0].astype(F32)
        for pr in range(n_pairs):
            kp = kc[:, pr * pw:(pr + 1) * pw]
            vp = vc_ref[0, :, pr * vw:(pr + 1) * vw]
            f_ref[pr] = lax.dot_general((kp * wf[:, pr * pw:(pr + 1) * pw]).astype(BF16), vp, tdn,
                                        preferred_element_type=F32)
            r_ref[pr] = lax.dot_general((kp * wb[:, pr * pw:(pr + 1) * pw]).astype(BF16), vp, tdn,
                                        preferred_element_type=F32)

    @pl.when(ps == 0)
    def _backward_states():
        cdb = jnp.exp(C * lgvb_ref[...])
        for cc in reversed(range(n_sub)):
            rows = slice(cc * C, (cc + 1) * C)
            c = (n_steps - 1 - i) * n_sub + cc
            k = rk_ref[0, rows, :].astype(F32)
            for pr in range(n_pairs):
                r_old = r_ref[pr]
                rs_ref[c, pr] = r_old.astype(BF16)
                kp = (k[:, pr * pw:(pr + 1) * pw] * kdb_ref[:, pr * pw:(pr + 1) * pw]).astype(BF16)
                vp = rv_ref[0, rows, pr * vw:(pr + 1) * vw]
                upd = lax.dot_general(kp, vp, tdn, preferred_element_type=F32)
                r_ref[pr] = r_old * cdb[:, pr * vw:(pr + 1) * vw] + upd

    @pl.when(ps == 1)
    def _forward_outputs():
        cdf = jnp.exp(C * lgvf_ref[...])
        lane = lax.broadcasted_iota(jnp.int32, (1, pw), 1)
        for cc in range(n_sub):
            rows = slice(cc * C, (cc + 1) * C)
            c = i * n_sub + cc
            q = rq_ref[0, rows, :].astype(F32)
            k = rk_ref[0, rows, :].astype(F32)
            for pr in range(n_pairs):
                sl = slice(pr * pw, (pr + 1) * pw)
                qp = q[:, sl]
                kpb = rk_ref[0, rows, sl]
                qf = qp * qdf_ref[:, sl]
                qb = qp * qdb_ref[:, sl]
                fb = f_ref[pr].astype(BF16)
                rb = rs_ref[c, pr]
                for hh in range(2):
                    hd = 2 * pr + hh
                    hm = (lane // RET_DK) == hh
                    vs = slice(hd * RET_DV, (hd + 1) * RET_DV)
                    fs = slice(hh * RET_DV, (hh + 1) * RET_DV)
                    a = lax.dot_general(jnp.where(hm, qp, 0.0).astype(BF16), kpb, ndn,
                                        preferred_element_type=F32)
                    a = (a * dm_ref[hd]).astype(BF16)
                    o = jnp.dot(a, rv_ref[0, rows, vs], preferred_element_type=F32)
                    o = o + jnp.dot(jnp.where(hm, qf, 0.0).astype(BF16), fb[:, fs],
                                    preferred_element_type=F32)
                    o = o + jnp.dot(jnp.where(hm, qb, 0.0).astype(BF16), rb[:, fs],
                                    preferred_element_type=F32)
                    o = o * lax.rsqrt(jnp.mean(o * o, axis=-1, keepdims=True) + EPS) * gout_ref[:, vs]
                    g = rg_ref[0, rows, vs]
                    o_ref[0, rows, vs] = (o * (g * _sigmoid(g))).astype(BF16)
                kp = (k[:, sl] * kdf_ref[:, sl]).astype(BF16)
                vp = rv_ref[0, rows, pr * vw:(pr + 1) * vw]
                upd = lax.dot_general(kp, vp, tdn, preferred_element_type=F32)
                f_ref[pr] = f_ref[pr] * cdf[:, pr * vw:(pr + 1) * vw] + upd


def _retention(log_gamma, g_ret_out, rq, rk, rv, rg, rk_c, rv_c, C, R):
    B, L, n_qk = rq.shape
    n_v = rv.shape[2]
    Lc = rk_c.shape[1]
    n = L // R
    lgf = jnp.repeat(log_gamma[0], RET_DK)[None, :]
    lgb = jnp.repeat(log_gamma[1], RET_DK)[None, :]
    lgvf = jnp.repeat(log_gamma[0], RET_DV)[None, :]
    lgvb = jnp.repeat(log_gamma[1], RET_DV)[None, :]
    chunk = lambda p, i: jnp.where(p == 0, n - 1 - i, i)
    fwd_only = lambda p, i: jnp.where(p == 0, 0, i)
    const2 = lambda b, p, i: (0, 0)
    n_pairs = RET_HEADS // 2
    return pl.pallas_call(
        functools.partial(_ret_kernel, n_steps=n, C=C),
        grid=(B, 2, n),
        in_specs=[
            pl.BlockSpec(memory_space=pltpu.SMEM),
            pl.BlockSpec((1, n_qk), const2),
            pl.BlockSpec((1, n_qk), const2),
            pl.BlockSpec((1, n_v), const2),
            pl.BlockSpec((1, n_v), const2),
            pl.BlockSpec((1, n_v), const2),
            pl.BlockSpec((1, R, n_qk), lambda b, p, i: (b, fwd_only(p, i), 0)),
            pl.BlockSpec((1, R, n_qk), lambda b, p, i: (b, chunk(p, i), 0)),
            pl.BlockSpec((1, R, n_v), lambda b, p, i: (b, chunk(p, i), 0)),
            pl.BlockSpec((1, R, n_v), lambda b, p, i: (b, fwd_only(p, i), 0)),
            pl.BlockSpec((1, Lc, n_qk), lambda b, p, i: (b, 0, 0)),
            pl.BlockSpec((1, Lc, n_v), lambda b, p, i: (b, 0, 0)),
        ],
        out_specs=pl.BlockSpec((1, R, n_v), lambda b, p, i: (b, fwd_only(p, i), 0)),
        out_shape=jax.ShapeDtypeStruct((B, L, n_v), BF16),
        scratch_shapes=[
            pltpu.VMEM((RET_HEADS, C, C), F32),
            pltpu.VMEM((C, n_qk), F32), pltpu.VMEM((C, n_qk), F32),
            pltpu.VMEM((C, n_qk), F32), pltpu.VMEM((C, n_qk), F32),
            pltpu.VMEM((n_pairs, 2 * RET_DK, 2 * RET_DV), F32),
            pltpu.VMEM((n_pairs, 2 * RET_DK, 2 * RET_DV), F32),
            pltpu.VMEM((L // C, n_pairs, 2 * RET_DK, 2 * RET_DV), BF16),
        ],
        compiler_params=_cparams(("arbitrary", "arbitrary", "arbitrary")),
        name="retention",
    )(log_gamma, lgf, lgb, lgvf, lgvb, g_ret_out, rq, rk, rv, rg, rk_c, rv_c)


def _out_router_kernel(mla_ref, ret_ref, x_ref, mod_ref, wo_ref, gffn_ref, wr_ref, br_ref,
                       x1_ref, hf_ref, idx_ref, rank_ref, gate_ref, cnt_ref):
    tm = x_ref.shape[0]
    n_mla = mla_ref.shape[1]
    y = jnp.dot(mla_ref[...], wo_ref[0:n_mla, :], preferred_element_type=F32)
    y = y + jnp.dot(ret_ref[...], wo_ref[n_mla:, :], preferred_element_type=F32)
    x1 = x_ref[...] + mod_ref[0, 2:3, :] * y
    x1_ref[...] = x1
    ms = jnp.mean(x1 * x1, axis=-1, keepdims=True)
    hf = x1 * lax.rsqrt(ms + EPS) * gffn_ref[...]
    hf = hf * (1.0 + mod_ref[0, 4:5, :]) + mod_ref[0, 3:4, :]
    for c in range(hf.shape[1] // LANES):
        hf_ref[pl.ds(c, tm, stride=SUBLANES), :] = hf[:, c * LANES:(c + 1) * LANES]

    hi = hf.astype(BF16)
    lo = (hf - hi.astype(F32)).astype(BF16)
    both = jnp.dot(hi, wr_ref[...], preferred_element_type=F32)
    logits = (both[:, :LANES] + both[:, LANES:]
              + jnp.dot(lo, wr_ref[:, :LANES], preferred_element_type=F32)) + br_ref[...]
    lt = logits.T[0:N_EXPERTS, :]
    e_iota = lax.broadcasted_iota(jnp.int32, (N_EXPERTS, tm), 0).astype(F32)
    vals, idxs, hits = [], [], []
    for _ in range(TOP_K):
        mx = jnp.max(lt, axis=0, keepdims=True)
        ix = jnp.min(jnp.where(lt == mx, e_iota, float(N_EXPERTS)), axis=0, keepdims=True)
        hit = e_iota == ix
        lt = jnp.where(hit, -jnp.inf, lt)
        vals.append(mx)
        idxs.append(ix)
        hits.append(hit.astype(F32))

    s_iota = lax.broadcasted_iota(jnp.int32, (tm, tm), 0)
    t_iota = lax.broadcasted_iota(jnp.int32, (tm, tm), 1)
    upper = (s_iota <= t_iota).astype(F32).astype(BF16)
    prefix = jnp.dot(jnp.concatenate(hits, axis=0).astype(BF16), upper,
                     preferred_element_type=F32)
    ranks = []
    seen = jnp.zeros((N_EXPERTS, 1), F32)
    for k in range(TOP_K):
        pk = prefix[k * N_EXPERTS:(k + 1) * N_EXPERTS, :]
        rank = jnp.sum(hits[k] * (pk - 1.0 + seen), axis=0, keepdims=True)
        seen = seen + jnp.sum(hits[k], axis=1, keepdims=True)
        ranks.append(rank.astype(jnp.int32))
    ex = [jnp.exp(v - vals[0]) for v in vals]
    den = ex[0] + ex[1] + ex[2] + ex[3]
    idx_ref[...] = jnp.concatenate(idxs, axis=0).astype(jnp.int32)
    rank_ref[...] = jnp.concatenate(ranks, axis=0)
    gate_ref[...] = jnp.concatenate([e / den for e in ex], axis=0)
    cnt_ref[0] = jnp.broadcast_to(seen, (N_EXPERTS, LANES)).astype(jnp.int32)


def _out_router(mla, ret, x2, mod3, w_out_b, g_ffn, w_r, b_r, L, tm):
    T, D = x2.shape
    n_tiles = T // tm
    per_b = L // tm
    const = lambda i: (0, 0)
    return pl.pallas_call(
        _out_router_kernel,
        grid=(n_tiles,),
        in_specs=[
            pl.BlockSpec((tm, mla.shape[1]), lambda i: (i, 0)),
            pl.BlockSpec((tm, ret.shape[1]), lambda i: (i, 0)),
            pl.BlockSpec((tm, D), lambda i: (i, 0)),
            pl.BlockSpec((1, N_MOD, D), lambda i: (i // per_b, 0, 0)),
            pl.BlockSpec(w_out_b.shape, const),
            pl.BlockSpec((1, D), const),
            pl.BlockSpec(w_r.shape, const),
            pl.BlockSpec((1, LANES), const),
        ],
        out_specs=[
            pl.BlockSpec((tm, D), lambda i: (i, 0)),
            pl.BlockSpec((tm * SUBLANES, LANES), lambda i: (i, 0)),
            pl.BlockSpec((TOP_K, tm), lambda i: (0, i)),
            pl.BlockSpec((TOP_K, tm), lambda i: (0, i)),
            pl.BlockSpec((TOP_K, tm), lambda i: (0, i)),
            pl.BlockSpec((1, N_EXPERTS, LANES), lambda i: (i, 0, 0)),
        ],
        out_shape=[
            jax.ShapeDtypeStruct((T, D), F32),
            jax.ShapeDtypeStruct((T * SUBLANES, LANES), F32),
            jax.ShapeDtypeStruct((TOP_K, T), jnp.int32),
            jax.ShapeDtypeStruct((TOP_K, T), jnp.int32),
            jax.ShapeDtypeStruct((TOP_K, T), F32),
            jax.ShapeDtypeStruct((n_tiles, N_EXPERTS, LANES), jnp.int32),
        ],
        compiler_params=_cparams(("arbitrary",)),
        name="out_router",
    )(mla, ret, x2, mod3, w_out_b, g_ffn, w_r, b_r)


def _row_copy(src, dst, sem):
    return pltpu.make_async_copy(src, dst, sem)


def _dispatch_kernel(dest_ref, hf_ref, w1_ref, w2_ref, xs_ref, g_ref, l_ref, w2b_ref, sem, *, td):
    def issue(t, carry):
        src = hf_ref.at[pl.ds(pl.multiple_of(t * SUBLANES, SUBLANES), SUBLANES), :]
        for k in range(TOP_K):
            d = dest_ref[t * TOP_K + k]
            dst = xs_ref.at[pl.ds(pl.multiple_of(d * SUBLANES, SUBLANES), SUBLANES), :]
            _row_copy(src, dst, sem).start(priority=k % 2)
        return carry

    lax.fori_loop(0, td, issue, 0, unroll=ISSUE_UNROLL)

    w = w1_ref[0].astype(BF16)
    sub = 2 * LANES
    r = lax.broadcasted_iota(jnp.int32, (sub, sub), 0)
    c = lax.broadcasted_iota(jnp.int32, (sub, sub), 1)
    src_col = jnp.where(c < LANES, 2 * c, 2 * (c - LANES) + 1)
    sel = (r == src_col).astype(F32).astype(BF16)
    for s in range(w.shape[1] // sub):
        t = jnp.dot(w[:, s * sub:(s + 1) * sub], sel, preferred_element_type=F32).astype(BF16)
        g_ref[0, :, s * LANES:(s + 1) * LANES] = t[:, :LANES]
        l_ref[0, :, s * LANES:(s + 1) * LANES] = t[:, LANES:]
    w2b_ref[0] = w2_ref[0].astype(BF16)

    for k in range(TOP_K):
        _row_copy(hf_ref, xs_ref.at[pl.ds(0, td * SUBLANES), :], sem).wait()


def _dispatch_prep(dest_flat, hf8, w1, w2, n_rows):
    T = hf8.shape[0] // SUBLANES
    E, d, f2 = w1.shape
    f = f2 // 2
    halves = 1
    n = E * halves
    assert T % (n * SUBLANES) == 0
    td = T // n
    half_w1 = jax.ShapeDtypeStruct((E, d, f), BF16)
    return pl.pallas_call(
        functools.partial(_dispatch_kernel, td=td),
        grid=(n,),
        in_specs=[
            pl.BlockSpec((td * TOP_K,), lambda i: (i,), memory_space=pltpu.SMEM),
            pl.BlockSpec((td * SUBLANES, LANES), lambda i: (i, 0)),
            pl.BlockSpec((1, d, f2 // halves), lambda i: (i // halves, 0, i % halves)),
            pl.BlockSpec((1, f // halves, d), lambda i: (i // halves, i % halves, 0)),
        ],
        out_specs=[
            pl.BlockSpec(memory_space=pl.ANY),
            pl.BlockSpec((1, d, f // halves), lambda i: (i // halves, 0, i % halves)),
            pl.BlockSpec((1, d, f // halves), lambda i: (i // halves, 0, i % halves)),
            pl.BlockSpec((1, f // halves, d), lambda i: (i // halves, i % halves, 0)),
        ],
        out_shape=[jax.ShapeDtypeStruct((n_rows * SUBLANES, LANES), F32), half_w1, half_w1,
                   jax.ShapeDtypeStruct(w2.shape, BF16)],
        scratch_shapes=[pltpu.SemaphoreType.DMA],
        compiler_params=_cparams(("arbitrary",)),
        name="dispatch_prep",
    )(dest_flat, hf8, w1, w2)


def _experts_kernel(ie_ref, ib_ref, lo_ref, hi_ref, first_ref, ni_ref, xs_ref, w1g_ref, w1l_ref,
                    w2_ref, b1g_ref, b1l_ref, b2_ref, ys_ref):
    i = pl.program_id(0)
    blk = xs_ref.shape[0] // SUBLANES
    d = w1g_ref.shape[1]

    @pl.when(i < ni_ref[0])
    def _():
        cols = [xs_ref[pl.ds(c, blk, stride=SUBLANES), :] for c in range(d // LANES)]
        x = jnp.concatenate(cols, axis=-1)
        row = lax.broadcasted_iota(jnp.int32, (blk, 1), 0)
        mine = (row >= lo_ref[i]) & (row < hi_ref[i])
        x = jnp.where(mine, x, 0.0).astype(BF16)
        hg = jnp.dot(x, w1g_ref[0], preferred_element_type=F32) + b1g_ref[0]
        hl = jnp.dot(x, w1l_ref[0], preferred_element_type=F32) + b1l_ref[0]
        glu = jnp.minimum(hg, SWIGLU_LIMIT)
        lin = jnp.clip(hl, -SWIGLU_LIMIT, SWIGLU_LIMIT)
        act = glu * _sigmoid(SWIGLU_ALPHA * glu) * (lin + 1.0)
        y = jnp.dot(act.astype(BF16), w2_ref[0], preferred_element_type=F32) + b2_ref[0]
        y = jnp.where(mine, y, 0.0)

        @pl.when(first_ref[i] == 1)
        def _():
            for c in range(d // LANES):
                ys_ref[pl.ds(c, blk, stride=SUBLANES), :] = y[:, c * LANES:(c + 1) * LANES]

        @pl.when(first_ref[i] == 0)
        def _():
            for c in range(d // LANES):
                ys_ref[pl.ds(c, blk, stride=SUBLANES), :] += y[:, c * LANES:(c + 1) * LANES]


def _experts(items, xs, w1g, w1l, w2, b1g, b1l, b2, blk):
    item_e, item_blk, item_lo, item_hi, item_first, n_items = items
    d = w1g.shape[1]
    f = w1g.shape[2]
    row_map = lambda i, ie, ib, lo, hi, fi, ni: (ib[i], 0)
    exp_map = lambda i, ie, ib, lo, hi, fi, ni: (ie[i], 0, 0)
    grid_spec = pltpu.PrefetchScalarGridSpec(
        num_scalar_prefetch=6,
        grid=(item_e.shape[0],),
        in_specs=[
            pl.BlockSpec((blk * SUBLANES, LANES), row_map),
            pl.BlockSpec((1, d, f), exp_map),
            pl.BlockSpec((1, d, f), exp_map),
            pl.BlockSpec((1, f, d), exp_map),
            pl.BlockSpec((1, 1, f), exp_map),
            pl.BlockSpec((1, 1, f), exp_map),
            pl.BlockSpec((1, 1, d), exp_map),
        ],
        out_specs=pl.BlockSpec((blk * SUBLANES, LANES), row_map),
    )
    return pl.pallas_call(
        _experts_kernel,
        grid_spec=grid_spec,
        out_shape=jax.ShapeDtypeStruct(xs.shape, F32),
        compiler_params=_cparams(("arbitrary",)),
        name="experts",
    )(item_e, item_blk, item_lo, item_hi, item_first, n_items, xs, w1g, w1l, w2, b1g, b1l, b2)


def _combine_kernel(dest_ref, dnext_ref, dnext2_ref, x1_ref, gate_ref, mod_ref, ys_ref, o_ref,
                    buf_ref, sems, *, tc):
    i = pl.program_id(0)
    n = pl.num_programs(0)
    slot = i % COMB_SLOTS
    n_groups = tc // SUBLANES

    def issue_token(idx_ref, s, t):
        for k in range(TOP_K):
            d = idx_ref[t * TOP_K + k]
            src = ys_ref.at[pl.ds(pl.multiple_of(d * SUBLANES, SUBLANES), SUBLANES), :]
            dst = buf_ref.at[s, k, pl.ds(pl.multiple_of(t * SUBLANES, SUBLANES), SUBLANES), :]
            _row_copy(src, dst, sems.at[s]).start(priority=k % 2)

    def gather_all(idx_ref, s):
        def body(t, carry):
            issue_token(idx_ref, s, t)
            return carry
        lax.fori_loop(0, tc, body, 0, unroll=ISSUE_UNROLL)

    def reduce_group(g):
        t0 = pl.multiple_of(g * SUBLANES, SUBLANES)
        gates = gate_ref[pl.ds(t0, SUBLANES), :]
        for c in range(o_ref.shape[1] // LANES):
            cs = slice(c * LANES, (c + 1) * LANES)
            rows = pl.ds(t0 * SUBLANES + c, SUBLANES, stride=SUBLANES)
            acc = gates[:, 0:1] * buf_ref[slot, 0, rows, :]
            for k in range(1, TOP_K):
                acc = acc + gates[:, k:k + 1] * buf_ref[slot, k, rows, :]
            o_ref[pl.ds(t0, SUBLANES), cs] = (x1_ref[pl.ds(t0, SUBLANES), cs]
                                              + mod_ref[0, 5:6, cs] * acc)

    @pl.when(i == 0)
    def _():
        gather_all(dest_ref, 0)

        @pl.when(n > 1)
        def _():
            gather_all(dnext_ref, 1)

    for k in range(TOP_K):
        _row_copy(ys_ref.at[pl.ds(0, tc * SUBLANES), :], buf_ref.at[slot, k], sems.at[slot]).wait()

    @pl.when(i + 2 < n)
    def _():
        nxt = (i + 2) % COMB_SLOTS

        def body(gg, carry):
            for u in range(COMB_GROUPS * SUBLANES):
                issue_token(dnext2_ref, nxt, gg * (COMB_GROUPS * SUBLANES) + u)
            for v in range(COMB_GROUPS):
                reduce_group(gg * COMB_GROUPS + v)
            return carry
        lax.fori_loop(0, n_groups // COMB_GROUPS, body, 0)

    @pl.when(i + 2 >= n)
    def _():
        def body(gg, carry):
            for v in range(COMB_GROUPS):
                reduce_group(gg * COMB_GROUPS + v)
            return carry
        lax.fori_loop(0, n_groups // COMB_GROUPS, body, 0)


def _combine(dest_flat, x1, gates_t, mod3, ys, L, tc):
    T, D = x1.shape
    per_b = L // tc
    n = T // tc
    ahead = lambda a: pl.BlockSpec((tc * TOP_K,), lambda i: (jnp.minimum(i + a, n - 1),),
                                   memory_space=pltpu.SMEM)
    return pl.pallas_call(
        functools.partial(_combine_kernel, tc=tc),
        grid=(n,),
        in_specs=[
            ahead(0), ahead(1), ahead(2),
            pl.BlockSpec((tc, D), lambda i: (i, 0)),
            pl.BlockSpec((tc, TOP_K), lambda i: (i, 0)),
            pl.BlockSpec((1, N_MOD, D), lambda i: (i // per_b, 0, 0)),
            pl.BlockSpec(memory_space=pl.ANY),
        ],
        out_specs=pl.BlockSpec((tc, D), lambda i: (i, 0)),
        out_shape=jax.ShapeDtypeStruct((T, D), F32),
        scratch_shapes=[pltpu.VMEM((COMB_SLOTS, TOP_K, tc * SUBLANES, LANES), F32),
                        pltpu.SemaphoreType.DMA((COMB_SLOTS,))],
        compiler_params=_cparams(("arbitrary",)),
        name="combine",
    )(dest_flat, dest_flat, dest_flat, x1, gates_t, mod3, ys)


def _rope_tables(L, dim, lane_off, width):
    rows = L // GRID_W
    nf = dim // 4
    inv = jnp.power(ROPE_BASE, -jnp.arange(nf, dtype=F32) / nf)
    row = jnp.repeat(jnp.arange(rows, dtype=F32), GRID_W)
    col = jnp.tile(jnp.arange(GRID_W, dtype=F32), rows)
    pos = jnp.stack([row, col], axis=-1)
    ang = pos[:, :, None] * inv
    ang = jnp.broadcast_to(ang[:, :, None, :], (L, 2, 2, nf)).reshape(L, dim)
    sign = jnp.where((jnp.arange(dim) % (dim // 2)) < nf, -1.0, 1.0).astype(F32)
    cos, sin = jnp.cos(ang), jnp.sin(ang) * sign
    if lane_off is None:
        reps = width // dim
        return jnp.tile(cos, (1, reps)), jnp.tile(sin, (1, reps))
    cfull = jnp.ones((L, width), F32).at[:, lane_off:lane_off + dim].set(cos)
    sfull = jnp.zeros((L, width), F32).at[:, lane_off:lane_off + dim].set(sin)
    return cfull, sfull


def _identity_tables(L):
    return jnp.ones((L, LANES), F32), jnp.zeros((L, LANES), F32)


def _half_rot_src(dim):
    j = jnp.arange(dim)
    return jnp.where((j % (dim // 2)) < dim // 4, j + dim // 4, j - dim // 4)


def _prep_weights(w_in, w_q_up, w_kv_up, g_q_head, g_k_head):
    D = w_in.shape[0]
    o = 0
    wq = w_in[:, o:o + Q_LORA]; o += Q_LORA
    wkv = w_in[:, o:o + KV_LORA]; o += KV_LORA
    wpe = w_in[:, o:o + MLA_ROPE]; o += MLA_ROPE
    n_qk = RET_HEADS * RET_DK
    wrq = w_in[:, o:o + n_qk]; o += n_qk
    wrk = w_in[:, o:o + n_qk]; o += n_qk
    rest = w_in[:, o:]
    src_m = _half_rot_src(MLA_ROPE)
    src_r = _half_rot_src(RET_DK)
    rope_lanes = slice(MLA_NOPE, MLA_NOPE + MLA_ROPE)
    pe_blk = jnp.zeros((D, LANES), w_in.dtype).at[:, rope_lanes].set(wpe)
    pe_perm = jnp.zeros((D, LANES), w_in.dtype).at[:, rope_lanes].set(wpe[:, src_m])
    perm_heads = lambda w: w.reshape(D, RET_HEADS, RET_DK)[:, :, src_r].reshape(D, n_qk)
    w_in_r = jnp.concatenate([wq, wkv, pe_blk, pe_perm, wrq, perm_heads(wrq), wrk, perm_heads(wrk),
                              rest], axis=1).astype(BF16)

    pad_h = LANES - MLA_QK
    wq3 = w_q_up.reshape(Q_LORA, MLA_HEADS, MLA_QK)
    q_main = jnp.pad(wq3, ((0, 0), (0, 0), (0, pad_h)))
    q_perm = jnp.zeros_like(q_main).at[:, :, rope_lanes].set(wq3[:, :, MLA_NOPE:][:, :, src_m])
    w_q_r = jnp.concatenate([q_main.reshape(Q_LORA, -1), q_perm.reshape(Q_LORA, -1)], axis=1).astype(BF16)

    kv = w_kv_up.reshape(KV_LORA, MLA_HEADS, MLA_NOPE + MLA_V)
    kpart = jnp.pad(kv[:, :, :MLA_NOPE], ((0, 0), (0, 0), (0, LANES - MLA_NOPE)))
    vpart = jnp.pad(kv[:, :, MLA_NOPE:], ((0, 0), (0, 0), (0, LANES - MLA_V)))
    w_kv_r = jnp.concatenate([kpart.reshape(KV_LORA, -1), vpart.reshape(KV_LORA, -1)], axis=1).astype(BF16)

    def gains(g):
        main = jnp.pad(g, (0, pad_h))[None, :]
        perm = jnp.zeros((1, LANES), g.dtype).at[0, rope_lanes].set(g[MLA_NOPE:][src_m])
        return main, perm

    gqh, gqr = gains(g_q_head)
    gkh, gkr = gains(g_k_head)
    return w_in_r, w_q_r, w_kv_r, gqh, gqr, gkh, gkr


def _routing_tables(idx, rank, counts, tm, blk):
    T = idx.shape[1]
    i32 = jnp.int32
    tot = jnp.sum(counts, axis=0)
    end = jnp.cumsum(tot)
    start = end - tot
    tile_base = start[None, :] + jnp.cumsum(counts, axis=0) - counts
    base_tok = jnp.repeat(tile_base, tm, axis=0)
    hit = idx[:, :, None] == jnp.arange(N_EXPERTS, dtype=i32)
    dest = jnp.sum(jnp.where(hit, base_tok[None], 0), axis=-1) + rank

    n_work = (T * TOP_K) // blk + N_EXPERTS
    first_blk = start // blk
    last_blk = (end - 1) // blk
    per_e = jnp.where(tot > 0, last_blk - first_blk + 1, 0)
    item_end = jnp.cumsum(per_e)
    item_start = item_end - per_e
    n_items = item_end[-1]
    j = jnp.minimum(jnp.arange(n_work, dtype=i32), n_items - 1)
    item_e = jnp.minimum(jnp.sum(item_end[None, :] <= j[:, None], axis=1), N_EXPERTS - 1).astype(i32)
    onehot = item_e[:, None] == jnp.arange(N_EXPERTS, dtype=i32)[None, :]
    pick = lambda tab: jnp.sum(jnp.where(onehot, tab[None, :], 0), axis=1)
    item_blk = pick(first_blk) + j - pick(item_start)
    item_lo = jnp.clip(pick(start) - item_blk * blk, 0, blk)
    item_hi = jnp.clip(pick(end) - item_blk * blk, 0, blk)
    prev_blk = jnp.concatenate([jnp.full((1,), -1, i32), item_blk[:-1].astype(i32)])
    item_first = (item_blk != prev_blk).astype(i32)
    items = (item_e, item_blk.astype(i32), item_lo.astype(i32), item_hi.astype(i32), item_first,
             n_items.astype(i32).reshape(1))
    return dest.astype(i32), items


def kernel(x, c, ctx, c_ctx, g_attn, g_ffn, w_ada, b_ada, w_in, g_q_lora, w_q_up, g_q_head,
           g_kv_lora, w_kv_up, g_k_head, ret_decay_logit, g_ret_out, w_out, w_router, b_router,
           w_mlp1, b_mlp1, w_mlp2, b_mlp2):
    B, L, D = x.shape
    Lc = ctx.shape[1]
    T = B * L
    l = 0
    assert w_ada.shape[0] == 1

    rows = ((B + 1 + SUBLANES - 1) // SUBLANES) * SUBLANES
    cc = jnp.zeros((rows, D), F32).at[:B].set(c).at[B].set(c_ctx)
    mod3 = _adaln(cc, w_ada[l], b_ada[l][None, :]).reshape(rows, N_MOD, D)

    w_in_r, w_q_r, w_kv_r, gqh, gqr, gkh, gkr = _prep_weights(
        w_in[l], w_q_up[l], w_kv_up[l], g_q_head[l], g_k_head[l])
    tabs_x = _rope_tables(L, MLA_ROPE, MLA_NOPE, LANES) + _rope_tables(L, RET_DK, None, LANES)
    tabs_c = _identity_tables(Lc) + _identity_tables(Lc)
    proj = functools.partial(_in_proj, g_attn=g_attn[l][None, :], w_in_r=w_in_r,
                             g_q_lora=g_q_lora[l][None, :], w_q_r=w_q_r, gqh=gqh, gqr=gqr,
                             g_kv_lora=g_kv_lora[l][None, :], w_kv_r=w_kv_r, gkh=gkh, gkr=gkr)
    q, kx, vx, rq, rk, rv, rg = proj(x, mod3, lambda b: b, tabs=tabs_x, tm=min(PROJ_TM, L))
    _, kc, vc, _, rk_c, rv_c, _ = proj(ctx, mod3, lambda b: B, tabs=tabs_c, tm=Lc)

    mla = _attention(q, kx, vx, kc, vc, min(ATT_TQ, L), min(ATT_TK, L // 2))

    log_gamma = jax.nn.log_sigmoid(ret_decay_logit[l].astype(F32))
    ret = _retention(log_gamma, g_ret_out[l][None, :], rq, rk, rv, rg, rk_c, rv_c,
                     min(RET_C, L), min(RET_ROWS, L))

    w_r32 = jnp.pad(w_router[l], ((0, 0), (0, LANES - N_EXPERTS)))
    w_r_hi = w_r32.astype(BF16)
    w_r = jnp.concatenate([w_r_hi, (w_r32 - w_r_hi.astype(F32)).astype(BF16)], axis=1)
    b_r = jnp.pad(b_router[l], (0, LANES - N_EXPERTS))[None, :]
    tm = min(OUT_TM, L)
    x1, hf8, idx, rank, gates, cnt = _out_router(
        mla.reshape(T, -1), ret.reshape(T, -1), x.reshape(T, D), mod3,
        w_out[l].astype(BF16), g_ffn[l][None, :], w_r, b_r, L, tm)

    blk = MOE_BLK
    assert (T * TOP_K) % blk == 0
    dest, items = _routing_tables(idx, rank, cnt[:, :, 0], tm, blk)
    dest_flat = dest.T.reshape(-1)

    xs, w1g, w1l, w2b = _dispatch_prep(dest_flat, hf8, w_mlp1[l], w_mlp2[l], T * TOP_K)
    b1g = b_mlp1[l][:, None, 0::2]
    b1l = b_mlp1[l][:, None, 1::2]
    ys = _experts(items, xs, w1g, w1l, w2b, b1g, b1l, b_mlp2[l][:, None, :], blk)

    out = _combine(dest_flat, x1, gates.T, mod3, ys, L, min(COMB_T, L))
    return out.reshape(B, L, D)
```
